```python
import math
import jax, jax.numpy as jnp
from jax import lax
import numpy as np

D_MODEL = 2048
BATCH = 4
SEQ = 4096
DEPTH = 1

HG_HEADS = 8
HG_KDIM = 128
HG_VDIM = 128
HG_WIDTH = HG_HEADS * HG_VDIM
HG_FDIM = HG_HEADS * HG_KDIM
HG_CHUNK = 64

NSA_HEADS = 16
NSA_GROUPS = 4
NSA_HPG = NSA_HEADS // NSA_GROUPS
NSA_DH = 64
NSA_WIDTH = NSA_HEADS * NSA_DH
NSA_KV = NSA_GROUPS * NSA_DH
CMP_LEN = 32
CMP_STRIDE = 16
CMP_HIDDEN = 256
SLC_LEN = 64
SLC_TOPK = 16
WIN = 512
NSA_QBLOCK = 64

MIX_WIDTH = HG_WIDTH + NSA_WIDTH

REL_BUCKETS = 32
REL_MAX_DIST = 128

N_EXPERT_GROUPS = 8
EXPERTS_PER_GROUP = 8
N_EXPERTS = N_EXPERT_GROUPS * EXPERTS_PER_GROUP
EXPERT_TOPK = 2
EXPERT_DFF = 1024
MOE_BLOCK = 128

RMS_EPS = 1e-6
NEG_INF = -1e30
BIG = 1e9

IN_SIZES = (HG_FDIM, HG_FDIM, HG_WIDTH, HG_WIDTH,
            NSA_WIDTH, NSA_KV, NSA_KV, NSA_KV, NSA_KV, NSA_KV, NSA_KV, 3 * NSA_HEADS)
IN_COLS = 2 * HG_FDIM + 2 * HG_WIDTH + NSA_WIDTH + 6 * NSA_KV + 3 * NSA_HEADS

kernel_name = "hybrid_hgrn2_nsa_hmoe_block"


def rmsnorm(x, g):
    xf = x.astype(jnp.float32)
    y = xf * lax.rsqrt(jnp.mean(xf * xf, axis=-1, keepdims=True) + RMS_EPS)
    return (y * g.astype(jnp.float32)).astype(x.dtype)


def masked_softmax(logits, mask):
    logits = jnp.where(mask, logits.astype(jnp.float32), NEG_INF)
    p = jax.nn.softmax(logits, axis=-1)
    return p * mask.astype(p.dtype)


def rel_bucket(dist):
    n = jnp.maximum(dist, 0)
    max_exact = REL_BUCKETS // 2
    nf = jnp.maximum(n, 1).astype(jnp.float32)
    large = max_exact + (jnp.log(nf / max_exact) / math.log(REL_MAX_DIST / max_exact)
                         * (REL_BUCKETS - max_exact)).astype(jnp.int32)
    large = jnp.minimum(large, REL_BUCKETS - 1)
    return jnp.where(n < max_exact, n, large)


def hgrn2_mixer(q, f_logit, i_in, g_out, lower_bound, norm_g):
    B, S = q.shape[0], q.shape[1]
    nc = S // HG_CHUNK
    f32 = jnp.float32
    shp_k = (B, S, HG_HEADS, HG_KDIM)
    shp_v = (B, S, HG_HEADS, HG_VDIM)
    q = jax.nn.silu(q.reshape(shp_k).astype(f32))
    lb = lower_bound.reshape(HG_HEADS, HG_KDIM)
    f = lb + (1.0 - lb) * jax.nn.sigmoid(f_logit.reshape(shp_k).astype(f32))
    log_f = jnp.log(f)
    k = 1.0 - f
    v = i_in.reshape(shp_v).astype(f32)

    def chunks(t):
        return t.reshape(B, nc, HG_CHUNK, HG_HEADS, t.shape[-1]).transpose(1, 0, 3, 2, 4)

    causal = jnp.tril(jnp.ones((HG_CHUNK, HG_CHUNK), dtype=bool))[:, :, None]

    def step(state, inp):
        qc, kc, vc, lfc = inp
        b = jnp.cumsum(lfc, axis=2)
        decay = jnp.exp(jnp.where(causal, b[:, :, :, None, :] - b[:, :, None, :, :], NEG_INF))
        scores = jnp.einsum('bhtd,bhsd,bhtsd->bhts', qc, kc, decay)
        o = (jnp.einsum('bhts,bhsv->bhtv', scores, vc)
             + jnp.einsum('bhtd,bhdv->bhtv', qc * jnp.exp(b), state))
        b_last = b[:, :, -1:, :]
        state = (jnp.exp(b_last[:, :, 0, :, None]) * state
                 + jnp.einsum('bhsd,bhsv->bhdv', kc * jnp.exp(b_last - b), vc))
        return state, o

    s0 = jnp.zeros((B, HG_HEADS, HG_KDIM, HG_VDIM), f32)
    _, o = lax.scan(step, s0, (chunks(q), chunks(k), chunks(v), chunks(log_f)))
    o = o.transpose(1, 0, 3, 2, 4).reshape(shp_v)
    o = o * lax.rsqrt(jnp.mean(o * o, axis=-1, keepdims=True) + RMS_EPS) * norm_g.astype(f32)
    o = o * jax.nn.silu(g_out.reshape(shp_v).astype(f32))
    return o.reshape(B, S, HG_WIDTH)


def nsa_mixer(q, k_cmp, v_cmp, k_slc, v_slc, k_win, v_win, gate_logits,
              cmp_pe_k, cmp_w1_k, cmp_w2_k, cmp_pe_v, cmp_w1_v, cmp_w2_v, rel_bias):
    B, S = q.shape[0], q.shape[1]
    f32 = jnp.float32
    C = NSA_QBLOCK
    nq = S // C
    n_cmp = (S - CMP_LEN) // CMP_STRIDE + 1
    n_slc = S // SLC_LEN
    top_k = min(SLC_TOPK, n_slc)
    L = top_k * SLC_LEN

    def kv(t):
        return t.reshape(B, S, NSA_GROUPS, NSA_DH)

    cmp_idx = np.arange(n_cmp)[:, None] * CMP_STRIDE + np.arange(CMP_LEN)[None, :]
    cmp_end = jnp.asarray(cmp_idx[:, -1], jnp.int32)

    def compress(t, pe, w1, w2):
        blk = kv(t)[:, cmp_idx] + pe[None, None, :, None, :]
        blk = blk.transpose(0, 3, 1, 2, 4).reshape(B, NSA_GROUPS, n_cmp, CMP_LEN * NSA_DH)
        return jax.nn.silu(blk @ w1) @ w2

    kc = compress(k_cmp, cmp_pe_k, cmp_w1_k, cmp_w2_k)
    vc = compress(v_cmp, cmp_pe_v, cmp_w1_v, cmp_w2_v)

    ci = np.arange(n_cmp)[:, None] * CMP_STRIDE
    sj = np.arange(n_slc)[None, :] * SLC_LEN
    overlap = jnp.asarray(((ci <= sj + SLC_LEN - 1) & (ci + CMP_LEN - 1 >= sj)).astype(np.float32))

    def blocks(t):
        return kv(t).reshape(B, n_slc, SLC_LEN, NSA_GROUPS, NSA_DH).transpose(0, 3, 1, 2, 4)

    ks_blk = blocks(k_slc)
    vs_blk = blocks(v_slc)

    def pad_win(t):
        return jnp.pad(kv(t), ((0, 0), (WIN, 0), (0, 0), (0, 0))).transpose(0, 2, 1, 3)

    kw_pad = pad_win(k_win)
    vw_pad = pad_win(v_win)

    table = rel_bias.T.reshape(NSA_GROUPS, NSA_HPG, REL_BUCKETS)
    ii = np.arange(C)[:, None]
    jj = np.arange(WIN + C)[None, :]
    dist_w = WIN + ii - jj
    band = jnp.asarray((dist_w >= 0) & (dist_w < WIN))
    bias_w = table[:, :, rel_bucket(jnp.asarray(dist_w, jnp.int32))]
    g_idx = jnp.arange(NSA_GROUPS)[None, :, None, None, None]
    h_idx = jnp.arange(NSA_HPG)[None, None, :, None, None]
    gather = jax.vmap(jax.vmap(lambda blk, ix: blk[ix]))

    qb = (q.reshape(B, nq, C, NSA_GROUPS, NSA_HPG, NSA_DH) * NSA_DH ** -0.5).transpose(1, 0, 3, 4, 2, 5)
    gb = jax.nn.sigmoid(gate_logits.astype(f32)).reshape(
        B, nq, C, 3, NSA_GROUPS, NSA_HPG).transpose(1, 0, 4, 5, 2, 3)

    def query_block(args):
        c, qc, gc = args
        t_pos = c * C + jnp.arange(C, dtype=jnp.int32)
        s = (jnp.einsum('bghcd,bgnd->bghcn', qc, kc).astype(f32)
             + table[:, :, rel_bucket(t_pos[:, None] - cmp_end[None, :])])
        p_cmp = masked_softmax(s, cmp_end[None, :] <= t_pos[:, None])
        o_cmp = jnp.einsum('bghcn,bgnd->bghcd', p_cmp, vc)
        imp = jnp.einsum('bgcn,nj->bgcj', p_cmp.sum(axis=2), overlap)
        q_blk = t_pos[:, None] // SLC_LEN
        j = jnp.arange(n_slc)[None, :]
        forced = (j == 0) | (j == q_blk) | (j == q_blk - 1)
        imp = jnp.where(forced, BIG, jnp.where(j > q_blk, -BIG, imp))
        _, sel = lax.top_k(imp, top_k)
        k_sel = gather(ks_blk, sel).reshape(B, NSA_GROUPS, C, L, NSA_DH)
        v_sel = gather(vs_blk, sel).reshape(B, NSA_GROUPS, C, L, NSA_DH)
        pos_sel = (sel[..., None] * SLC_LEN + jnp.arange(SLC_LEN, dtype=jnp.int32)).reshape(B, NSA_GROUPS, C, L)
        dist = t_pos[:, None] - pos_sel
        s = (jnp.einsum('bghcd,bgcld->bghcl', qc, k_sel).astype(f32)
             + table[g_idx, h_idx, rel_bucket(dist)[:, :, None]])
        p = masked_softmax(s, (dist >= 0)[:, :, None])
        o_slc = jnp.einsum('bghcl,bgcld->bghcd', p, v_sel)
        kw = lax.dynamic_slice_in_dim(kw_pad, c * C, WIN + C, axis=2)
        vw = lax.dynamic_slice_in_dim(vw_pad, c * C, WIN + C, axis=2)
        valid = band & (jnp.arange(WIN + C, dtype=jnp.int32)[None, :] >= WIN - c * C)
        s = jnp.einsum('bghcd,bgld->bghcl', qc, kw).astype(f32) + bias_w
        p = masked_softmax(s, valid)
        o_win = jnp.einsum('bghcl,bgld->bghcd', p, vw)
        return gc[..., 0:1] * o_cmp + gc[..., 1:2] * o_slc + gc[..., 2:3] * o_win

    o = lax.map(query_block, (jnp.arange(nq, dtype=jnp.int32), qb, gb))
    return o.transpose(1, 0, 4, 2, 3, 5).reshape(B, S, NSA_WIDTH)


def hier_moe(h, w_rg, b_rg, w_re, b_re, w_gate, w_up, w_down):
    B, S, D = h.shape
    T = B * S
    A = T * EXPERT_TOPK
    f32 = jnp.float32
    xt = h.reshape(T, D)
    p_grp = jax.nn.softmax((xt @ w_rg).astype(f32) + b_rg.astype(f32), axis=-1)
    p_top, g_top = lax.top_k(p_grp, 1)
    le = ((xt @ w_re).astype(f32) + b_re.astype(f32)).reshape(T, N_EXPERT_GROUPS, EXPERTS_PER_GROUP)
    le = le[jnp.arange(T), g_top[:, 0]]
    top_v, top_i = lax.top_k(le, EXPERT_TOPK)
    gate = p_top * jax.nn.softmax(top_v, axis=-1)
    e_flat = (g_top * EXPERTS_PER_GROUP + top_i).reshape(A)
    tok_flat = jnp.arange(A, dtype=jnp.int32) // EXPERT_TOPK
    w_flat = gate.reshape(A)
    order = jnp.argsort(e_flat)
    e_s, tok_s, w_s = e_flat[order], tok_flat[order], w_flat[order]
    counts = jnp.bincount(e_flat, length=N_EXPERTS)
    offs = jnp.cumsum(counts) - counts
    padded = (counts + MOE_BLOCK - 1) // MOE_BLOCK * MOE_BLOCK
    pend = jnp.cumsum(padded)
    poffs = pend - padded
    dest = poffs[e_s] + (jnp.arange(A, dtype=jnp.int32) - offs[e_s])
    P = (A + MOE_BLOCK - 1) // MOE_BLOCK * MOE_BLOCK + N_EXPERTS * MOE_BLOCK
    n_blocks = P // MOE_BLOCK
    tok_buf = jnp.zeros((P,), jnp.int32).at[dest].set(tok_s)
    w_buf = jnp.zeros((P,), h.dtype).at[dest].set(w_s.astype(h.dtype))
    blk_start = jnp.arange(n_blocks, dtype=jnp.int32) * MOE_BLOCK
    blk_expert = jnp.minimum(jnp.searchsorted(pend, blk_start, side='right'), N_EXPERTS - 1)
    x_buf = xt[tok_buf].reshape(n_blocks, MOE_BLOCK, D)

    def expert_block(args):
        xb, e = args
        hid = jax.nn.silu(xb @ w_gate[e]) * (xb @ w_up[e])
        return hid @ w_down[e]

    y = lax.map(expert_block, (x_buf, blk_expert)).reshape(P, D)
    out = jnp.zeros((T, D), h.dtype).at[tok_buf].add((y * w_buf[:, None]).astype(h.dtype))
    return out.reshape(B, S, D)


def setup_inputs(seed: int = 0) -> dict:
    key = jax.random.key(seed)
    ks = jax.random.split(key, 24)
    f32 = jnp.float32

    def nrm(k, shape, scale):
        return jax.random.normal(k, shape, f32) * scale

    def gain(k, shape):
        return 1.0 + 0.02 * jax.random.normal(k, shape, f32)

    cmp_in = CMP_LEN * NSA_DH
    return {
        "x": nrm(ks[0], (BATCH, SEQ, D_MODEL), 1.0),
        "norm1_g": gain(ks[1], (DEPTH, D_MODEL)),
        "w_in": nrm(ks[2], (DEPTH, D_MODEL, IN_COLS), D_MODEL ** -0.5),
        "hg_lb_logits": nrm(ks[3], (DEPTH + 1, HG_FDIM), 0.5),
        "hg_norm_g": gain(ks[4], (DEPTH, HG_VDIM)),
        "cmp_pe_k": nrm(ks[5], (DEPTH, CMP_LEN, NSA_DH), 0.1),
        "cmp_w1_k": nrm(ks[6], (DEPTH, cmp_in, CMP_HIDDEN), cmp_in ** -0.5),
        "cmp_w2_k": nrm(ks[7], (DEPTH, CMP_HIDDEN, NSA_DH), CMP_HIDDEN ** -0.5),
        "cmp_pe_v": nrm(ks[8], (DEPTH, CMP_LEN, NSA_DH), 0.1),
        "cmp_w1_v": nrm(ks[9], (DEPTH, cmp_in, CMP_HIDDEN), cmp_in ** -0.5),
        "cmp_w2_v": nrm(ks[10], (DEPTH, CMP_HIDDEN, NSA_DH), CMP_HIDDEN ** -0.5),
        "rel_bias": nrm(ks[11], (REL_BUCKETS, NSA_HEADS), 0.5),
        "w_out": nrm(ks[12], (DEPTH, MIX_WIDTH, D_MODEL), MIX_WIDTH ** -0.5),
        "norm2_g": gain(ks[13], (DEPTH, D_MODEL)),
        "w_router_group": nrm(ks[14], (DEPTH, D_MODEL, N_EXPERT_GROUPS), D_MODEL ** -0.5),
        "b_router_group": nrm(ks[15], (DEPTH, N_EXPERT_GROUPS), 0.01),
        "w_router_expert": nrm(ks[16], (DEPTH, D_MODEL, N_EXPERTS), D_MODEL ** -0.5),
        "b_router_expert": nrm(ks[17], (DEPTH, N_EXPERTS), 0.01),
        "w_expert_gate": nrm(ks[18], (DEPTH, N_EXPERTS, D_MODEL, EXPERT_DFF), D_MODEL ** -0.5),
        "w_expert_up": nrm(ks[19], (DEPTH, N_EXPERTS, D_MODEL, EXPERT_DFF), D_MODEL ** -0.5),
        "w_expert_down": nrm(ks[20], (DEPTH, N_EXPERTS, EXPERT_DFF, D_MODEL), EXPERT_DFF ** -0.5),
        "final_norm_g": gain(ks[21], (D_MODEL,)),
    }


def reference(x, norm1_g, w_in, hg_lb_logits, hg_norm_g, cmp_pe_k, cmp_w1_k, cmp_w2_k,
              cmp_pe_v, cmp_w1_v, cmp_w2_v, rel_bias, w_out, norm2_g, w_router_group,
              b_router_group, w_router_expert, b_router_expert, w_expert_gate, w_expert_up,
              w_expert_down, final_norm_g):
    lower_bounds = jnp.cumsum(jax.nn.softmax(hg_lb_logits.astype(jnp.float32), axis=0), axis=0)
    split_at = np.cumsum(IN_SIZES)[:-1].tolist()
    for l in range(DEPTH):
        h = rmsnorm(x, norm1_g[l])
        (hq, hf, hi, hgate, nsa_q, kcm, vcm, ksl, vsl, kwn, vwn, nsa_gate) = jnp.split(
            h @ w_in[l], split_at, axis=-1)
        y_hg = hgrn2_mixer(hq, hf, hi, hgate, lower_bounds[l], hg_norm_g[l])
        y_nsa = nsa_mixer(nsa_q, kcm, vcm, ksl, vsl, kwn, vwn, nsa_gate,
                          cmp_pe_k[l], cmp_w1_k[l], cmp_w2_k[l],
                          cmp_pe_v[l], cmp_w1_v[l], cmp_w2_v[l], rel_bias)
        mix = jnp.concatenate([y_hg, y_nsa], axis=-1).astype(x.dtype)
        x = x + mix @ w_out[l]
        x = x + hier_moe(rmsnorm(x, norm2_g[l]), w_router_group[l], b_router_group[l],
                         w_router_expert[l], b_router_expert[l],
                         w_expert_gate[l], w_expert_up[l], w_expert_down[l])
    return rmsnorm(x, final_norm_g)
```

```python
import functools
import math

import jax
import jax.numpy as jnp
import numpy as np
from jax import lax
from jax.experimental import pallas as pl
from jax.experimental.pallas import tpu as pltpu

F32 = jnp.float32
BF16 = jnp.bfloat16

D_MODEL = 2048
HG_HEADS = 8
HG_DIM = 128
HG_WIDTH = HG_HEADS * HG_DIM
HG_CHUNK = 64
NSA_HEADS = 16
NSA_GROUPS = 4
NSA_HPG = NSA_HEADS // NSA_GROUPS
NSA_DH = 64
NSA_WIDTH = NSA_HEADS * NSA_DH
NSA_KV = NSA_GROUPS * NSA_DH
CMP_LEN = 32
CMP_STRIDE = 16
CMP_HIDDEN = 256
SLC_LEN = 64
SLC_TOPK = 16
WIN = 512
REL_BUCKETS = 32
REL_MAX_DIST = 128
N_EXPERT_GROUPS = 8
EXPERTS_PER_GROUP = 8
N_EXPERTS = N_EXPERT_GROUPS * EXPERTS_PER_GROUP
EXPERT_DFF = 1024
RMS_EPS = 1e-6
NEG_INF = -1e30
BIG = 1e9

MAIN_COLS = 4 * HG_WIDTH + NSA_WIDTH + 6 * NSA_KV
GATE_COLS = 3 * NSA_HEADS
LANES = 128
VMEM_LIMIT = 56 * 1024 * 1024


def _cparams(*sem):
    return pltpu.CompilerParams(dimension_semantics=sem, vmem_limit_bytes=VMEM_LIMIT)


def _nt(a, b):
    return lax.dot_general(a, b, (((1,), (1,)), ((), ())), preferred_element_type=F32)


def _tn(a, b):
    return lax.dot_general(a, b, (((0,), (0,)), ((), ())), preferred_element_type=F32)


def _dot(a, b):
    return jnp.dot(a, b, preferred_element_type=F32)


def _inproj_body(x_ref, g_ref, w_ref, wg_ref, o_ref, og_ref, h_scr):
    @pl.when(pl.program_id(1) == 0)
    def _():
        x = x_ref[...]
        y = x * lax.rsqrt(jnp.mean(x * x, axis=-1, keepdims=True) + RMS_EPS) * g_ref[...]
        h_scr[...] = y.astype(BF16)
        og_ref[...] = _dot(h_scr[...], wg_ref[...])

    o_ref[...] = _dot(h_scr[...], w_ref[...])


def _inproj(x2d, g, w_main, w_gate, tm, tn):
    T, D = x2d.shape
    N = w_main.shape[1]
    return pl.pallas_call(
        _inproj_body,
        grid=(T // tm, N // tn),
        in_specs=[
            pl.BlockSpec((tm, D), lambda i, j: (i, 0)),
            pl.BlockSpec((1, D), lambda i, j: (0, 0)),
            pl.BlockSpec((D, tn), lambda i, j: (0, j)),
            pl.BlockSpec((D, LANES), lambda i, j: (0, 0)),
        ],
        out_specs=[
            pl.BlockSpec((tm, tn), lambda i, j: (i, j)),
            pl.BlockSpec((tm, LANES), lambda i, j: (i, 0)),
        ],
        out_shape=[jax.ShapeDtypeStruct((T, N), F32),
                   jax.ShapeDtypeStruct((T, LANES), F32)],
        scratch_shapes=[pltpu.VMEM((tm, D), BF16)],
        compiler_params=_cparams("arbitrary", "arbitrary"),
        name="inproj",
    )(x2d, g, w_main, w_gate)


def _hgrn_tables(C):
    t = np.arange(C)
    levels = []
    m = C // 2
    while m >= 1:
        levels.append(m)
        m //= 2
    rows = [np.tril(np.ones((C, C), np.float32)),
            np.triu(np.ones((C, C), np.float32), 1)]
    lev = np.full((C, C), -1, np.int32)
    for li, m in enumerate(levels):
        r = t % (2 * m)
        mid = t - r + m - 1
        M = np.zeros((C, C), np.float32)
        for i in range(C):
            if r[i] >= m:
                M[i, mid[i] + 1:i + 1] = 1.0
            else:
                M[i, i + 1:mid[i] + 1] = 1.0
        rows.append(M)
        same = (t[:, None] // (2 * m)) == (t[None, :] // (2 * m))
        lev[same & (r[:, None] >= m) & (r[None, :] < m)] = li
    lev[t[:, None] == t[None, :]] = len(levels)
    return np.concatenate(rows, 0), lev, len(levels)


def _hgrn_body(q_ref, f_ref, i_ref, g_ref, lb_ref, ng_ref, mst_ref, lev_ref, o_ref, st_ref,
               *, C, n_lev):
    @pl.when(pl.program_id(2) == 0)
    def _():
        st_ref[...] = jnp.zeros_like(st_ref)

    lb = lb_ref[...]
    ng = ng_ref[...]
    mst = mst_ref[...]
    lev = lev_ref[...]
    n_chunks = q_ref.shape[0] // C

    def chunk(c, carry):
        rows = pl.ds(pl.multiple_of(c * C, C), C)
        q = jax.nn.silu(q_ref[rows, :])
        f = lb + (1.0 - lb) * jax.nn.sigmoid(f_ref[rows, :])
        lf = jnp.log(f)
        k = 1.0 - f
        v = i_ref[rows, :].astype(BF16)
        lf0 = lf.astype(BF16)
        r1 = lf - lf0.astype(F32)
        lf1 = r1.astype(BF16)
        lf2 = (r1 - lf1.astype(F32)).astype(BF16)
        e = _dot(mst, lf0) + _dot(mst, lf1) + _dot(mst, lf2)
        gdec = jnp.exp(e)
        g_b = gdec[0:C]
        g_r = gdec[C:2 * C]
        st = st_ref[...]
        o = _nt((q * g_b).astype(BF16), st.astype(BF16))
        a = jnp.where(lev == n_lev, _nt(q.astype(BF16), k.astype(BF16)), 0.0)
        for li in range(n_lev):
            g_m = gdec[(2 + li) * C:(3 + li) * C]
            a_m = _nt((q * g_m).astype(BF16), (k * g_m).astype(BF16))
            a = jnp.where(lev == li, a_m, a)
        o = o + _dot(a.astype(BF16), v)
        g_last = g_b[C - 1:C, :]
        st_ref[...] = st * g_last + _tn(v, (k * g_r).astype(BF16))
        o = o * lax.rsqrt(jnp.mean(o * o, axis=-1, keepdims=True) + RMS_EPS) * ng
        o_ref[rows, :] = o * jax.nn.silu(g_ref[rows, :])
        return carry

    lax.fori_loop(0, n_chunks, chunk, 0)


def _hgrn(proj, lb, ng, B, S, tb):
    C = HG_CHUNK
    mst, lev, n_lev = _hgrn_tables(C)
    nb = S // tb
    H = HG_HEADS

    def col(group):
        return pl.BlockSpec((tb, HG_DIM), lambda b, h, t: (b * nb + t, group * H + h))

    return pl.pallas_call(
        functools.partial(_hgrn_body, C=C, n_lev=n_lev),
        grid=(B, H, nb),
        in_specs=[
            col(0), col(1), col(2), col(3),
            pl.BlockSpec((None, 1, HG_DIM), lambda b, h, t: (h, 0, 0)),
            pl.BlockSpec((1, HG_DIM), lambda b, h, t: (0, 0)),
            pl.BlockSpec(mst.shape, lambda b, h, t: (0, 0)),
            pl.BlockSpec(lev.shape, lambda b, h, t: (0, 0)),
        ],
        out_specs=pl.BlockSpec((tb, HG_DIM), lambda b, h, t: (b * nb + t, h)),
        out_shape=jax.ShapeDtypeStruct((B * S, HG_WIDTH), F32),
        scratch_shapes=[pltpu.VMEM((HG_DIM, HG_DIM), F32)],
        compiler_params=_cparams("arbitrary", "arbitrary", "arbitrary"),
        name="hgrn2",
    )(proj, proj, proj, proj, lb, ng, jnp.asarray(mst, BF16), jnp.asarray(lev))


def _compress_body(h_ref, pe_ref, w1_ref, w2_ref, o_ref):
    hv = h_ref[...]
    half = hv.shape[1]
    nh = hv.shape[0]
    u = _dot((hv + pe_ref[0:1, :]).astype(BF16), w1_ref[0:half, :].astype(BF16))
    v = _dot((hv + pe_ref[1:2, :]).astype(BF16), w1_ref[half:2 * half, :].astype(BF16))
    pre = u + pltpu.roll(v, nh - 1, axis=0)
    o_ref[...] = _dot(jax.nn.silu(pre).astype(BF16), w2_ref[...].astype(BF16))


def _compress(halves, pe, w1, w2):
    _, B, G, NH, HW = halves.shape
    return pl.pallas_call(
        _compress_body,
        grid=(2, B, G),
        in_specs=[
            pl.BlockSpec((None, None, None, NH, HW), lambda s, b, g: (s, b, g, 0, 0)),
            pl.BlockSpec((None, 2, HW), lambda s, b, g: (s, 0, 0)),
            pl.BlockSpec((None, 2 * HW, CMP_HIDDEN), lambda s, b, g: (s, 0, 0)),
            pl.BlockSpec((None, CMP_HIDDEN, NSA_DH), lambda s, b, g: (s, 0, 0)),
        ],
        out_specs=pl.BlockSpec((None, None, None, NH, NSA_DH), lambda s, b, g: (s, b, g, 0, 0)),
        out_shape=jax.ShapeDtypeStruct((2, B, G, NH, NSA_DH), F32),
        compiler_params=_cparams("arbitrary", "arbitrary", "arbitrary"),
        name="nsa_compress",
    )(halves, pe, w1, w2)


def _nsa_compress_all(proj, pe_k, w1_k, w2_k, pe_v, w1_v, w2_v, B, S):
    NH = S // CMP_STRIDE
    c0 = 4 * HG_WIDTH + NSA_WIDTH

    def halves(idx):
        t = proj[:, c0 + idx * NSA_KV:c0 + (idx + 1) * NSA_KV]
        t = t.reshape(B, NH, CMP_STRIDE, NSA_GROUPS, NSA_DH).transpose(0, 3, 1, 2, 4)
        return t.reshape(B, NSA_GROUPS, NH, CMP_STRIDE * NSA_DH)

    hw = CMP_STRIDE * NSA_DH
    return _compress(jnp.stack([halves(0), halves(1)]),
                     jnp.stack([pe_k.reshape(2, hw), pe_v.reshape(2, hw)]),
                     jnp.stack([w1_k, w1_v]), jnp.stack([w2_k, w2_v]))


NSA_TQ = 128


def _softmax_parts(s):
    m = jnp.max(s, axis=-1, keepdims=True)
    p = jnp.exp(s - m)
    return m, p, jnp.sum(p, axis=-1, keepdims=True)


def _nsa_body(q_ref, gl_ref, kct_ref, vc_ref, kst_ref, vs_ref, kwt_ref, vw_ref,
              bc_ref, bw_ref, bn_ref, ov_ref, ex_ref, o_ref, *, top_k):
    TQ = NSA_TQ
    R = NSA_HPG * TQ
    i = pl.program_id(2)
    qb = q_ref[...] * (NSA_DH ** -0.5)
    q4 = jnp.concatenate([qb[:, h * NSA_DH:(h + 1) * NSA_DH] for h in range(NSA_HPG)],
                         axis=0).astype(BF16)

    def tile4(a):
        return jnp.concatenate([a] * NSA_HPG, axis=0)

    nh = kct_ref.shape[1]
    s = _dot(q4, kct_ref[...]) + bc_ref[...].reshape(R, nh)
    m, p, l = _softmax_parts(s)
    p = jnp.where(s > 0.5 * NEG_INF, p, 0.0)
    l = jnp.sum(p, axis=-1, keepdims=True)
    pc = p * (1.0 / jnp.maximum(l, 1e-30))
    o_cmp = _dot(pc.astype(BF16), vc_ref[...])

    ps = pc[0:TQ] + pc[TQ:2 * TQ] + pc[2 * TQ:3 * TQ] + pc[3 * TQ:4 * TQ]
    ps_hi = ps.astype(BF16)
    ps_lo = (ps - ps_hi.astype(F32)).astype(BF16)
    ov = ov_ref[...]
    imp = _dot(ps_hi, ov) + _dot(ps_lo, ov)
    n_slc = imp.shape[1]
    jidx = lax.broadcasted_iota(jnp.int32, (TQ, n_slc), 1)
    ridx = lax.broadcasted_iota(jnp.int32, (TQ, n_slc), 0)
    qblk = (i * TQ + ridx) // SLC_LEN
    forced = (jidx == 0) | (jidx == qblk) | (jidx == qblk - 1)
    imp = jnp.where(forced, BIG, jnp.where(jidx > qblk, -BIG, imp))
    cnt = jnp.zeros((TQ, n_slc), F32)
    for jp in range(n_slc):
        col = imp[:, jp:jp + 1]
        beats = (col > imp) | ((col == imp) & (jidx > jp))
        cnt = cnt + jnp.where(beats, 1.0, 0.0)
    selb = jnp.where(cnt < top_k, 1.0, 0.0).astype(BF16)

    def sel_bias(lanes):
        km = _dot(selb, ex_ref[:, lanes])
        return tile4(jnp.where(km > 0.5, 0.0, NEG_INF))

    st0 = pl.multiple_of(i * TQ, TQ)
    near = pl.ds(st0, 2 * TQ)
    s = _dot(q4, kst_ref[:, near]) + bn_ref[...].reshape(R, 2 * TQ) + sel_bias(near)
    m, p, l = _softmax_parts(s)
    acc = _dot(p.astype(BF16), vs_ref[near, :])

    def far_step(kc, carry):
        m, l, acc = carry
        lanes = pl.ds(pl.multiple_of((kc + 1) * TQ, TQ), TQ)
        s = _dot(q4, kst_ref[:, lanes]) + sel_bias(lanes)
        m_new = jnp.maximum(m, jnp.max(s, axis=-1, keepdims=True))
        alpha = jnp.exp(m - m_new)
        p = jnp.exp(s - m_new)
        l = alpha * l + jnp.sum(p, axis=-1, keepdims=True)
        acc = alpha * acc + _dot(p.astype(BF16), vs_ref[lanes, :])
        return m_new, l, acc

    m, l, acc = lax.fori_loop(0, jnp.maximum(i - 1, 0), far_step, (m, l, acc))
    o_slc = acc * (1.0 / l)

    wlen = WIN + TQ
    win = pl.ds(st0, wlen)
    s = _dot(q4, kwt_ref[:, win]) + bw_ref[...].reshape(R, wlen)
    jj = lax.broadcasted_iota(jnp.int32, (R, wlen), 1)
    s = jnp.where(jj >= WIN - i * TQ, s, NEG_INF)
    m, p, l = _softmax_parts(s)
    o_win = _dot(p.astype(BF16), vw_ref[win, :]) * (1.0 / l)

    sg = jax.nn.sigmoid(gl_ref[...])
    outs = []
    for h in range(NSA_HPG):
        rows = slice(h * TQ, (h + 1) * TQ)
        outs.append(sg[:, h:h + 1] * o_cmp[rows]
                    + sg[:, NSA_HPG + h:NSA_HPG + h + 1] * o_slc[rows]
                    + sg[:, 2 * NSA_HPG + h:2 * NSA_HPG + h + 1] * o_win[rows])
    o_ref[...] = jnp.concatenate(outs, axis=1)


def _rel_bucket(dist):
    n = jnp.maximum(dist, 0)
    max_exact = REL_BUCKETS // 2
    nf = jnp.maximum(n, 1).astype(F32)
    large = max_exact + (jnp.log(nf / max_exact) / math.log(REL_MAX_DIST / max_exact)
                         * (REL_BUCKETS - max_exact)).astype(jnp.int32)
    large = jnp.minimum(large, REL_BUCKETS - 1)
    return jnp.where(n < max_exact, n, large)


def _nsa(proj, glog, kc, vc, rel_bias, B, S):
    TQ = NSA_TQ
    G, HPG, DH = NSA_GROUPS, NSA_HPG, NSA_DH
    NH = S // CMP_STRIDE
    n_cmp = (S - CMP_LEN) // CMP_STRIDE + 1
    n_slc = S // SLC_LEN
    top_k = min(SLC_TOPK, n_slc)
    nt = S // TQ
    c0 = 4 * HG_WIDTH + NSA_WIDTH

    def kv_t(idx, pad):
        t = proj[:, c0 + idx * NSA_KV:c0 + (idx + 1) * NSA_KV].reshape(B, S, G, DH)
        t = jnp.pad(t.astype(BF16), ((0, 0), (pad, 0), (0, 0), (0, 0)))
        return t.transpose(0, 2, 3, 1)

    def kv_n(idx, pad):
        t = proj[:, c0 + idx * NSA_KV:c0 + (idx + 1) * NSA_KV].reshape(B, S, G, DH)
        t = jnp.pad(t.astype(BF16), ((0, 0), (pad, 0), (0, 0), (0, 0)))
        return t.transpose(0, 2, 1, 3)

    kst, vs = kv_t(2, TQ), kv_n(3, TQ)
    kwt, vw = kv_t(4, WIN), kv_n(5, WIN)
    kct = kc.astype(BF16).transpose(0, 1, 3, 2)
    vcb = vc.astype(BF16)

    table = rel_bias.T.reshape(G, HPG, REL_BUCKETS)
    far = table[:, :, REL_BUCKETS - 1]
    t_pos = jnp.arange(S, dtype=jnp.int32)[:, None]
    cmp_end = jnp.arange(NH, dtype=jnp.int32)[None, :] * CMP_STRIDE + (CMP_LEN - 1)
    ok = (cmp_end <= t_pos) & (jnp.arange(NH)[None, :] < n_cmp)
    bias_cmp = jnp.where(ok, table[:, :, _rel_bucket(t_pos - cmp_end)], NEG_INF)
    r = jnp.arange(TQ, dtype=jnp.int32)[:, None]
    dist_w = r + WIN - jnp.arange(WIN + TQ, dtype=jnp.int32)[None, :]
    bias_win = jnp.where((dist_w >= 0) & (dist_w < WIN), table[:, :, _rel_bucket(dist_w)], NEG_INF)
    dist_n = r + TQ - jnp.arange(2 * TQ, dtype=jnp.int32)[None, :]
    bias_near = jnp.where(dist_n >= 0,
                          table[:, :, _rel_bucket(dist_n)] - far[:, :, None, None], NEG_INF)

    ci = np.arange(NH)[:, None] * CMP_STRIDE
    sj = np.arange(n_slc)[None, :] * SLC_LEN
    overlap = ((ci <= sj + SLC_LEN - 1) & (ci + CMP_LEN - 1 >= sj) & (np.arange(NH)[:, None] < n_cmp))
    key_blk = (np.arange(S + TQ) - TQ) // SLC_LEN
    expand = (np.arange(n_slc)[:, None] == key_blk[None, :])

    glog_g = glog[:, :GATE_COLS].reshape(B * S, 3, G, HPG).transpose(2, 0, 1, 3).reshape(G, B * S, 3 * HPG)
    glog_g = jnp.pad(glog_g, ((0, 0), (0, 0), (0, LANES - 3 * HPG)))

    qcol = (4 * HG_WIDTH) // (HPG * DH)
    whole = lambda shape: pl.BlockSpec((None, None) + shape, lambda b, g, i: (b, g, 0, 0))
    return pl.pallas_call(
        functools.partial(_nsa_body, top_k=top_k),
        grid=(B, G, nt),
        in_specs=[
            pl.BlockSpec((TQ, HPG * DH), lambda b, g, i: (b * nt + i, qcol + g)),
            pl.BlockSpec((None, TQ, LANES), lambda b, g, i: (g, b * nt + i, 0)),
            whole((DH, NH)), whole((NH, DH)),
            whole((DH, S + TQ)), whole((S + TQ, DH)),
            whole((DH, S + WIN)), whole((S + WIN, DH)),
            pl.BlockSpec((None, HPG, TQ, NH), lambda b, g, i: (g, 0, i, 0)),
            pl.BlockSpec((None, HPG, TQ, WIN + TQ), lambda b, g, i: (g, 0, 0, 0)),
            pl.BlockSpec((None, HPG, TQ, 2 * TQ), lambda b, g, i: (g, 0, 0, 0)),
            pl.BlockSpec((NH, n_slc), lambda b, g, i: (0, 0)),
            pl.BlockSpec((n_slc, S + TQ), lambda b, g, i: (0, 0)),
        ],
        out_specs=pl.BlockSpec((TQ, HPG * DH), lambda b, g, i: (b * nt + i, g)),
        out_shape=jax.ShapeDtypeStruct((B * S, NSA_WIDTH), F32),
        compiler_params=_cparams("arbitrary", "arbitrary", "arbitrary"),
        name="nsa_attention",
    )(proj, glog_g, kct, vcb, kst, vs, kwt, vw, bias_cmp, bias_win, bias_near,
      jnp.asarray(overlap, BF16), jnp.asarray(expand, BF16))


ROUTE_E1, ROUTE_E2, ROUTE_W1, ROUTE_W2 = 0, 1, 2, 3


def _outproj_body(x_ref, yh_ref, yn_ref, wo_ref, g2_ref, wrh_ref, wrl_ref, br_ref,
                  x2_ref, h2_ref, rt_ref):
    x2 = (x_ref[...] + _dot(yh_ref[...].astype(BF16), wo_ref[0:HG_WIDTH, :])
          + _dot(yn_ref[...].astype(BF16), wo_ref[HG_WIDTH:HG_WIDTH + NSA_WIDTH, :]))
    x2_ref[...] = x2
    h = x2 * lax.rsqrt(jnp.mean(x2 * x2, axis=-1, keepdims=True) + RMS_EPS) * g2_ref[...]
    hb = h.astype(BF16)
    h2_ref[...] = hb
    hl = (h - hb.astype(F32)).astype(BF16)
    wrh = wrh_ref[...]
    lg = _dot(hb, wrh) + _dot(hl, wrh) + _dot(hb, wrl_ref[...]) + br_ref[...]
    lane = lax.broadcasted_iota(jnp.int32, lg.shape, 1)
    first = lambda hit: jnp.min(jnp.where(hit, lane, LANES), axis=-1, keepdims=True)
    lgg = jnp.where(lane < N_EXPERT_GROUPS, lg, NEG_INF)
    mg = jnp.max(lgg, axis=-1, keepdims=True)
    p_top = 1.0 / jnp.sum(jnp.exp(lgg - mg), axis=-1, keepdims=True)
    lo = N_EXPERT_GROUPS + EXPERTS_PER_GROUP * first(lgg == mg)
    le = jnp.where((lane >= lo) & (lane < lo + EXPERTS_PER_GROUP), lg, NEG_INF)
    v1 = jnp.max(le, axis=-1, keepdims=True)
    i1 = first(le == v1)
    le = jnp.where(lane == i1, NEG_INF, le)
    v2 = jnp.max(le, axis=-1, keepdims=True)
    i2 = first(le == v2)
    e21 = jnp.exp(v2 - v1)
    w1 = p_top / (1.0 + e21)
    rt = jnp.where(lane == ROUTE_E1, (i1 - N_EXPERT_GROUPS).astype(F32), 0.0)
    rt = jnp.where(lane == ROUTE_E2, (i2 - N_EXPERT_GROUPS).astype(F32), rt)
    rt = jnp.where(lane == ROUTE_W1, w1, rt)
    rt_ref[...] = jnp.where(lane == ROUTE_W2, w1 * e21, rt)


def _outproj(x2d, y_hg, y_nsa, w_out, g2, wr_hi, wr_lo, b_r, tm):
    T, D = x2d.shape
    row = lambda n: pl.BlockSpec((tm, n), lambda i: (i, 0))
    full = lambda a: pl.BlockSpec(a.shape, lambda i: (0, 0))
    return pl.pallas_call(
        _outproj_body,
        grid=(T // tm,),
        in_specs=[row(D), row(HG_WIDTH), row(NSA_WIDTH), full(w_out), full(g2),
                  full(wr_hi), full(wr_lo), full(b_r)],
        out_specs=[row(D), row(D), row(LANES)],
        out_shape=[jax.ShapeDtypeStruct((T, D), F32), jax.ShapeDtypeStruct((T, D), BF16),
                   jax.ShapeDtypeStruct((T, LANES), F32)],
        compiler_params=_cparams("arbitrary"),
        name="outproj_router",
    )(x2d, y_hg, y_nsa, w_out, g2, wr_hi, wr_lo, b_r)


MOE_ALIGN = 128
MOE_TM = 1024
MOE_SUB = 256
MOE_TF = 256


def _moe_body(ie_ref, ir_ref, in_ref, xs_hbm, wg_ref, wu_ref, wd_ref, y0_hbm, y_hbm,
              xbuf, acc, sem_in, sem_out):
    del ie_ref, y0_hbm
    w = pl.program_id(0)
    f = pl.program_id(1)
    nsub = in_ref[w]
    row0 = pl.multiple_of(ir_ref[w] * MOE_ALIGN, MOE_ALIGN)

    @pl.when((f == 0) & (nsub > 0))
    def _():
        cp = pltpu.make_async_copy(xs_hbm.at[pl.ds(row0, MOE_TM), :], xbuf, sem_in)
        cp.start()
        acc[...] = jnp.zeros_like(acc)
        cp.wait()

    @pl.when(nsub > 0)
    def _():
        wg = wg_ref[...].astype(BF16)
        wu = wu_ref[...].astype(BF16)
        wd = wd_ref[...].astype(BF16)

        def sub(j, carry):
            rows = pl.ds(pl.multiple_of(j * MOE_SUB, MOE_SUB), MOE_SUB)
            xj = xbuf[rows, :]
            hid = jax.nn.silu(_dot(xj, wg)) * _dot(xj, wu)
            acc[rows, :] += _dot(hid.astype(BF16), wd)
            return carry

        lax.fori_loop(0, (nsub * MOE_ALIGN + MOE_SUB - 1) // MOE_SUB, sub, 0)

    @pl.when((f == pl.num_programs(1) - 1) & (nsub > 0))
    def _():
        def out_copy(j):
            r = pl.multiple_of(j * MOE_ALIGN, MOE_ALIGN)
            return pltpu.make_async_copy(acc.at[pl.ds(r, MOE_ALIGN), :],
                                         y_hbm.at[pl.ds(row0 + r, MOE_ALIGN), :], sem_out)

        def start(j, carry):
            out_copy(j).start()
            return carry

        def wait(j, carry):
            out_copy(j).wait()
            return carry

        lax.fori_loop(0, nsub, start, 0)
        lax.fori_loop(0, nsub, wait, 0)


def _moe(xs, item_e, item_r, item_n, w_gate, w_up, w_down):
    rows, D = xs.shape
    n_items = item_e.shape[0]
    nf = EXPERT_DFF // MOE_TF

    def fcol(f, nn, w):
        return jnp.where(nn[w] > 0, f, nf - 1)

    return pl.pallas_call(
        _moe_body,
        grid_spec=pltpu.PrefetchScalarGridSpec(
            num_scalar_prefetch=3,
            grid=(n_items, nf),
            in_specs=[
                pl.BlockSpec(memory_space=pl.ANY),
                pl.BlockSpec((None, D, MOE_TF), lambda w, f, ie, ir, nn: (ie[w], 0, fcol(f, nn, w))),
                pl.BlockSpec((None, D, MOE_TF), lambda w, f, ie, ir, nn: (ie[w], 0, fcol(f, nn, w))),
                pl.BlockSpec((None, MOE_TF, D), lambda w, f, ie, ir, nn: (ie[w], fcol(f, nn, w), 0)),
                pl.BlockSpec(memory_space=pl.ANY),
            ],
            out_specs=pl.BlockSpec(memory_space=pl.ANY),
            scratch_shapes=[pltpu.VMEM((MOE_TM, D), BF16), pltpu.VMEM((MOE_TM, D), F32),
                            pltpu.SemaphoreType.DMA(()), pltpu.SemaphoreType.DMA(())],
        ),
        out_shape=jax.ShapeDtypeStruct((rows, D), F32),
        input_output_aliases={7: 0},
        compiler_params=_cparams("arbitrary", "arbitrary"),
        name="moe_experts",
    )(item_e, item_r, item_n, xs, w_gate, w_up, w_down, jnp.zeros((rows, D), F32))


def _final_body(x2_ref, y_ref, rt_ref, g_ref, o_ref):
    D = x2_ref.shape[1]
    rt = rt_ref[...]
    x = (x2_ref[...] + rt[:, ROUTE_W1:ROUTE_W1 + 1] * y_ref[:, 0:D]
         + rt[:, ROUTE_W2:ROUTE_W2 + 1] * y_ref[:, D:2 * D])
    o_ref[...] = x * lax.rsqrt(jnp.mean(x * x, axis=-1, keepdims=True) + RMS_EPS) * g_ref[...]


def _final(x2, y_pair, rt, g, tm):
    T, D = x2.shape
    return pl.pallas_call(
        _final_body,
        grid=(T // tm,),
        in_specs=[pl.BlockSpec((tm, D), lambda i: (i, 0)),
                  pl.BlockSpec((tm, 2 * D), lambda i: (i, 0)),
                  pl.BlockSpec((tm, LANES), lambda i: (i, 0)),
                  pl.BlockSpec((1, D), lambda i: (0, 0))],
        out_specs=pl.BlockSpec((tm, D), lambda i: (i, 0)),
        out_shape=jax.ShapeDtypeStruct((T, D), F32),
        compiler_params=_cparams("arbitrary"),
        name="combine_final_norm",
    )(x2, y_pair, rt, g)


def _moe_layout(e_flat):
    A = e_flat.shape[0]
    onehot = (e_flat[:, None] == jnp.arange(N_EXPERTS, dtype=jnp.int32)[None, :]).astype(jnp.int32)
    csum = jnp.cumsum(onehot, axis=0)
    counts = csum[-1]
    rank = jnp.take_along_axis(csum, e_flat[:, None], axis=1)[:, 0] - 1
    nblk = (counts + MOE_ALIGN - 1) // MOE_ALIGN
    blk0 = jnp.cumsum(nblk) - nblk
    dest = blk0[e_flat] * MOE_ALIGN + rank
    per_item = MOE_TM // MOE_ALIGN
    nitem = (nblk + per_item - 1) // per_item
    iend = jnp.cumsum(nitem)
    n_items = (A // MOE_ALIGN + N_EXPERTS + N_EXPERTS * (per_item - 1)) // per_item
    w = jnp.arange(n_items, dtype=jnp.int32)
    total = iend[-1]
    wv = jnp.minimum(w, total - 1)
    item_e = jnp.minimum(jnp.searchsorted(iend, wv, side='right'), N_EXPERTS - 1).astype(jnp.int32)
    sb = wv - (iend - nitem)[item_e]
    item_r = (blk0[item_e] + sb * per_item).astype(jnp.int32)
    item_n = jnp.where(w < total, jnp.minimum(per_item, nblk[item_e] - sb * per_item), 0).astype(jnp.int32)
    return dest, item_e, item_r, item_n


def kernel(x, norm1_g, w_in, hg_lb_logits, hg_norm_g, cmp_pe_k, cmp_w1_k, cmp_w2_k, cmp_pe_v,
           cmp_w1_v, cmp_w2_v, rel_bias, w_out, norm2_g, w_router_group, b_router_group,
           w_router_expert, b_router_expert, w_expert_gate, w_expert_up, w_expert_down,
           final_norm_g):
    B, S, D = x.shape
    T = B * S
    assert w_in.shape[0] == 1, "single-layer block"
    x2d = x.reshape(T, D)

    lower = jax.nn.softmax(hg_lb_logits.astype(F32), axis=0)[0].reshape(HG_HEADS, 1, HG_DIM)

    w_in_b = w_in[0].astype(BF16)
    w_gate_cols = jnp.pad(w_in_b[:, MAIN_COLS:], ((0, 0), (0, LANES - GATE_COLS)))
    proj, glog = _inproj(x2d, norm1_g[0].reshape(1, D), w_in_b[:, :MAIN_COLS], w_gate_cols,
                         tm=1024, tn=512)

    y_hg = _hgrn(proj, lower, hg_norm_g[0].reshape(1, HG_DIM), B, S, tb=512)
    kcvc = _nsa_compress_all(proj, cmp_pe_k[0], cmp_w1_k[0], cmp_w2_k[0],
                             cmp_pe_v[0], cmp_w1_v[0], cmp_w2_v[0], B, S)
    y_nsa = _nsa(proj, glog, kcvc[0], kcvc[1], rel_bias, B, S)

    w_r = jnp.concatenate([w_router_group[0], w_router_expert[0]], axis=1)
    w_r = jnp.pad(w_r, ((0, 0), (0, LANES - w_r.shape[1])))
    wr_hi = w_r.astype(BF16)
    wr_lo = (w_r - wr_hi.astype(F32)).astype(BF16)
    b_r = jnp.concatenate([b_router_group[0], b_router_expert[0]])
    b_r = jnp.pad(b_r, (0, LANES - b_r.shape[0])).reshape(1, LANES)
    x2, h2, rt = _outproj(x2d, y_hg, y_nsa, w_out[0].astype(BF16), norm2_g[0].reshape(1, D),
                          wr_hi, wr_lo, b_r, tm=256)

    e_flat = rt[:, ROUTE_E1:ROUTE_E2 + 1].astype(jnp.int32).reshape(2 * T)
    dest, item_e, item_r, item_n = _moe_layout(e_flat)
    n_rows = (2 * T // MOE_ALIGN + N_EXPERTS) * MOE_ALIGN + MOE_TM
    tok_of_row = jnp.zeros((n_rows,), jnp.int32).at[dest].set(jnp.arange(2 * T, dtype=jnp.int32) // 2)
    xs = h2[tok_of_row]
    ys = _moe(xs, item_e, item_r, item_n, w_expert_gate[0], w_expert_up[0], w_expert_down[0])
    y_pair = ys[dest].reshape(T, 2 * D)

    out = _final(x2, y_pair, rt, final_norm_g.reshape(1, D), tm=256)
    return out.reshape(B, S, D)
```

```python
import functools
import math

import jax
import jax.numpy as jnp
import numpy as np
from jax import lax
from jax.experimental import pallas as pl
from jax.experimental.pallas import tpu as pltpu

F32 = jnp.float32
BF16 = jnp.bfloat16

D_MODEL = 2048
HG_HEADS = 8
HG_DIM = 128
HG_WIDTH = HG_HEADS * HG_DIM
HG_CHUNK = 64
NSA_HEADS = 16
NSA_GROUPS = 4
NSA_HPG = NSA_HEADS // NSA_GROUPS
NSA_DH = 64
NSA_WIDTH = NSA_HEADS * NSA_DH
NSA_KV = NSA_GROUPS * NSA_DH
CMP_LEN = 32
CMP_STRIDE = 16
CMP_HIDDEN = 256
SLC_LEN = 64
SLC_TOPK = 16
WIN = 512
REL_BUCKETS = 32
REL_MAX_DIST = 128
N_EXPERT_GROUPS = 8
EXPERTS_PER_GROUP = 8
N_EXPERTS = N_EXPERT_GROUPS * EXPERTS_PER_GROUP
EXPERT_DFF = 1024
RMS_EPS = 1e-6
NEG_INF = -1e30
BIG = 1e9

MAIN_COLS = 4 * HG_WIDTH + NSA_WIDTH + 6 * NSA_KV
GATE_COLS = 3 * NSA_HEADS
LANES = 128
VMEM_LIMIT = 56 * 1024 * 1024


def _cparams(*sem):
    return pltpu.CompilerParams(dimension_semantics=sem, vmem_limit_bytes=VMEM_LIMIT)


def _nt(a, b):
    return lax.dot_general(a, b, (((1,), (1,)), ((), ())), preferred_element_type=F32)


def _tn(a, b):
    return lax.dot_general(a, b, (((0,), (0,)), ((), ())), preferred_element_type=F32)


def _dot(a, b):
    return jnp.dot(a, b, preferred_element_type=F32)


def _inproj_body(x_ref, g_ref, w_ref, wg_ref, o_ref, og_ref, h_scr):
    @pl.when(pl.program_id(1) == 0)
    def _():
        x = x_ref[...]
        y = x * lax.rsqrt(jnp.mean(x * x, axis=-1, keepdims=True) + RMS_EPS) * g_ref[...]
        h_scr[...] = y.astype(BF16)
        og_ref[...] = _dot(h_scr[...], wg_ref[...])

    o_ref[...] = _dot(h_scr[...], w_ref[...])


def _inproj(x2d, g, w_main, w_gate, tm, tn):
    T, D = x2d.shape
    N = w_main.shape[1]
    return pl.pallas_call(
        _inproj_body,
        grid=(T // tm, N // tn),
        in_specs=[
            pl.BlockSpec((tm, D), lambda i, j: (i, 0)),
            pl.BlockSpec((1, D), lambda i, j: (0, 0)),
            pl.BlockSpec((D, tn), lambda i, j: (0, j)),
            pl.BlockSpec((D, LANES), lambda i, j: (0, 0)),
        ],
        out_specs=[
            pl.BlockSpec((tm, tn), lambda i, j: (i, j)),
            pl.BlockSpec((tm, LANES), lambda i, j: (i, 0)),
        ],
        out_shape=[jax.ShapeDtypeStruct((T, N), F32),
                   jax.ShapeDtypeStruct((T, LANES), F32)],
        scratch_shapes=[pltpu.VMEM((tm, D), BF16)],
        compiler_params=_cparams("arbitrary", "arbitrary"),
        name="inproj",
    )(x2d, g, w_main, w_gate)


def _hgrn_tables(C):
    t = np.arange(C)
    levels = []
    m = C // 2
    while m >= 1:
        levels.append(m)
        m //= 2
    rows = [np.tril(np.ones((C, C), np.float32)),
            np.triu(np.ones((C, C), np.float32), 1)]
    lev = np.full((C, C), -1, np.int32)
    for li, m in enumerate(levels):
        r = t % (2 * m)
        mid = t - r + m - 1
        M = np.zeros((C, C), np.float32)
        for i in range(C):
            if r[i] >= m:
                M[i, mid[i] + 1:i + 1] = 1.0
            else:
                M[i, i + 1:mid[i] + 1] = 1.0
        rows.append(M)
        same = (t[:, None] // (2 * m)) == (t[None, :] // (2 * m))
        lev[same & (r[:, None] >= m) & (r[None, :] < m)] = li
    lev[t[:, None] == t[None, :]] = len(levels)
    return np.concatenate(rows, 0), lev, len(levels)


def _hgrn_body(q_ref, f_ref, i_ref, g_ref, lb_ref, ng_ref, mst_ref, lev_ref, o_ref, st_ref,
               *, C, n_lev):
    @pl.when(pl.program_id(2) == 0)
    def _():
        st_ref[...] = jnp.zeros_like(st_ref)

    lb = lb_ref[...]
    ng = ng_ref[...]
    mst = mst_ref[...]
    lev = lev_ref[...]
    n_chunks = q_ref.shape[0] // C

    def chunk(c, carry):
        rows = pl.ds(pl.multiple_of(c * C, C), C)
        q = jax.nn.silu(q_ref[rows, :])
        f = lb + (1.0 - lb) * jax.nn.sigmoid(f_ref[rows, :])
        lf = jnp.log(f)
        k = 1.0 - f
        v = i_ref[rows, :].astype(BF16)
        lf0 = lf.astype(BF16)
        r1 = lf - lf0.astype(F32)
        lf1 = r1.astype(BF16)
        lf2 = (r1 - lf1.astype(F32)).astype(BF16)
        e = _dot(mst, lf0) + _dot(mst, lf1) + _dot(mst, lf2)
        gdec = jnp.exp(e)
        g_b = gdec[0:C]
        g_r = gdec[C:2 * C]
        st = st_ref[...]
        o = _nt((q * g_b).astype(BF16), st.astype(BF16))
        a = jnp.where(lev == n_lev, _nt(q.astype(BF16), k.astype(BF16)), 0.0)
        for li in range(n_lev):
            g_m = gdec[(2 + li) * C:(3 + li) * C]
            a_m = _nt((q * g_m).astype(BF16), (k * g_m).astype(BF16))
            a = jnp.where(lev == li, a_m, a)
        o = o + _dot(a.astype(BF16), v)
        g_last = g_b[C - 1:C, :]
        st_ref[...] = st * g_last + _tn(v, (k * g_r).astype(BF16))
        o = o * lax.rsqrt(jnp.mean(o * o, axis=-1, keepdims=True) + RMS_EPS) * ng
        o_ref[rows, :] = o * jax.nn.silu(g_ref[rows, :])
        return carry

    lax.fori_loop(0, n_chunks, chunk, 0)


def _hgrn(proj, lb, ng, B, S, tb):
    C = HG_CHUNK
    mst, lev, n_lev = _hgrn_tables(C)
    nb = S // tb
    H = HG_HEADS

    def col(group):
        return pl.BlockSpec((tb, HG_DIM), lambda b, h, t: (b * nb + t, group * H + h))

    return pl.pallas_call(
        functools.partial(_hgrn_body, C=C, n_lev=n_lev),
        grid=(B, H, nb),
        in_specs=[
            col(0), col(1), col(2), col(3),
            pl.BlockSpec((None, 1, HG_DIM), lambda b, h, t: (h, 0, 0)),
            pl.BlockSpec((1, HG_DIM), lambda b, h, t: (0, 0)),
            pl.BlockSpec(mst.shape, lambda b, h, t: (0, 0)),
            pl.BlockSpec(lev.shape, lambda b, h, t: (0, 0)),
        ],
        out_specs=pl.BlockSpec((tb, HG_DIM), lambda b, h, t: (b * nb + t, h)),
        out_shape=jax.ShapeDtypeStruct((B * S, HG_WIDTH), F32),
        scratch_shapes=[pltpu.VMEM((HG_DIM, HG_DIM), F32)],
        compiler_params=_cparams("arbitrary", "arbitrary", "arbitrary"),
        name="hgrn2",
    )(proj, proj, proj, proj, lb, ng, jnp.asarray(mst, BF16), jnp.asarray(lev))


def _compress_body(h_ref, pe_ref, w1_ref, w2_ref, o_ref):
    hv = h_ref[...]
    half = hv.shape[1]
    nh = hv.shape[0]
    u = _dot((hv + pe_ref[0:1, :]).astype(BF16), w1_ref[0:half, :].astype(BF16))
    v = _dot((hv + pe_ref[1:2, :]).astype(BF16), w1_ref[half:2 * half, :].astype(BF16))
    pre = u + pltpu.roll(v, nh - 1, axis=0)
    o_ref[...] = _dot(jax.nn.silu(pre).astype(BF16), w2_ref[...].astype(BF16))


def _compress(halves, pe, w1, w2):
    _, B, G, NH, HW = halves.shape
    return pl.pallas_call(
        _compress_body,
        grid=(2, B, G),
        in_specs=[
            pl.BlockSpec((None, None, None, NH, HW), lambda s, b, g: (s, b, g, 0, 0)),
            pl.BlockSpec((None, 2, HW), lambda s, b, g: (s, 0, 0)),
            pl.BlockSpec((None, 2 * HW, CMP_HIDDEN), lambda s, b, g: (s, 0, 0)),
            pl.BlockSpec((None, CMP_HIDDEN, NSA_DH), lambda s, b, g: (s, 0, 0)),
        ],
        out_specs=pl.BlockSpec((None, None, None, NH, NSA_DH), lambda s, b, g: (s, b, g, 0, 0)),
        out_shape=jax.ShapeDtypeStruct((2, B, G, NH, NSA_DH), F32),
        compiler_params=_cparams("arbitrary", "arbitrary", "arbitrary"),
        name="nsa_compress",
    )(halves, pe, w1, w2)


def _nsa_compress_all(proj, pe_k, w1_k, w2_k, pe_v, w1_v, w2_v, B, S):
    NH = S // CMP_STRIDE
    c0 = 4 * HG_WIDTH + NSA_WIDTH

    def halves(idx):
        t = proj[:, c0 + idx * NSA_KV:c0 + (idx + 1) * NSA_KV]
        t = t.reshape(B, NH, CMP_STRIDE, NSA_GROUPS, NSA_DH).transpose(0, 3, 1, 2, 4)
        return t.reshape(B, NSA_GROUPS, NH, CMP_STRIDE * NSA_DH)

    hw = CMP_STRIDE * NSA_DH
    return _compress(jnp.stack([halves(0), halves(1)]),
                     jnp.stack([pe_k.reshape(2, hw), pe_v.reshape(2, hw)]),
                     jnp.stack([w1_k, w1_v]), jnp.stack([w2_k, w2_v]))


NSA_TQ = 128


NSA_MASK_ROWS = 128
NSA_WIN_MASK_ROWS = 16
NSA_FAR = 4
NSA_SLC_PAD = (NSA_FAR - 1) * NSA_TQ


def _softmax_cols(s):
    m = jnp.max(s, axis=0, keepdims=True)
    p = jnp.exp(s - m)
    return m, p, jnp.sum(p, axis=0, keepdims=True)


def _nsa_body(qt_ref, gl_ref, kc_ref, vct_ref, ks_ref, vst_ref, kw_ref, vwt_ref,
              bc_ref, bw_ref, bn_ref, ovt_ref, o_ref, *, top_k, n_slc):
    TQ = NSA_TQ
    i = pl.program_id(2)
    qt = qt_ref[...]
    q4 = jnp.concatenate([qt[h * NSA_DH:(h + 1) * NSA_DH, :] for h in range(NSA_HPG)],
                         axis=1)

    def heads(a):
        return jnp.concatenate([a] * NSA_HPG, axis=1)

    s = _dot(kc_ref[...], q4) + bc_ref[...]
    m = jnp.max(s, axis=0, keepdims=True)
    p = jnp.where(s > 0.5 * NEG_INF, jnp.exp(s - m), 0.0)
    l = jnp.sum(p, axis=0, keepdims=True)
    pc = p * (1.0 / jnp.maximum(l, 1e-30))
    o_cmp = _dot(vct_ref[...], pc.astype(BF16))

    ps = pc[:, 0:TQ] + pc[:, TQ:2 * TQ] + pc[:, 2 * TQ:3 * TQ] + pc[:, 3 * TQ:4 * TQ]
    ps_hi = ps.astype(BF16)
    ps_lo = (ps - ps_hi.astype(F32)).astype(BF16)
    ovt = ovt_ref[...]
    imp = _dot(ovt, ps_hi) + _dot(ovt, ps_lo)
    jb = lax.broadcasted_iota(jnp.int32, (n_slc, TQ), 0)
    qblk = (i * TQ + lax.broadcasted_iota(jnp.int32, (n_slc, TQ), 1)) // SLC_LEN
    forced = (jb == 0) | (jb == qblk) | (jb == qblk - 1)
    imp = jnp.where(forced, BIG, jnp.where(jb > qblk, -BIG, imp))
    slabs = []
    for v in range(n_slc // 8):
        slab = imp[8 * v:8 * v + 8, :]
        jbs = jb[8 * v:8 * v + 8, :]
        cnt = jnp.zeros((8, TQ), F32)
        for jp in range(n_slc):
            row = imp[jp:jp + 1, :]
            if jp < 8 * v:
                beats = row >= slab
            elif jp >= 8 * v + 8:
                beats = row > slab
            else:
                beats = (row > slab) | ((row == slab) & (jbs > jp))
            cnt = cnt + jnp.where(beats, 1.0, 0.0)
        slabs.append(jnp.where(cnt < top_k, 0.0, 1.0))
    unsel = jnp.concatenate(slabs + [jnp.ones((NSA_MASK_ROWS - n_slc, TQ), F32)], axis=0)
    qa = jnp.concatenate([heads(unsel.astype(BF16)), q4], axis=0)

    near = pl.ds(pl.multiple_of(NSA_SLC_PAD + (i - 1) * TQ, TQ), 2 * TQ)
    s = _dot(ks_ref[near, :], qa) + bn_ref[...]
    m, p, l = _softmax_cols(s)
    acc = _dot(vst_ref[:, near], p.astype(BF16))

    n_far = i - 1
    n_chunks = jnp.maximum((n_far + NSA_FAR - 1) // NSA_FAR, 0)
    far0 = NSA_SLC_PAD + TQ * (n_far - NSA_FAR * n_chunks)

    def far_step(c, carry):
        m, l, acc = carry
        keys = pl.ds(pl.multiple_of(far0 + c * NSA_FAR * TQ, TQ), NSA_FAR * TQ)
        s = _dot(ks_ref[keys, :], qa)
        m_new = jnp.maximum(m, jnp.max(s, axis=0, keepdims=True))
        alpha = jnp.exp(m - m_new)
        p = jnp.exp(s - m_new)
        l = alpha * l + jnp.sum(p, axis=0, keepdims=True)
        acc = alpha * acc + _dot(vst_ref[:, keys], p.astype(BF16))
        return m_new, l, acc

    m, l, acc = lax.fori_loop(0, n_chunks, far_step, (m, l, acc))
    o_slc = acc * (1.0 / l)

    win = pl.ds(pl.multiple_of(i * TQ, TQ), WIN + TQ)
    pad_row = jnp.where(lax.broadcasted_iota(jnp.int32, (NSA_WIN_MASK_ROWS, NSA_HPG * TQ), 0) == 0,
                        1.0, 0.0).astype(BF16)
    s = _dot(kw_ref[win, :], jnp.concatenate([pad_row, q4], axis=0)) + bw_ref[...]
    m, p, l = _softmax_cols(s)
    o_win = _dot(vwt_ref[:, win], p.astype(BF16)) * (1.0 / l)

    sg = jax.nn.sigmoid(gl_ref[...])
    outs = []
    for h in range(NSA_HPG):
        lanes = slice(h * TQ, (h + 1) * TQ)
        outs.append(sg[h:h + 1, :] * o_cmp[:, lanes]
                    + sg[NSA_HPG + h:NSA_HPG + h + 1, :] * o_slc[:, lanes]
                    + sg[2 * NSA_HPG + h:2 * NSA_HPG + h + 1, :] * o_win[:, lanes])
    o_ref[...] = jnp.concatenate(outs, axis=0).T


def _rel_bucket(dist):
    n = jnp.maximum(dist, 0)
    max_exact = REL_BUCKETS // 2
    nf = jnp.maximum(n, 1).astype(F32)
    large = max_exact + (jnp.log(nf / max_exact) / math.log(REL_MAX_DIST / max_exact)
                         * (REL_BUCKETS - max_exact)).astype(jnp.int32)
    large = jnp.minimum(large, REL_BUCKETS - 1)
    return jnp.where(n < max_exact, n, large)


def _bias_table(table, dist, valid):
    onehot = (_rel_bucket(jnp.asarray(dist, jnp.int32))[..., None]
              == jnp.arange(REL_BUCKETS, dtype=jnp.int32)).astype(F32)
    b = jnp.einsum('rmk,ghk->ghrm', onehot, table, precision=lax.Precision.HIGHEST)
    return jnp.where(jnp.asarray(valid), b, NEG_INF)


def _nsa(proj, glog, kc, vc, rel_bias, B, S):
    TQ = NSA_TQ
    G, HPG, DH = NSA_GROUPS, NSA_HPG, NSA_DH
    NH = S // CMP_STRIDE
    n_cmp = (S - CMP_LEN) // CMP_STRIDE + 1
    n_slc = S // SLC_LEN
    top_k = min(SLC_TOPK, n_slc)
    nt = S // TQ
    c0 = 4 * HG_WIDTH + NSA_WIDTH

    MR = NSA_MASK_ROWS
    assert n_slc < MR and n_slc % 8 == 0

    def kv(idx, pad):
        t = proj[:, c0 + idx * NSA_KV:c0 + (idx + 1) * NSA_KV].reshape(B, S, G, DH)
        t = jnp.pad(t.astype(BF16), ((0, 0), (pad, 0), (0, 0), (0, 0)))
        return t.transpose(0, 2, 1, 3)

    def keys_with_masks(idx, pad, with_blocks):
        pos = np.arange(pad + S) - pad
        if with_blocks:
            cols = np.zeros((pad + S, MR), np.float32)
            cols[:, :n_slc] = ((pos // SLC_LEN)[:, None] == np.arange(n_slc)[None, :]) & (pos >= 0)[:, None]
            cols[:, n_slc] = pos < 0
        else:
            cols = np.zeros((pad + S, NSA_WIN_MASK_ROWS), np.float32)
            cols[:, 0] = pos < 0
        m = jnp.broadcast_to(jnp.asarray(cols * NEG_INF, BF16), (B, G) + cols.shape)
        return jnp.concatenate([m, kv(idx, pad)], axis=3)

    SP = NSA_SLC_PAD
    ks, vst = keys_with_masks(2, SP, True), kv(3, SP).transpose(0, 1, 3, 2)
    kw, vwt = keys_with_masks(4, WIN, False), kv(5, WIN).transpose(0, 1, 3, 2)
    kcb = kc.astype(BF16)
    vct = vc.astype(BF16).transpose(0, 1, 3, 2)
    qt = (proj[:, 4 * HG_WIDTH:c0] * (DH ** -0.5)).astype(BF16)
    qt = qt.reshape(B, S, G, HPG * DH).transpose(0, 2, 3, 1)

    table = rel_bias.T.reshape(G, HPG, REL_BUCKETS)
    far = table[:, :, REL_BUCKETS - 1]
    r = np.arange(TQ)[:, None]

    def keys_major(b):
        return b.transpose(0, 3, 1, 2).reshape(G, b.shape[3], HPG * TQ)

    per_tile = TQ // CMP_STRIDE
    d_c = r - (CMP_LEN - 1) - CMP_STRIDE * (np.arange(2 * NH)[None, :] - NH)
    rel_cmp = keys_major(_bias_table(table, d_c, d_c >= 0))
    bias_cmp = jnp.stack([rel_cmp[:, NH - per_tile * i:2 * NH - per_tile * i] for i in range(nt)],
                         axis=1)
    d_w = r + WIN - np.arange(WIN + TQ)[None, :]
    bias_win = keys_major(_bias_table(table, d_w, (d_w >= 0) & (d_w < WIN)))
    d_n = r + TQ - np.arange(2 * TQ)[None, :]
    bias_near = keys_major(_bias_table(table - far[:, :, None], d_n, d_n >= 0))

    ci = np.arange(NH)[None, :] * CMP_STRIDE
    sj = np.arange(n_slc)[:, None] * SLC_LEN
    overlap_t = ((ci <= sj + SLC_LEN - 1) & (ci + CMP_LEN - 1 >= sj) & (np.arange(NH)[None, :] < n_cmp))

    gate_rows = 16
    glog_g = glog[:, :GATE_COLS].reshape(B * S, 3, G, HPG).transpose(2, 1, 3, 0).reshape(G, 3 * HPG, B * S)
    glog_g = jnp.pad(glog_g, ((0, 0), (0, gate_rows - 3 * HPG), (0, 0)))

    whole = lambda shape: pl.BlockSpec((None, None) + shape, lambda b, g, i: (b, g, 0, 0))
    tile = lambda k: pl.BlockSpec((None, k, HPG * TQ), lambda b, g, i: (g, 0, 0))
    return pl.pallas_call(
        functools.partial(_nsa_body, top_k=top_k, n_slc=n_slc),
        grid=(B, G, nt),
        in_specs=[
            pl.BlockSpec((None, None, HPG * DH, TQ), lambda b, g, i: (b, g, 0, i)),
            pl.BlockSpec((None, gate_rows, TQ), lambda b, g, i: (g, 0, b * nt + i)),
            whole((NH, DH)), whole((DH, NH)),
            whole((S + SP, MR + DH)), whole((DH, S + SP)),
            whole((S + WIN, NSA_WIN_MASK_ROWS + DH)), whole((DH, S + WIN)),
            pl.BlockSpec((None, None, NH, HPG * TQ), lambda b, g, i: (g, i, 0, 0)),
            tile(WIN + TQ), tile(2 * TQ),
            pl.BlockSpec((n_slc, NH), lambda b, g, i: (0, 0)),
        ],
        out_specs=pl.BlockSpec((TQ, HPG * DH), lambda b, g, i: (b * nt + i, g)),
        out_shape=jax.ShapeDtypeStruct((B * S, NSA_WIDTH), F32),
        compiler_params=_cparams("arbitrary", "arbitrary", "arbitrary"),
        name="nsa_attention",
    )(qt, glog_g, kcb, vct, ks, vst, kw, vwt, bias_cmp, bias_win, bias_near,
      jnp.asarray(overlap_t, BF16))


ROUTE_E1, ROUTE_E2, ROUTE_W1, ROUTE_W2 = 0, 1, 2, 3


def _outproj_body(x_ref, yh_ref, yn_ref, wo_ref, g2_ref, wrh_ref, wrl_ref, br_ref,
                  x2_ref, h2_ref, rt_ref):
    x2 = (x_ref[...] + _dot(yh_ref[...].astype(BF16), wo_ref[0:HG_WIDTH, :])
          + _dot(yn_ref[...].astype(BF16), wo_ref[HG_WIDTH:HG_WIDTH + NSA_WIDTH, :]))
    x2_ref[...] = x2
    h = x2 * lax.rsqrt(jnp.mean(x2 * x2, axis=-1, keepdims=True) + RMS_EPS) * g2_ref[...]
    hb = h.astype(BF16)
    hbf = hb.astype(F32)
    half = h.shape[1] // 2
    bits = pltpu.bitcast(hbf, jnp.int32)
    h2_ref[...] = (lax.shift_right_logical(bits[:, :half], 16)
                   | (bits[:, half:] & jnp.int32(-65536)))
    hl = (h - hbf).astype(BF16)
    wrh = wrh_ref[...]
    lg = _dot(hb, wrh) + _dot(hl, wrh) + _dot(hb, wrl_ref[...]) + br_ref[...]
    lane = lax.broadcasted_iota(jnp.int32, lg.shape, 1)
    first = lambda hit: jnp.min(jnp.where(hit, lane, LANES), axis=-1, keepdims=True)
    lgg = jnp.where(lane < N_EXPERT_GROUPS, lg, NEG_INF)
    mg = jnp.max(lgg, axis=-1, keepdims=True)
    p_top = 1.0 / jnp.sum(jnp.exp(lgg - mg), axis=-1, keepdims=True)
    lo = N_EXPERT_GROUPS + EXPERTS_PER_GROUP * first(lgg == mg)
    le = jnp.where((lane >= lo) & (lane < lo + EXPERTS_PER_GROUP), lg, NEG_INF)
    v1 = jnp.max(le, axis=-1, keepdims=True)
    i1 = first(le == v1)
    le = jnp.where(lane == i1, NEG_INF, le)
    v2 = jnp.max(le, axis=-1, keepdims=True)
    i2 = first(le == v2)
    e21 = jnp.exp(v2 - v1)
    w1 = p_top / (1.0 + e21)
    rt = jnp.where(lane == ROUTE_E1, (i1 - N_EXPERT_GROUPS).astype(F32), 0.0)
    rt = jnp.where(lane == ROUTE_E2, (i2 - N_EXPERT_GROUPS).astype(F32), rt)
    rt = jnp.where(lane == ROUTE_W1, w1, rt)
    rt_ref[...] = jnp.where(lane == ROUTE_W2, w1 * e21, rt)


def _outproj(x2d, y_hg, y_nsa, w_out, g2, wr_hi, wr_lo, b_r, tm):
    T, D = x2d.shape
    row = lambda n: pl.BlockSpec((tm, n), lambda i: (i, 0))
    full = lambda a: pl.BlockSpec(a.shape, lambda i: (0, 0))
    return pl.pallas_call(
        _outproj_body,
        grid=(T // tm,),
        in_specs=[row(D), row(HG_WIDTH), row(NSA_WIDTH), full(w_out), full(g2),
                  full(wr_hi), full(wr_lo), full(b_r)],
        out_specs=[row(D), row(D // 2), row(LANES)],
        out_shape=[jax.ShapeDtypeStruct((T, D), F32), jax.ShapeDtypeStruct((T, D // 2), jnp.int32),
                   jax.ShapeDtypeStruct((T, LANES), F32)],
        compiler_params=_cparams("arbitrary"),
        name="outproj_router",
    )(x2d, y_hg, y_nsa, w_out, g2, wr_hi, wr_lo, b_r)


ROW_TILE = 256


def _dispatch_body(dest_ref, h_ref, xs0_hbm, xs_hbm, sem):
    del xs0_hbm
    tb = h_ref.shape[0]

    def start(t, carry):
        for k in range(2):
            pltpu.make_async_copy(h_ref.at[pl.ds(t, 1), :],
                                  xs_hbm.at[pl.ds(dest_ref[0, 2 * t + k], 1), :], sem).start()
        return carry

    lax.fori_loop(0, tb, start, 0)
    for _ in range(2):
        pltpu.make_async_copy(h_ref, xs_hbm.at[pl.ds(0, tb), :], sem).wait()


def _dispatch(h2p, dest, n_rows):
    T, W = h2p.shape
    tb = ROW_TILE
    return pl.pallas_call(
        _dispatch_body,
        grid=(T // tb,),
        in_specs=[pl.BlockSpec((None, 1, 2 * tb), lambda i: (i, 0, 0), memory_space=pltpu.SMEM),
                  pl.BlockSpec((tb, W), lambda i: (i, 0)),
                  pl.BlockSpec(memory_space=pl.ANY)],
        out_specs=pl.BlockSpec(memory_space=pl.ANY),
        out_shape=jax.ShapeDtypeStruct((n_rows, W), jnp.int32),
        scratch_shapes=[pltpu.SemaphoreType.DMA(())],
        input_output_aliases={2: 0},
        compiler_params=_cparams("arbitrary"),
        name="moe_dispatch",
    )(dest.reshape(T // tb, 1, 2 * tb), h2p, jnp.zeros((n_rows, W), jnp.int32))


MOE_ALIGN = 128
MOE_TM = 1024
MOE_SUB = 256
MOE_TF = 256


def _moe_body(ie_ref, ir_ref, in_ref, xs_hbm, wg_ref, wu_ref, wd_ref, y0_hbm, y_hbm,
              xbuf, xlo, xhi, acc, sem_in, sem_out):
    del ie_ref, y0_hbm
    w = pl.program_id(0)
    f = pl.program_id(1)
    nsub = in_ref[w]
    row0 = pl.multiple_of(ir_ref[w] * MOE_ALIGN, MOE_ALIGN)
    n_mm = (nsub * MOE_ALIGN + MOE_SUB - 1) // MOE_SUB
    half = xlo.shape[1]

    @pl.when((f == 0) & (nsub > 0))
    def _():
        cp = pltpu.make_async_copy(xs_hbm.at[pl.ds(row0, MOE_TM), :], xbuf, sem_in)
        cp.start()
        acc[...] = jnp.zeros_like(acc)
        cp.wait()

        def unpack(j, carry):
            rows = pl.ds(pl.multiple_of(j * MOE_SUB, MOE_SUB), MOE_SUB)
            bits = xbuf[rows, :]
            xlo[rows, :] = pltpu.bitcast(lax.shift_left(bits, 16), F32).astype(BF16)
            xhi[rows, :] = pltpu.bitcast(bits & jnp.int32(-65536), F32).astype(BF16)
            return carry

        lax.fori_loop(0, n_mm, unpack, 0)

    @pl.when(nsub > 0)
    def _():
        wg = wg_ref[...].astype(BF16)
        wu = wu_ref[...].astype(BF16)
        wd = wd_ref[...].astype(BF16)

        def sub(j, carry):
            rows = pl.ds(pl.multiple_of(j * MOE_SUB, MOE_SUB), MOE_SUB)
            lo = xlo[rows, :]
            hi = xhi[rows, :]
            hg = _dot(lo, wg[0:half, :]) + _dot(hi, wg[half:2 * half, :])
            hu = _dot(lo, wu[0:half, :]) + _dot(hi, wu[half:2 * half, :])
            acc[rows, :] += _dot((jax.nn.silu(hg) * hu).astype(BF16), wd)
            return carry

        lax.fori_loop(0, n_mm, sub, 0)

    @pl.when((f == pl.num_programs(1) - 1) & (nsub > 0))
    def _():
        def out_copy(j):
            r = pl.multiple_of(j * MOE_ALIGN, MOE_ALIGN)
            return pltpu.make_async_copy(acc.at[pl.ds(r, MOE_ALIGN), :],
                                         y_hbm.at[pl.ds(row0 + r, MOE_ALIGN), :], sem_out)

        def start(j, carry):
            out_copy(j).start()
            return carry

        def wait(j, carry):
            out_copy(j).wait()
            return carry

        lax.fori_loop(0, nsub, start, 0)
        lax.fori_loop(0, nsub, wait, 0)


def _moe(xs, item_e, item_r, item_n, w_gate, w_up, w_down):
    rows = xs.shape[0]
    D = 2 * xs.shape[1]
    n_items = item_e.shape[0]
    nf = EXPERT_DFF // MOE_TF

    def fcol(f, nn, w):
        return jnp.where(nn[w] > 0, f, nf - 1)

    return pl.pallas_call(
        _moe_body,
        grid_spec=pltpu.PrefetchScalarGridSpec(
            num_scalar_prefetch=3,
            grid=(n_items, nf),
            in_specs=[
                pl.BlockSpec(memory_space=pl.ANY),
                pl.BlockSpec((None, D, MOE_TF), lambda w, f, ie, ir, nn: (ie[w], 0, fcol(f, nn, w))),
                pl.BlockSpec((None, D, MOE_TF), lambda w, f, ie, ir, nn: (ie[w], 0, fcol(f, nn, w))),
                pl.BlockSpec((None, MOE_TF, D), lambda w, f, ie, ir, nn: (ie[w], fcol(f, nn, w), 0)),
                pl.BlockSpec(memory_space=pl.ANY),
            ],
            out_specs=pl.BlockSpec(memory_space=pl.ANY),
            scratch_shapes=[pltpu.VMEM((MOE_TM, D // 2), jnp.int32),
                            pltpu.VMEM((MOE_TM, D // 2), BF16), pltpu.VMEM((MOE_TM, D // 2), BF16),
                            pltpu.VMEM((MOE_TM, D), F32),
                            pltpu.SemaphoreType.DMA(()), pltpu.SemaphoreType.DMA(())],
        ),
        out_shape=jax.ShapeDtypeStruct((rows, D), F32),
        input_output_aliases={7: 0},
        compiler_params=_cparams("arbitrary", "arbitrary"),
        name="moe_experts",
    )(item_e, item_r, item_n, xs, w_gate, w_up, w_down, jnp.zeros((rows, D), F32))


def _final_body(dcur_ref, dnext_ref, x2_ref, rt_ref, g_ref, ys_hbm, o_ref, ybuf, sems):
    i = pl.program_id(0)
    tb = x2_ref.shape[0]
    slot = i % 2

    def gather(d_ref, s):
        def body(t, carry):
            for k in range(2):
                pltpu.make_async_copy(ys_hbm.at[pl.ds(d_ref[0, 2 * t + k], 1), :],
                                      ybuf.at[s, k, pl.ds(t, 1), :], sems.at[s]).start()
            return carry

        lax.fori_loop(0, tb, body, 0)

    @pl.when(i == 0)
    def _():
        gather(dcur_ref, 0)

    @pl.when(i + 1 < pl.num_programs(0))
    def _():
        gather(dnext_ref, 1 - slot)

    for k in range(2):
        pltpu.make_async_copy(ys_hbm.at[pl.ds(0, tb), :], ybuf.at[slot, k], sems.at[slot]).wait()
    rt = rt_ref[...]
    x = (x2_ref[...] + rt[:, ROUTE_W1:ROUTE_W1 + 1] * ybuf[slot, 0]
         + rt[:, ROUTE_W2:ROUTE_W2 + 1] * ybuf[slot, 1])
    o_ref[...] = x * lax.rsqrt(jnp.mean(x * x, axis=-1, keepdims=True) + RMS_EPS) * g_ref[...]


def _final(x2, ys, dest, rt, g):
    T, D = x2.shape
    tb = ROW_TILE
    nb = T // tb
    dest3 = dest.reshape(nb, 1, 2 * tb)
    return pl.pallas_call(
        _final_body,
        grid=(nb,),
        in_specs=[pl.BlockSpec((None, 1, 2 * tb), lambda i: (i, 0, 0), memory_space=pltpu.SMEM),
                  pl.BlockSpec((None, 1, 2 * tb), lambda i: (jnp.minimum(i + 1, nb - 1), 0, 0),
                               memory_space=pltpu.SMEM),
                  pl.BlockSpec((tb, D), lambda i: (i, 0)),
                  pl.BlockSpec((tb, LANES), lambda i: (i, 0)),
                  pl.BlockSpec((1, D), lambda i: (0, 0)),
                  pl.BlockSpec(memory_space=pl.ANY)],
        out_specs=pl.BlockSpec((tb, D), lambda i: (i, 0)),
        out_shape=jax.ShapeDtypeStruct((T, D), F32),
        scratch_shapes=[pltpu.VMEM((2, 2, tb, D), F32), pltpu.SemaphoreType.DMA((2,))],
        compiler_params=_cparams("arbitrary"),
        name="combine_final_norm",
    )(dest3, dest3, x2, rt, g, ys)


def _moe_layout(e_flat):
    A = e_flat.shape[0]
    onehot = (e_flat[:, None] == jnp.arange(N_EXPERTS, dtype=jnp.int32)[None, :]).astype(jnp.int32)
    csum = jnp.cumsum(onehot, axis=0)
    counts = csum[-1]
    nblk = (counts + MOE_ALIGN - 1) // MOE_ALIGN
    blk0 = jnp.cumsum(nblk) - nblk
    dest = jnp.sum(onehot * (blk0[None, :] * MOE_ALIGN + csum - 1), axis=1).astype(jnp.int32)
    per_item = MOE_TM // MOE_ALIGN
    nitem = (nblk + per_item - 1) // per_item
    iend = jnp.cumsum(nitem)
    n_items = (A // MOE_ALIGN + N_EXPERTS + N_EXPERTS * (per_item - 1)) // per_item
    w = jnp.arange(n_items, dtype=jnp.int32)
    total = iend[-1]
    wv = jnp.minimum(w, total - 1)
    item_e = jnp.minimum(jnp.searchsorted(iend, wv, side='right'), N_EXPERTS - 1).astype(jnp.int32)
    sb = wv - (iend - nitem)[item_e]
    item_r = (blk0[item_e] + sb * per_item).astype(jnp.int32)
    item_n = jnp.where(w < total, jnp.minimum(per_item, nblk[item_e] - sb * per_item), 0).astype(jnp.int32)
    return dest, item_e, item_r, item_n


def kernel(x, norm1_g, w_in, hg_lb_logits, hg_norm_g, cmp_pe_k, cmp_w1_k, cmp_w2_k, cmp_pe_v,
           cmp_w1_v, cmp_w2_v, rel_bias, w_out, norm2_g, w_router_group, b_router_group,
           w_router_expert, b_router_expert, w_expert_gate, w_expert_up, w_expert_down,
           final_norm_g):
    B, S, D = x.shape
    T = B * S
    assert w_in.shape[0] == 1, "single-layer block"
    x2d = x.reshape(T, D)

    lower = jax.nn.softmax(hg_lb_logits.astype(F32), axis=0)[0].reshape(HG_HEADS, 1, HG_DIM)

    w_in_b = w_in[0].astype(BF16)
    w_gate_cols = jnp.pad(w_in_b[:, MAIN_COLS:], ((0, 0), (0, LANES - GATE_COLS)))
    proj, glog = _inproj(x2d, norm1_g[0].reshape(1, D), w_in_b[:, :MAIN_COLS], w_gate_cols,
                         tm=1024, tn=512)

    y_hg = _hgrn(proj, lower, hg_norm_g[0].reshape(1, HG_DIM), B, S, tb=512)
    kcvc = _nsa_compress_all(proj, cmp_pe_k[0], cmp_w1_k[0], cmp_w2_k[0],
                             cmp_pe_v[0], cmp_w1_v[0], cmp_w2_v[0], B, S)
    y_nsa = _nsa(proj, glog, kcvc[0], kcvc[1], rel_bias, B, S)

    w_r = jnp.concatenate([w_router_group[0], w_router_expert[0]], axis=1)
    w_r = jnp.pad(w_r, ((0, 0), (0, LANES - w_r.shape[1])))
    wr_hi = w_r.astype(BF16)
    wr_lo = (w_r - wr_hi.astype(F32)).astype(BF16)
    b_r = jnp.concatenate([b_router_group[0], b_router_expert[0]])
    b_r = jnp.pad(b_r, (0, LANES - b_r.shape[0])).reshape(1, LANES)
    x2, h2, rt = _outproj(x2d, y_hg, y_nsa, w_out[0].astype(BF16), norm2_g[0].reshape(1, D),
                          wr_hi, wr_lo, b_r, tm=256)

    e_flat = rt[:, ROUTE_E1:ROUTE_E2 + 1].astype(jnp.int32).reshape(2 * T)
    dest, item_e, item_r, item_n = _moe_layout(e_flat)
    n_rows = (2 * T // MOE_ALIGN + N_EXPERTS) * MOE_ALIGN + MOE_TM
    xs = _dispatch(h2, dest, n_rows)
    ys = _moe(xs, item_e, item_r, item_n, w_expert_gate[0], w_expert_up[0], w_expert_down[0])

    out = _final(x2, ys, dest, rt, final_norm_g.reshape(1, D))
    return out.reshape(B, S, D)
```

```python
import functools
import math

import jax
import jax.numpy as jnp
import numpy as np
from jax import lax
from jax.experimental import pallas as pl
from jax.experimental.pallas import tpu as pltpu

F32 = jnp.float32
BF16 = jnp.bfloat16

D_MODEL = 2048
HG_HEADS = 8
HG_DIM = 128
HG_WIDTH = HG_HEADS * HG_DIM
HG_CHUNK = 64
NSA_HEADS = 16
NSA_GROUPS = 4
NSA_HPG = NSA_HEADS // NSA_GROUPS
NSA_DH = 64
NSA_WIDTH = NSA_HEADS * NSA_DH
NSA_KV = NSA_GROUPS * NSA_DH
CMP_LEN = 32
CMP_STRIDE = 16
CMP_HIDDEN = 256
SLC_LEN = 64
SLC_TOPK = 16
WIN = 512
REL_BUCKETS = 32
REL_MAX_DIST = 128
N_EXPERT_GROUPS = 8
EXPERTS_PER_GROUP = 8
N_EXPERTS = N_EXPERT_GROUPS * EXPERTS_PER_GROUP
EXPERT_DFF = 1024
RMS_EPS = 1e-6
NEG_INF = -1e30
BIG = 1e9

MAIN_COLS = 4 * HG_WIDTH + NSA_WIDTH + 6 * NSA_KV
GATE_COLS = 3 * NSA_HEADS
LANES = 128
VMEM_LIMIT = 56 * 1024 * 1024


def _cparams(*sem):
    return pltpu.CompilerParams(dimension_semantics=sem, vmem_limit_bytes=VMEM_LIMIT)


def _nt(a, b):
    return lax.dot_general(a, b, (((1,), (1,)), ((), ())), preferred_element_type=F32)


def _tn(a, b):
    return lax.dot_general(a, b, (((0,), (0,)), ((), ())), preferred_element_type=F32)


def _dot(a, b):
    return jnp.dot(a, b, preferred_element_type=F32)


def _inproj_body(x_ref, g_ref, w_ref, wg_ref, o_ref, og_ref, h_scr):
    @pl.when(pl.program_id(1) == 0)
    def _():
        x = x_ref[...]
        y = x * lax.rsqrt(jnp.mean(x * x, axis=-1, keepdims=True) + RMS_EPS) * g_ref[...]
        h_scr[...] = y.astype(BF16)
        og_ref[...] = _dot(h_scr[...], wg_ref[...])

    o_ref[...] = _dot(h_scr[...], w_ref[...])


def _inproj(x2d, g, w_main, w_gate, tm, tn):
    T, D = x2d.shape
    N = w_main.shape[1]
    return pl.pallas_call(
        _inproj_body,
        grid=(T // tm, N // tn),
        in_specs=[
            pl.BlockSpec((tm, D), lambda i, j: (i, 0)),
            pl.BlockSpec((1, D), lambda i, j: (0, 0)),
            pl.BlockSpec((D, tn), lambda i, j: (0, j)),
            pl.BlockSpec((D, LANES), lambda i, j: (0, 0)),
        ],
        out_specs=[
            pl.BlockSpec((tm, tn), lambda i, j: (i, j)),
            pl.BlockSpec((tm, LANES), lambda i, j: (i, 0)),
        ],
        out_shape=[jax.ShapeDtypeStruct((T, N), F32),
                   jax.ShapeDtypeStruct((T, LANES), F32)],
        scratch_shapes=[pltpu.VMEM((tm, D), BF16)],
        compiler_params=_cparams("arbitrary", "arbitrary"),
        name="inproj",
    )(x2d, g, w_main, w_gate)


def _hgrn_tables(C):
    t = np.arange(C)
    levels = []
    m = C // 2
    while m >= 1:
        levels.append(m)
        m //= 2
    rows = [np.tril(np.ones((C, C), np.float32)),
            np.triu(np.ones((C, C), np.float32), 1)]
    lev = np.full((C, C), -1, np.int32)
    for li, m in enumerate(levels):
        r = t % (2 * m)
        mid = t - r + m - 1
        M = np.zeros((C, C), np.float32)
        for i in range(C):
            if r[i] >= m:
                M[i, mid[i] + 1:i + 1] = 1.0
            else:
                M[i, i + 1:mid[i] + 1] = 1.0
        rows.append(M)
        same = (t[:, None] // (2 * m)) == (t[None, :] // (2 * m))
        lev[same & (r[:, None] >= m) & (r[None, :] < m)] = li
    lev[t[:, None] == t[None, :]] = len(levels)
    return np.concatenate(rows, 0), lev, len(levels)


HG_HEADS_PER_STEP = 8


def _hgrn_body(q_ref, f_ref, i_ref, g_ref, lb_ref, ng_ref, mst_ref, lev_ref, o_ref, st_ref,
               *, C, n_lev):
    @pl.when(pl.program_id(2) == 0)
    def _():
        st_ref[...] = jnp.zeros_like(st_ref)

    ng = ng_ref[...]
    mst = mst_ref[...]
    lev = lev_ref[...]
    n_chunks = q_ref.shape[0] // C

    def gates(rows, hb):
        cols = slice(hb * HG_DIM, (hb + 1) * HG_DIM)
        lb = lb_ref[hb]
        q = jax.nn.silu(q_ref[rows, cols])
        f = lb + (1.0 - lb) * jax.nn.sigmoid(f_ref[rows, cols])
        lf = jnp.log(f)
        lf0 = lf.astype(BF16)
        r1 = lf - lf0.astype(F32)
        lf1 = r1.astype(BF16)
        lf2 = (r1 - lf1.astype(F32)).astype(BF16)
        e = _dot(mst, jnp.concatenate([lf0, lf1, lf2], axis=0))
        return q, 1.0 - f, i_ref[rows, cols].astype(BF16), e

    def scores(q, k, gdec):
        a = jnp.where(lev == n_lev, _nt(q.astype(BF16), k.astype(BF16)), 0.0)
        for li in range(n_lev):
            g_m = gdec[(2 + li) * C:(3 + li) * C]
            a_m = _nt((q * g_m).astype(BF16), (k * g_m).astype(BF16))
            a = jnp.where(lev == li, a_m, a)
        return a.astype(BF16)

    def finish(rows, hb, q, k, v, gdec, a):
        cols = slice(hb * HG_DIM, (hb + 1) * HG_DIM)
        g_b = gdec[0:C]
        g_r = gdec[C:2 * C]
        st = st_ref[hb]
        o = _nt((q * g_b).astype(BF16), st.astype(BF16)) + _dot(a, v)
        g_last = g_b[C - 1:C, :]
        st_ref[hb] = st * g_last + _tn(v, (k * g_r).astype(BF16))
        o = o * lax.rsqrt(jnp.mean(o * o, axis=-1, keepdims=True) + RMS_EPS) * ng
        o_ref[rows, cols] = (o * jax.nn.silu(g_ref[rows, cols])).astype(o_ref.dtype)

    def chunk(c, carry):
        rows = pl.ds(pl.multiple_of(c * C, C), C)
        hs = range(HG_HEADS_PER_STEP)
        qkve = [gates(rows, hb) for hb in hs]
        gdec = [jnp.exp(x[3]) for x in qkve]
        a = [scores(qkve[hb][0], qkve[hb][1], gdec[hb]) for hb in hs]
        for hb in hs:
            finish(rows, hb, qkve[hb][0], qkve[hb][1], qkve[hb][2], gdec[hb], a[hb])
        return carry

    lax.fori_loop(0, n_chunks, chunk, 0)


def _hgrn(proj, lb, ng, B, S, tb):
    C = HG_CHUNK
    mst, lev, n_lev = _hgrn_tables(C)
    mst = np.concatenate([mst] * 3, axis=1)
    nb = S // tb
    HB = HG_HEADS_PER_STEP
    hsteps = HG_HEADS // HB

    def col(group):
        return pl.BlockSpec((tb, HB * HG_DIM), lambda b, h, t: (b * nb + t, group * hsteps + h))

    return pl.pallas_call(
        functools.partial(_hgrn_body, C=C, n_lev=n_lev),
        grid=(B, hsteps, nb),
        in_specs=[
            col(0), col(1), col(2), col(3),
            pl.BlockSpec((HB, 1, HG_DIM), lambda b, h, t: (h, 0, 0)),
            pl.BlockSpec((1, HG_DIM), lambda b, h, t: (0, 0)),
            pl.BlockSpec(mst.shape, lambda b, h, t: (0, 0)),
            pl.BlockSpec(lev.shape, lambda b, h, t: (0, 0)),
        ],
        out_specs=pl.BlockSpec((tb, HB * HG_DIM), lambda b, h, t: (b * nb + t, h)),
        out_shape=jax.ShapeDtypeStruct((B * S, HG_WIDTH), BF16),
        scratch_shapes=[pltpu.VMEM((HB, HG_DIM, HG_DIM), F32)],
        compiler_params=_cparams("arbitrary", "arbitrary", "arbitrary"),
        name="hgrn2",
    )(proj, proj, proj, proj, lb, ng, jnp.asarray(mst, BF16), jnp.asarray(lev))


def _compress_body(h_ref, pe_ref, w1_ref, w2_ref, o_ref):
    hv = h_ref[...]
    half = hv.shape[1]
    nh = hv.shape[0]
    u = _dot((hv + pe_ref[0:1, :]).astype(BF16), w1_ref[0:half, :].astype(BF16))
    v = _dot((hv + pe_ref[1:2, :]).astype(BF16), w1_ref[half:2 * half, :].astype(BF16))
    pre = u + pltpu.roll(v, nh - 1, axis=0)
    o_ref[...] = _dot(jax.nn.silu(pre).astype(BF16), w2_ref[...].astype(BF16))


def _compress(halves, pe, w1, w2):
    _, B, G, NH, HW = halves.shape
    return pl.pallas_call(
        _compress_body,
        grid=(2, B, G),
        in_specs=[
            pl.BlockSpec((None, None, None, NH, HW), lambda s, b, g: (s, b, g, 0, 0)),
            pl.BlockSpec((None, 2, HW), lambda s, b, g: (s, 0, 0)),
            pl.BlockSpec((None, 2 * HW, CMP_HIDDEN), lambda s, b, g: (s, 0, 0)),
            pl.BlockSpec((None, CMP_HIDDEN, NSA_DH), lambda s, b, g: (s, 0, 0)),
        ],
        out_specs=pl.BlockSpec((None, None, None, NH, NSA_DH), lambda s, b, g: (s, b, g, 0, 0)),
        out_shape=jax.ShapeDtypeStruct((2, B, G, NH, NSA_DH), F32),
        compiler_params=_cparams("arbitrary", "arbitrary", "arbitrary"),
        name="nsa_compress",
    )(halves, pe, w1, w2)


def _nsa_compress_all(proj, pe_k, w1_k, w2_k, pe_v, w1_v, w2_v, B, S):
    NH = S // CMP_STRIDE
    c0 = 4 * HG_WIDTH + NSA_WIDTH

    def halves(idx):
        t = proj[:, c0 + idx * NSA_KV:c0 + (idx + 1) * NSA_KV]
        t = t.reshape(B, NH, CMP_STRIDE, NSA_GROUPS, NSA_DH).transpose(0, 3, 1, 2, 4)
        return t.reshape(B, NSA_GROUPS, NH, CMP_STRIDE * NSA_DH)

    hw = CMP_STRIDE * NSA_DH
    return _compress(jnp.stack([halves(0), halves(1)]),
                     jnp.stack([pe_k.reshape(2, hw), pe_v.reshape(2, hw)]),
                     jnp.stack([w1_k, w1_v]), jnp.stack([w2_k, w2_v]))


NSA_TQ = 128


NSA_MASK_ROWS = 128
NSA_WIN_MASK_ROWS = 16
NSA_FAR = 4
NSA_SLC_PAD = (NSA_FAR - 1) * NSA_TQ


def _softmax_cols(s):
    m = jnp.max(s, axis=0, keepdims=True)
    p = jnp.exp(s - m)
    return m, p, jnp.sum(p, axis=0, keepdims=True)


NSA_GROUPS_PER_STEP = 2


def _nsa_body(qt_ref, gl_ref, kc_ref, vct_ref, ks_ref, vst_ref, kw_ref, vwt_ref,
              bc_ref, bw_ref, bn_ref, ovt_ref, o_ref, *, top_k, n_slc):
    TQ = NSA_TQ
    GG = NSA_GROUPS_PER_STEP
    i = pl.program_id(2)
    ovt = ovt_ref[...]
    jb = lax.broadcasted_iota(jnp.int32, (n_slc, TQ), 0)
    qblk = (i * TQ + lax.broadcasted_iota(jnp.int32, (n_slc, TQ), 1)) // SLC_LEN
    forced = (jb == 0) | (jb == qblk) | (jb == qblk - 1)
    future = jb > qblk
    near = pl.ds(pl.multiple_of(NSA_SLC_PAD + (i - 1) * TQ, TQ), 2 * TQ)

    def heads(a):
        return jnp.concatenate([a] * NSA_HPG, axis=1)

    def queries(g):
        qt = qt_ref[g]
        return jnp.concatenate([qt[h * NSA_DH:(h + 1) * NSA_DH, :] for h in range(NSA_HPG)],
                               axis=1)

    def compressed(g, s):
        m = jnp.max(s, axis=0, keepdims=True)
        p = jnp.where(s > 0.5 * NEG_INF, jnp.exp(s - m), 0.0)
        l = jnp.sum(p, axis=0, keepdims=True)
        pc = p * (1.0 / jnp.maximum(l, 1e-30))
        o_cmp = _dot(vct_ref[g], pc.astype(BF16))
        ps = pc[:, 0:TQ] + pc[:, TQ:2 * TQ] + pc[:, 2 * TQ:3 * TQ] + pc[:, 3 * TQ:4 * TQ]
        ps_hi = ps.astype(BF16)
        ps_lo = (ps - ps_hi.astype(F32)).astype(BF16)
        imp = _dot(ovt, ps_hi) + _dot(ovt, ps_lo)
        return o_cmp, jnp.where(forced, BIG, jnp.where(future, -BIG, imp))

    def window(g, s):
        m, p, l = _softmax_cols(s)
        return _dot(vwt_ref[g, :, win], p.astype(BF16)) * (1.0 / l)

    def select(imp, q4):
        slabs = []
        for v in range(n_slc // 8):
            slab = imp[8 * v:8 * v + 8, :]
            jbs = jb[8 * v:8 * v + 8, :]
            cnt = jnp.zeros((8, TQ), F32)
            for jp in range(n_slc):
                row = imp[jp:jp + 1, :]
                if jp < 8 * v:
                    beats = row >= slab
                elif jp >= 8 * v + 8:
                    beats = row > slab
                else:
                    beats = (row > slab) | ((row == slab) & (jbs > jp))
                cnt = cnt + jnp.where(beats, 1.0, 0.0)
            slabs.append(jnp.where(cnt < top_k, 0.0, 1.0))
        unsel = jnp.concatenate(slabs + [jnp.ones((NSA_MASK_ROWS - n_slc, TQ), F32)], axis=0)
        return jnp.concatenate([heads(unsel.astype(BF16)), q4], axis=0)

    def near_tiles(g, s):
        m, p, l = _softmax_cols(s)
        return m, l, _dot(vst_ref[g, :, near], p.astype(BF16))

    win = pl.ds(pl.multiple_of(i * TQ, TQ), WIN + TQ)
    pad_row = jnp.where(lax.broadcasted_iota(jnp.int32, (NSA_WIN_MASK_ROWS, NSA_HPG * TQ), 0) == 0,
                        1.0, 0.0).astype(BF16)
    gs = range(GG)
    q4 = [queries(g) for g in gs]
    s_cmp = [_dot(kc_ref[g], q4[g]) + bc_ref[g] for g in gs]
    s_win = [_dot(kw_ref[g, win, :], jnp.concatenate([pad_row, q4[g]], axis=0)) + bw_ref[g] for g in gs]
    cmp_out = [compressed(g, s_cmp[g]) for g in gs]
    o_win = [window(g, s_win[g]) for g in gs]
    qa = [select(cmp_out[g][1], q4[g]) for g in gs]
    s_near = [_dot(ks_ref[g, near, :], qa[g]) + bn_ref[g] for g in gs]
    start = tuple(near_tiles(g, s_near[g]) for g in gs)

    n_far = i - 1
    n_chunks = jnp.maximum((n_far + NSA_FAR - 1) // NSA_FAR, 0)
    far0 = NSA_SLC_PAD + TQ * (n_far - NSA_FAR * n_chunks)

    def far_step(c, carry):
        keys = pl.ds(pl.multiple_of(far0 + c * NSA_FAR * TQ, TQ), NSA_FAR * TQ)
        out = []
        logits = [_dot(ks_ref[g, keys, :], qa[g]) for g in gs]
        for g in gs:
            m, l, acc = carry[g]
            s = logits[g]
            m_new = jnp.maximum(m, jnp.max(s, axis=0, keepdims=True))
            alpha = jnp.exp(m - m_new)
            p = jnp.exp(s - m_new)
            l = alpha * l + jnp.sum(p, axis=0, keepdims=True)
            acc = alpha * acc + _dot(vst_ref[g, :, keys], p.astype(BF16))
            out.append((m_new, l, acc))
        return tuple(out)

    far = lax.fori_loop(0, n_chunks, far_step, start)

    for g in gs:
        _, l, acc = far[g]
        o_slc = acc * (1.0 / l)
        o_cmp = cmp_out[g][0]
        sg = jax.nn.sigmoid(gl_ref[g])
        outs = []
        for h in range(NSA_HPG):
            lanes = slice(h * TQ, (h + 1) * TQ)
            outs.append(sg[h:h + 1, :] * o_cmp[:, lanes]
                        + sg[NSA_HPG + h:NSA_HPG + h + 1, :] * o_slc[:, lanes]
                        + sg[2 * NSA_HPG + h:2 * NSA_HPG + h + 1, :] * o_win[g][:, lanes])
        width = NSA_HPG * NSA_DH
        o_ref[:, g * width:(g + 1) * width] = jnp.concatenate(outs, axis=0).T.astype(o_ref.dtype)


def _rel_bucket(dist):
    n = jnp.maximum(dist, 0)
    max_exact = REL_BUCKETS // 2
    nf = jnp.maximum(n, 1).astype(F32)
    large = max_exact + (jnp.log(nf / max_exact) / math.log(REL_MAX_DIST / max_exact)
                         * (REL_BUCKETS - max_exact)).astype(jnp.int32)
    large = jnp.minimum(large, REL_BUCKETS - 1)
    return jnp.where(n < max_exact, n, large)


def _bias_table(table, dist, valid):
    onehot = (_rel_bucket(jnp.asarray(dist, jnp.int32))[..., None]
              == jnp.arange(REL_BUCKETS, dtype=jnp.int32)).astype(F32)
    b = jnp.einsum('rmk,ghk->ghrm', onehot, table, precision=lax.Precision.HIGHEST)
    return jnp.where(jnp.asarray(valid), b, NEG_INF)


def _nsa(proj, glog, kc, vc, rel_bias, B, S):
    TQ = NSA_TQ
    G, HPG, DH = NSA_GROUPS, NSA_HPG, NSA_DH
    NH = S // CMP_STRIDE
    n_cmp = (S - CMP_LEN) // CMP_STRIDE + 1
    n_slc = S // SLC_LEN
    top_k = min(SLC_TOPK, n_slc)
    nt = S // TQ
    c0 = 4 * HG_WIDTH + NSA_WIDTH

    MR = NSA_MASK_ROWS
    assert n_slc < MR and n_slc % 8 == 0

    def kv(idx, pad):
        t = proj[:, c0 + idx * NSA_KV:c0 + (idx + 1) * NSA_KV].reshape(B, S, G, DH)
        t = jnp.pad(t.astype(BF16), ((0, 0), (pad, 0), (0, 0), (0, 0)))
        return t.transpose(0, 2, 1, 3)

    def keys_with_masks(idx, pad, with_blocks):
        pos = np.arange(pad + S) - pad
        if with_blocks:
            cols = np.zeros((pad + S, MR), np.float32)
            cols[:, :n_slc] = ((pos // SLC_LEN)[:, None] == np.arange(n_slc)[None, :]) & (pos >= 0)[:, None]
            cols[:, n_slc] = pos < 0
        else:
            cols = np.zeros((pad + S, NSA_WIN_MASK_ROWS), np.float32)
            cols[:, 0] = pos < 0
        m = jnp.broadcast_to(jnp.asarray(cols * NEG_INF, BF16), (B, G) + cols.shape)
        return jnp.concatenate([m, kv(idx, pad)], axis=3)

    SP = NSA_SLC_PAD
    ks, vst = keys_with_masks(2, SP, True), kv(3, SP).transpose(0, 1, 3, 2)
    kw, vwt = keys_with_masks(4, WIN, False), kv(5, WIN).transpose(0, 1, 3, 2)
    kcb = kc.astype(BF16)
    vct = vc.astype(BF16).transpose(0, 1, 3, 2)
    qt = (proj[:, 4 * HG_WIDTH:c0] * (DH ** -0.5)).astype(BF16)
    qt = qt.reshape(B, S, G, HPG * DH).transpose(0, 2, 3, 1)

    table = rel_bias.T.reshape(G, HPG, REL_BUCKETS)
    far = table[:, :, REL_BUCKETS - 1]
    r = np.arange(TQ)[:, None]

    def keys_major(b):
        return b.transpose(0, 3, 1, 2).reshape(G, b.shape[3], HPG * TQ)

    per_tile = TQ // CMP_STRIDE
    d_c = r - (CMP_LEN - 1) - CMP_STRIDE * (np.arange(2 * NH)[None, :] - NH)
    rel_cmp = keys_major(_bias_table(table, d_c, d_c >= 0))
    bias_cmp = jnp.stack([rel_cmp[:, NH - per_tile * i:2 * NH - per_tile * i] for i in range(nt)],
                         axis=1)
    d_w = r + WIN - np.arange(WIN + TQ)[None, :]
    bias_win = keys_major(_bias_table(table, d_w, (d_w >= 0) & (d_w < WIN)))
    d_n = r + TQ - np.arange(2 * TQ)[None, :]
    bias_near = keys_major(_bias_table(table - far[:, :, None], d_n, d_n >= 0))

    ci = np.arange(NH)[None, :] * CMP_STRIDE
    sj = np.arange(n_slc)[:, None] * SLC_LEN
    overlap_t = ((ci <= sj + SLC_LEN - 1) & (ci + CMP_LEN - 1 >= sj) & (np.arange(NH)[None, :] < n_cmp))

    gate_rows = 16
    glog_g = glog[:, :GATE_COLS].reshape(B * S, 3, G, HPG).transpose(2, 1, 3, 0).reshape(G, 3 * HPG, B * S)
    glog_g = jnp.pad(glog_g, ((0, 0), (0, gate_rows - 3 * HPG), (0, 0)))

    GG = NSA_GROUPS_PER_STEP
    whole = lambda shape: pl.BlockSpec((None, GG) + shape, lambda b, g, i: (b, g, 0, 0))
    tile = lambda k: pl.BlockSpec((GG, k, HPG * TQ), lambda b, g, i: (g, 0, 0))
    return pl.pallas_call(
        functools.partial(_nsa_body, top_k=top_k, n_slc=n_slc),
        grid=(B, G // GG, nt),
        in_specs=[
            pl.BlockSpec((None, GG, HPG * DH, TQ), lambda b, g, i: (b, g, 0, i)),
            pl.BlockSpec((GG, gate_rows, TQ), lambda b, g, i: (g, 0, b * nt + i)),
            whole((NH, DH)), whole((DH, NH)),
            whole((S + SP, MR + DH)), whole((DH, S + SP)),
            whole((S + WIN, NSA_WIN_MASK_ROWS + DH)), whole((DH, S + WIN)),
            pl.BlockSpec((GG, None, NH, HPG * TQ), lambda b, g, i: (g, i, 0, 0)),
            tile(WIN + TQ), tile(2 * TQ),
            pl.BlockSpec((n_slc, NH), lambda b, g, i: (0, 0)),
        ],
        out_specs=pl.BlockSpec((TQ, GG * HPG * DH), lambda b, g, i: (b * nt + i, g)),
        out_shape=jax.ShapeDtypeStruct((B * S, NSA_WIDTH), BF16),
        compiler_params=_cparams("arbitrary", "arbitrary", "arbitrary"),
        name="nsa_attention",
    )(qt, glog_g, kcb, vct, ks, vst, kw, vwt, bias_cmp, bias_win, bias_near,
      jnp.asarray(overlap_t, BF16))


ROUTE_E1, ROUTE_E2, ROUTE_W1, ROUTE_W2 = 0, 1, 2, 3


def _outproj_body(x_ref, yh_ref, yn_ref, wo_ref, g2_ref, wrh_ref, wrl_ref, br_ref,
                  x2_ref, h2_ref, rt_ref):
    x2 = (x_ref[...] + _dot(yh_ref[...], wo_ref[0:HG_WIDTH, :])
          + _dot(yn_ref[...], wo_ref[HG_WIDTH:HG_WIDTH + NSA_WIDTH, :]))
    x2_ref[...] = x2
    h = x2 * lax.rsqrt(jnp.mean(x2 * x2, axis=-1, keepdims=True) + RMS_EPS) * g2_ref[...]
    hb = h.astype(BF16)
    hbf = hb.astype(F32)
    half = h.shape[1] // 2
    bits = pltpu.bitcast(hbf, jnp.int32)
    h2_ref[...] = (lax.shift_right_logical(bits[:, :half], 16)
                   | (bits[:, half:] & jnp.int32(-65536)))
    hl = (h - hbf).astype(BF16)
    wrh = wrh_ref[...]
    lg = _dot(hb, wrh) + _dot(hl, wrh) + _dot(hb, wrl_ref[...]) + br_ref[...]
    lane = lax.broadcasted_iota(jnp.int32, lg.shape, 1)
    first = lambda hit: jnp.min(jnp.where(hit, lane, LANES), axis=-1, keepdims=True)
    lgg = jnp.where(lane < N_EXPERT_GROUPS, lg, NEG_INF)
    mg = jnp.max(lgg, axis=-1, keepdims=True)
    p_top = 1.0 / jnp.sum(jnp.exp(lgg - mg), axis=-1, keepdims=True)
    lo = N_EXPERT_GROUPS + EXPERTS_PER_GROUP * first(lgg == mg)
    le = jnp.where((lane >= lo) & (lane < lo + EXPERTS_PER_GROUP), lg, NEG_INF)
    v1 = jnp.max(le, axis=-1, keepdims=True)
    i1 = first(le == v1)
    le = jnp.where(lane == i1, NEG_INF, le)
    v2 = jnp.max(le, axis=-1, keepdims=True)
    i2 = first(le == v2)
    e21 = jnp.exp(v2 - v1)
    w1 = p_top / (1.0 + e21)
    rt = jnp.where(lane == ROUTE_E1, (i1 - N_EXPERT_GROUPS).astype(F32), 0.0)
    rt = jnp.where(lane == ROUTE_E2, (i2 - N_EXPERT_GROUPS).astype(F32), rt)
    rt = jnp.where(lane == ROUTE_W1, w1, rt)
    rt_ref[...] = jnp.where(lane == ROUTE_W2, w1 * e21, rt)


def _outproj(x2d, y_hg, y_nsa, w_out, g2, wr_hi, wr_lo, b_r, tm):
    T, D = x2d.shape
    row = lambda n: pl.BlockSpec((tm, n), lambda i: (i, 0))
    full = lambda a: pl.BlockSpec(a.shape, lambda i: (0, 0))
    return pl.pallas_call(
        _outproj_body,
        grid=(T // tm,),
        in_specs=[row(D), row(HG_WIDTH), row(NSA_WIDTH), full(w_out), full(g2),
                  full(wr_hi), full(wr_lo), full(b_r)],
        out_specs=[row(D), row(D // 2), row(LANES)],
        out_shape=[jax.ShapeDtypeStruct((T, D), F32), jax.ShapeDtypeStruct((T, D // 2), jnp.int32),
                   jax.ShapeDtypeStruct((T, LANES), F32)],
        compiler_params=_cparams("arbitrary"),
        name="outproj_router",
    )(x2d, y_hg, y_nsa, w_out, g2, wr_hi, wr_lo, b_r)


ROW_TILE = 256


def _dispatch_body(dest_ref, h_ref, xs0_hbm, xs_hbm, sem):
    del xs0_hbm
    tb = h_ref.shape[0]

    def start(t, carry):
        for k in range(2):
            pltpu.make_async_copy(h_ref.at[pl.ds(t, 1), :],
                                  xs_hbm.at[pl.ds(dest_ref[0, 2 * t + k], 1), :], sem).start()
        return carry

    lax.fori_loop(0, tb, start, 0)
    for _ in range(2):
        pltpu.make_async_copy(h_ref, xs_hbm.at[pl.ds(0, tb), :], sem).wait()


def _dispatch(h2p, dest, n_rows):
    T, W = h2p.shape
    tb = ROW_TILE
    return pl.pallas_call(
        _dispatch_body,
        grid=(T // tb,),
        in_specs=[pl.BlockSpec((None, 1, 2 * tb), lambda i: (i, 0, 0), memory_space=pltpu.SMEM),
                  pl.BlockSpec((tb, W), lambda i: (i, 0)),
                  pl.BlockSpec(memory_space=pl.ANY)],
        out_specs=pl.BlockSpec(memory_space=pl.ANY),
        out_shape=jax.ShapeDtypeStruct((n_rows, W), jnp.int32),
        scratch_shapes=[pltpu.SemaphoreType.DMA(())],
        input_output_aliases={2: 0},
        compiler_params=_cparams("arbitrary"),
        name="moe_dispatch",
    )(dest.reshape(T // tb, 1, 2 * tb), h2p, jnp.zeros((n_rows, W), jnp.int32))


MOE_ALIGN = 128
MOE_TM = 1024
MOE_SUB = 256
MOE_TF = 256


def _moe_body(ie_ref, ir_ref, in_ref, xs_hbm, wg_ref, wu_ref, wd_ref, y0_hbm, y_hbm,
              xbuf, xlo, xhi, acc, sem_in, sem_out):
    del ie_ref, y0_hbm
    w = pl.program_id(0)
    f = pl.program_id(1)
    nsub = in_ref[w]
    row0 = pl.multiple_of(ir_ref[w] * MOE_ALIGN, MOE_ALIGN)
    n_mm = (nsub * MOE_ALIGN + MOE_SUB - 1) // MOE_SUB
    half = xlo.shape[1]

    @pl.when((f == 0) & (nsub > 0))
    def _():
        cp = pltpu.make_async_copy(xs_hbm.at[pl.ds(row0, MOE_TM), :], xbuf, sem_in)
        cp.start()
        acc[...] = jnp.zeros_like(acc)
        cp.wait()

        def unpack(j, carry):
            rows = pl.ds(pl.multiple_of(j * MOE_SUB, MOE_SUB), MOE_SUB)
            bits = xbuf[rows, :]
            xlo[rows, :] = pltpu.bitcast(lax.shift_left(bits, 16), F32).astype(BF16)
            xhi[rows, :] = pltpu.bitcast(bits & jnp.int32(-65536), F32).astype(BF16)
            return carry

        lax.fori_loop(0, n_mm, unpack, 0)

    @pl.when(nsub > 0)
    def _():
        wg = wg_ref[...].astype(BF16)
        wu = wu_ref[...].astype(BF16)
        wd = wd_ref[...].astype(BF16)

        def sub(j, carry):
            rows = pl.ds(pl.multiple_of(j * MOE_SUB, MOE_SUB), MOE_SUB)
            lo = xlo[rows, :]
            hi = xhi[rows, :]
            hg = _dot(lo, wg[0:half, :]) + _dot(hi, wg[half:2 * half, :])
            hu = _dot(lo, wu[0:half, :]) + _dot(hi, wu[half:2 * half, :])
            acc[rows, :] += _dot((jax.nn.silu(hg) * hu).astype(BF16), wd)
            return carry

        lax.fori_loop(0, n_mm, sub, 0)

    @pl.when((f == pl.num_programs(1) - 1) & (nsub > 0))
    def _():
        def out_copy(j):
            r = pl.multiple_of(j * MOE_ALIGN, MOE_ALIGN)
            return pltpu.make_async_copy(acc.at[pl.ds(r, MOE_ALIGN), :],
                                         y_hbm.at[pl.ds(row0 + r, MOE_ALIGN), :], sem_out)

        def start(j, carry):
            out_copy(j).start()
            return carry

        def wait(j, carry):
            out_copy(j).wait()
            return carry

        lax.fori_loop(0, nsub, start, 0)
        lax.fori_loop(0, nsub, wait, 0)


def _moe(xs, item_e, item_r, item_n, w_gate, w_up, w_down):
    rows = xs.shape[0]
    D = 2 * xs.shape[1]
    n_items = item_e.shape[0]
    nf = EXPERT_DFF // MOE_TF

    def fcol(f, nn, w):
        return jnp.where(nn[w] > 0, f, nf - 1)

    return pl.pallas_call(
        _moe_body,
        grid_spec=pltpu.PrefetchScalarGridSpec(
            num_scalar_prefetch=3,
            grid=(n_items, nf),
            in_specs=[
                pl.BlockSpec(memory_space=pl.ANY),
                pl.BlockSpec((None, D, MOE_TF), lambda w, f, ie, ir, nn: (ie[w], 0, fcol(f, nn, w))),
                pl.BlockSpec((None, D, MOE_TF), lambda w, f, ie, ir, nn: (ie[w], 0, fcol(f, nn, w))),
                pl.BlockSpec((None, MOE_TF, D), lambda w, f, ie, ir, nn: (ie[w], fcol(f, nn, w), 0)),
                pl.BlockSpec(memory_space=pl.ANY),
            ],
            out_specs=pl.BlockSpec(memory_space=pl.ANY),
            scratch_shapes=[pltpu.VMEM((MOE_TM, D // 2), jnp.int32),
                            pltpu.VMEM((MOE_TM, D // 2), BF16), pltpu.VMEM((MOE_TM, D // 2), BF16),
                            pltpu.VMEM((MOE_TM, D), F32),
                            pltpu.SemaphoreType.DMA(()), pltpu.SemaphoreType.DMA(())],
        ),
        out_shape=jax.ShapeDtypeStruct((rows, D), F32),
        input_output_aliases={7: 0},
        compiler_params=_cparams("arbitrary", "arbitrary"),
        name="moe_experts",
    )(item_e, item_r, item_n, xs, w_gate, w_up, w_down, jnp.zeros((rows, D), F32))


def _final_body(dcur_ref, dnext_ref, x2_ref, rt_ref, g_ref, ys_hbm, o_ref, ybuf, sems):
    i = pl.program_id(0)
    tb = x2_ref.shape[0]
    slot = i % 2

    def gather(d_ref, s):
        def body(t, carry):
            for k in range(2):
                pltpu.make_async_copy(ys_hbm.at[pl.ds(d_ref[0, 2 * t + k], 1), :],
                                      ybuf.at[s, k, pl.ds(t, 1), :], sems.at[s]).start()
            return carry

        lax.fori_loop(0, tb, body, 0)

    @pl.when(i == 0)
    def _():
        gather(dcur_ref, 0)

    @pl.when(i + 1 < pl.num_programs(0))
    def _():
        gather(dnext_ref, 1 - slot)

    for k in range(2):
        pltpu.make_async_copy(ys_hbm.at[pl.ds(0, tb), :], ybuf.at[slot, k], sems.at[slot]).wait()
    rt = rt_ref[...]
    x = (x2_ref[...] + rt[:, ROUTE_W1:ROUTE_W1 + 1] * ybuf[slot, 0]
         + rt[:, ROUTE_W2:ROUTE_W2 + 1] * ybuf[slot, 1])
    o_ref[...] = x * lax.rsqrt(jnp.mean(x * x, axis=-1, keepdims=True) + RMS_EPS) * g_ref[...]


def _final(x2, ys, dest, rt, g):
    T, D = x2.shape
    tb = ROW_TILE
    nb = T // tb
    dest3 = dest.reshape(nb, 1, 2 * tb)
    return pl.pallas_call(
        _final_body,
        grid=(nb,),
        in_specs=[pl.BlockSpec((None, 1, 2 * tb), lambda i: (i, 0, 0), memory_space=pltpu.SMEM),
                  pl.BlockSpec((None, 1, 2 * tb), lambda i: (jnp.minimum(i + 1, nb - 1), 0, 0),
                               memory_space=pltpu.SMEM),
                  pl.BlockSpec((tb, D), lambda i: (i, 0)),
                  pl.BlockSpec((tb, LANES), lambda i: (i, 0)),
                  pl.BlockSpec((1, D), lambda i: (0, 0)),
                  pl.BlockSpec(memory_space=pl.ANY)],
        out_specs=pl.BlockSpec((tb, D), lambda i: (i, 0)),
        out_shape=jax.ShapeDtypeStruct((T, D), F32),
        scratch_shapes=[pltpu.VMEM((2, 2, tb, D), F32), pltpu.SemaphoreType.DMA((2,))],
        compiler_params=_cparams("arbitrary"),
        name="combine_final_norm",
    )(dest3, dest3, x2, rt, g, ys)


def _moe_layout(e_flat):
    A = e_flat.shape[0]
    onehot = (e_flat[:, None] == jnp.arange(N_EXPERTS, dtype=jnp.int32)[None, :]).astype(jnp.int32)
    csum = jnp.cumsum(onehot, axis=0)
    counts = csum[-1]
    nblk = (counts + MOE_ALIGN - 1) // MOE_ALIGN
    blk0 = jnp.cumsum(nblk) - nblk
    dest = jnp.sum(onehot * (blk0[None, :] * MOE_ALIGN + csum - 1), axis=1).astype(jnp.int32)
    per_item = MOE_TM // MOE_ALIGN
    nitem = (nblk + per_item - 1) // per_item
    iend = jnp.cumsum(nitem)
    n_items = (A // MOE_ALIGN + N_EXPERTS + N_EXPERTS * (per_item - 1)) // per_item
    w = jnp.arange(n_items, dtype=jnp.int32)
    total = iend[-1]
    wv = jnp.minimum(w, total - 1)
    item_e = jnp.minimum(jnp.searchsorted(iend, wv, side='right'), N_EXPERTS - 1).astype(jnp.int32)
    sb = wv - (iend - nitem)[item_e]
    item_r = (blk0[item_e] + sb * per_item).astype(jnp.int32)
    item_n = jnp.where(w < total, jnp.minimum(per_item, nblk[item_e] - sb * per_item), 0).astype(jnp.int32)
    return dest, item_e, item_r, item_n


def kernel(x, norm1_g, w_in, hg_lb_logits, hg_norm_g, cmp_pe_k, cmp_w1_k, cmp_w2_k, cmp_pe_v,
           cmp_w1_v, cmp_w2_v, rel_bias, w_out, norm2_g, w_router_group, b_router_group,
           w_router_expert, b_router_expert, w_expert_gate, w_expert_up, w_expert_down,
           final_norm_g):
    B, S, D = x.shape
    T = B * S
    assert w_in.shape[0] == 1, "single-layer block"
    x2d = x.reshape(T, D)

    lower = jax.nn.softmax(hg_lb_logits.astype(F32), axis=0)[0].reshape(HG_HEADS, 1, HG_DIM)

    w_main = w_in[0, :, :MAIN_COLS].astype(BF16)
    w_gate_cols = jnp.pad(w_in[0, :, MAIN_COLS:].astype(BF16), ((0, 0), (0, LANES - GATE_COLS)))
    proj, glog = _inproj(x2d, norm1_g[0].reshape(1, D), w_main, w_gate_cols, tm=1024, tn=512)

    y_hg = _hgrn(proj, lower, hg_norm_g[0].reshape(1, HG_DIM), B, S, tb=512)
    kcvc = _nsa_compress_all(proj, cmp_pe_k[0], cmp_w1_k[0], cmp_w2_k[0],
                             cmp_pe_v[0], cmp_w1_v[0], cmp_w2_v[0], B, S)
    y_nsa = _nsa(proj, glog, kcvc[0], kcvc[1], rel_bias, B, S)

    w_r = jnp.concatenate([w_router_group[0], w_router_expert[0]], axis=1)
    w_r = jnp.pad(w_r, ((0, 0), (0, LANES - w_r.shape[1])))
    wr_hi = w_r.astype(BF16)
    wr_lo = (w_r - wr_hi.astype(F32)).astype(BF16)
    b_r = jnp.concatenate([b_router_group[0], b_router_expert[0]])
    b_r = jnp.pad(b_r, (0, LANES - b_r.shape[0])).reshape(1, LANES)
    x2, h2, rt = _outproj(x2d, y_hg, y_nsa, w_out[0].astype(BF16), norm2_g[0].reshape(1, D),
                          wr_hi, wr_lo, b_r, tm=512)

    e_flat = rt[:, ROUTE_E1:ROUTE_E2 + 1].astype(jnp.int32).reshape(2 * T)
    dest, item_e, item_r, item_n = _moe_layout(e_flat)
    n_rows = (2 * T // MOE_ALIGN + N_EXPERTS) * MOE_ALIGN + MOE_TM
    xs = _dispatch(h2, dest, n_rows)
    ys = _moe(xs, item_e, item_r, item_n, w_expert_gate[0], w_expert_up[0], w_expert_down[0])

    out = _final(x2, ys, dest, rt, final_norm_g.reshape(1, D))
    return out.reshape(B, S, D)
```

```python
import functools
import math

import jax
import jax.numpy as jnp
import numpy as np
from jax import lax
from jax.experimental import pallas as pl
from jax.experimental.pallas import tpu as pltpu

F32 = jnp.float32
BF16 = jnp.bfloat16

D_MODEL = 2048
HG_HEADS = 8
HG_DIM = 128
HG_WIDTH = HG_HEADS * HG_DIM
HG_CHUNK = 64
NSA_HEADS = 16
NSA_GROUPS = 4
NSA_HPG = NSA_HEADS // NSA_GROUPS
NSA_DH = 64
NSA_WIDTH = NSA_HEADS * NSA_DH
NSA_KV = NSA_GROUPS * NSA_DH
CMP_LEN = 32
CMP_STRIDE = 16
CMP_HIDDEN = 256
SLC_LEN = 64
SLC_TOPK = 16
WIN = 512
REL_BUCKETS = 32
REL_MAX_DIST = 128
N_EXPERT_GROUPS = 8
EXPERTS_PER_GROUP = 8
N_EXPERTS = N_EXPERT_GROUPS * EXPERTS_PER_GROUP
EXPERT_DFF = 1024
RMS_EPS = 1e-6
NEG_INF = -1e30
BIG = 1e9

MAIN_COLS = 4 * HG_WIDTH + NSA_WIDTH + 6 * NSA_KV
GATE_COLS = 3 * NSA_HEADS
LANES = 128
VMEM_LIMIT = 56 * 1024 * 1024


def _cparams(*sem):
    return pltpu.CompilerParams(dimension_semantics=sem, vmem_limit_bytes=VMEM_LIMIT)


def _nt(a, b):
    return lax.dot_general(a, b, (((1,), (1,)), ((), ())), preferred_element_type=F32)


def _tn(a, b):
    return lax.dot_general(a, b, (((0,), (0,)), ((), ())), preferred_element_type=F32)


def _dot(a, b):
    return jnp.dot(a, b, preferred_element_type=F32)


def _inproj_body(x_ref, g_ref, w_ref, wg_ref, oh_ref, on_ref, og_ref, h_scr, *, hg_tiles):
    j = pl.program_id(1)

    @pl.when(j == 0)
    def _():
        x = x_ref[...]
        y = x * lax.rsqrt(jnp.mean(x * x, axis=-1, keepdims=True) + RMS_EPS) * g_ref[...]
        h_scr[...] = y.astype(BF16)
        og_ref[...] = _dot(h_scr[...], wg_ref[...])

    @pl.when(j < hg_tiles)
    def _():
        oh_ref[...] = _dot(h_scr[...], w_ref[...])

    @pl.when(j >= hg_tiles)
    def _():
        on_ref[...] = _dot(h_scr[...], w_ref[...]).astype(on_ref.dtype)


def _inproj(x2d, g, w_main, w_gate, tm, tn):
    T, D = x2d.shape
    N = w_main.shape[1]
    n_hg = 4 * HG_WIDTH
    hg_tiles = n_hg // tn
    return pl.pallas_call(
        functools.partial(_inproj_body, hg_tiles=hg_tiles),
        grid=(T // tm, N // tn),
        in_specs=[
            pl.BlockSpec((tm, D), lambda i, j: (i, 0)),
            pl.BlockSpec((1, D), lambda i, j: (0, 0)),
            pl.BlockSpec((D, tn), lambda i, j: (0, j)),
            pl.BlockSpec((D, LANES), lambda i, j: (0, 0)),
        ],
        out_specs=[
            pl.BlockSpec((tm, tn), lambda i, j: (i, jnp.minimum(j, hg_tiles - 1))),
            pl.BlockSpec((tm, tn), lambda i, j: (i, jnp.maximum(j - hg_tiles, 0))),
            pl.BlockSpec((tm, LANES), lambda i, j: (i, 0)),
        ],
        out_shape=[jax.ShapeDtypeStruct((T, n_hg), F32),
                   jax.ShapeDtypeStruct((T, N - n_hg), BF16),
                   jax.ShapeDtypeStruct((T, LANES), F32)],
        scratch_shapes=[pltpu.VMEM((tm, D), BF16)],
        compiler_params=_cparams("arbitrary", "arbitrary"),
        name="inproj",
    )(x2d, g, w_main, w_gate)


def _hgrn_tables(C):
    t = np.arange(C)
    levels = []
    m = C // 2
    while m >= 1:
        levels.append(m)
        m //= 2
    rows = [np.tril(np.ones((C, C), np.float32)),
            np.triu(np.ones((C, C), np.float32), 1)]
    lev = np.full((C, C), -1, np.int32)
    for li, m in enumerate(levels):
        r = t % (2 * m)
        mid = t - r + m - 1
        M = np.zeros((C, C), np.float32)
        for i in range(C):
            if r[i] >= m:
                M[i, mid[i] + 1:i + 1] = 1.0
            else:
                M[i, i + 1:mid[i] + 1] = 1.0
        rows.append(M)
        same = (t[:, None] // (2 * m)) == (t[None, :] // (2 * m))
        lev[same & (r[:, None] >= m) & (r[None, :] < m)] = li
    lev[t[:, None] == t[None, :]] = len(levels)
    return np.concatenate(rows, 0), lev, len(levels)


HG_HEADS_PER_STEP = 8


def _hgrn_body(q_ref, f_ref, i_ref, g_ref, lb_ref, ng_ref, mst_ref, lev_ref, o_ref, st_ref,
               *, C, n_lev):
    @pl.when(pl.program_id(2) == 0)
    def _():
        st_ref[...] = jnp.zeros_like(st_ref)

    ng = ng_ref[...]
    mst = mst_ref[...]
    lev = lev_ref[...]
    n_chunks = q_ref.shape[0] // C

    def gates(rows, hb):
        cols = slice(hb * HG_DIM, (hb + 1) * HG_DIM)
        lb = lb_ref[hb]
        q = jax.nn.silu(q_ref[rows, cols])
        f = lb + (1.0 - lb) * jax.nn.sigmoid(f_ref[rows, cols])
        lf = jnp.log(f)
        lf0 = lf.astype(BF16)
        r1 = lf - lf0.astype(F32)
        lf1 = r1.astype(BF16)
        lf2 = (r1 - lf1.astype(F32)).astype(BF16)
        e = _dot(mst, jnp.concatenate([lf0, lf1, lf2], axis=0))
        return q, 1.0 - f, i_ref[rows, cols].astype(BF16), e

    def scores(q, k, gdec):
        a = jnp.where(lev == n_lev, _nt(q.astype(BF16), k.astype(BF16)), 0.0)
        for li in range(n_lev):
            g_m = gdec[(2 + li) * C:(3 + li) * C]
            a_m = _nt((q * g_m).astype(BF16), (k * g_m).astype(BF16))
            a = jnp.where(lev == li, a_m, a)
        return a.astype(BF16)

    def finish(rows, hb, q, k, v, gdec, a):
        cols = slice(hb * HG_DIM, (hb + 1) * HG_DIM)
        g_b = gdec[0:C]
        g_r = gdec[C:2 * C]
        st = st_ref[hb]
        o = _nt((q * g_b).astype(BF16), st.astype(BF16)) + _dot(a, v)
        g_last = g_b[C - 1:C, :]
        st_ref[hb] = st * g_last + _tn(v, (k * g_r).astype(BF16))
        o = o * lax.rsqrt(jnp.mean(o * o, axis=-1, keepdims=True) + RMS_EPS) * ng
        o_ref[rows, cols] = (o * jax.nn.silu(g_ref[rows, cols])).astype(o_ref.dtype)

    def chunk(c, carry):
        rows = pl.ds(pl.multiple_of(c * C, C), C)
        hs = range(HG_HEADS_PER_STEP)
        qkve = [gates(rows, hb) for hb in hs]
        gdec = [jnp.exp(x[3]) for x in qkve]
        a = [scores(qkve[hb][0], qkve[hb][1], gdec[hb]) for hb in hs]
        for hb in hs:
            finish(rows, hb, qkve[hb][0], qkve[hb][1], qkve[hb][2], gdec[hb], a[hb])
        return carry

    lax.fori_loop(0, n_chunks, chunk, 0)


def _hgrn(proj, lb, ng, B, S, tb):
    C = HG_CHUNK
    mst, lev, n_lev = _hgrn_tables(C)
    mst = np.concatenate([mst] * 3, axis=1)
    nb = S // tb
    HB = HG_HEADS_PER_STEP
    hsteps = HG_HEADS // HB

    def col(group):
        return pl.BlockSpec((tb, HB * HG_DIM), lambda b, h, t: (b * nb + t, group * hsteps + h))

    return pl.pallas_call(
        functools.partial(_hgrn_body, C=C, n_lev=n_lev),
        grid=(B, hsteps, nb),
        in_specs=[
            col(0), col(1), col(2), col(3),
            pl.BlockSpec((HB, 1, HG_DIM), lambda b, h, t: (h, 0, 0)),
            pl.BlockSpec((1, HG_DIM), lambda b, h, t: (0, 0)),
            pl.BlockSpec(mst.shape, lambda b, h, t: (0, 0)),
            pl.BlockSpec(lev.shape, lambda b, h, t: (0, 0)),
        ],
        out_specs=pl.BlockSpec((tb, HB * HG_DIM), lambda b, h, t: (b * nb + t, h)),
        out_shape=jax.ShapeDtypeStruct((B * S, HG_WIDTH), BF16),
        scratch_shapes=[pltpu.VMEM((HB, HG_DIM, HG_DIM), F32)],
        compiler_params=_cparams("arbitrary", "arbitrary", "arbitrary"),
        name="hgrn2",
    )(proj, proj, proj, proj, lb, ng, jnp.asarray(mst, BF16), jnp.asarray(lev))


def _compress_body(h_ref, pe_ref, w1_ref, w2_ref, o_ref):
    hv = h_ref[...]
    half = hv.shape[1]
    nh = hv.shape[0]
    u = _dot((hv + pe_ref[0:1, :]).astype(BF16), w1_ref[0:half, :].astype(BF16))
    v = _dot((hv + pe_ref[1:2, :]).astype(BF16), w1_ref[half:2 * half, :].astype(BF16))
    pre = u + pltpu.roll(v, nh - 1, axis=0)
    o_ref[...] = _dot(jax.nn.silu(pre).astype(BF16), w2_ref[...].astype(BF16)).astype(o_ref.dtype)


def _compress(halves, pe, w1, w2):
    _, B, G, NH, HW = halves.shape
    return pl.pallas_call(
        _compress_body,
        grid=(2, B, G),
        in_specs=[
            pl.BlockSpec((None, None, None, NH, HW), lambda s, b, g: (s, b, g, 0, 0)),
            pl.BlockSpec((None, 2, HW), lambda s, b, g: (s, 0, 0)),
            pl.BlockSpec((None, 2 * HW, CMP_HIDDEN), lambda s, b, g: (s, 0, 0)),
            pl.BlockSpec((None, CMP_HIDDEN, NSA_DH), lambda s, b, g: (s, 0, 0)),
        ],
        out_specs=pl.BlockSpec((None, None, None, NH, NSA_DH), lambda s, b, g: (s, b, g, 0, 0)),
        out_shape=jax.ShapeDtypeStruct((2, B, G, NH, NSA_DH), BF16),
        compiler_params=_cparams("arbitrary", "arbitrary", "arbitrary"),
        name="nsa_compress",
    )(halves, pe, w1, w2)


def _nsa_compress_all(proj, pe_k, w1_k, w2_k, pe_v, w1_v, w2_v, B, S):
    NH = S // CMP_STRIDE
    c0 = NSA_WIDTH

    def halves(idx):
        t = proj[:, c0 + idx * NSA_KV:c0 + (idx + 1) * NSA_KV]
        t = t.reshape(B, NH, CMP_STRIDE, NSA_GROUPS, NSA_DH).transpose(0, 3, 1, 2, 4)
        return t.reshape(B, NSA_GROUPS, NH, CMP_STRIDE * NSA_DH)

    hw = CMP_STRIDE * NSA_DH
    return _compress(jnp.stack([halves(0), halves(1)]),
                     jnp.stack([pe_k.reshape(2, hw), pe_v.reshape(2, hw)]),
                     jnp.stack([w1_k, w1_v]), jnp.stack([w2_k, w2_v]))


NSA_TQ = 128


NSA_MASK_ROWS = 128
NSA_WIN_MASK_ROWS = 16
NSA_FAR = 4
NSA_SLC_PAD = (NSA_FAR - 1) * NSA_TQ


def _softmax_cols(s):
    m = jnp.max(s, axis=0, keepdims=True)
    p = jnp.exp(s - m)
    return m, p, jnp.sum(p, axis=0, keepdims=True)


NSA_GROUPS_PER_STEP = 2


def _nsa_body(qt_ref, gl_ref, kc_ref, vct_ref, ks_ref, vst_ref, kw_ref, vwt_ref,
              bc_ref, bw_ref, bn_ref, ovt_ref, o_ref, *, top_k, n_slc):
    TQ = NSA_TQ
    GG = NSA_GROUPS_PER_STEP
    i = pl.program_id(2)
    ovt = ovt_ref[...]
    jb = lax.broadcasted_iota(jnp.int32, (n_slc, TQ), 0)
    qblk = (i * TQ + lax.broadcasted_iota(jnp.int32, (n_slc, TQ), 1)) // SLC_LEN
    forced = (jb == 0) | (jb == qblk) | (jb == qblk - 1)
    future = jb > qblk
    near = pl.ds(pl.multiple_of(NSA_SLC_PAD + (i - 1) * TQ, TQ), 2 * TQ)

    def heads(a):
        return jnp.concatenate([a] * NSA_HPG, axis=1)

    def queries(g):
        qt = qt_ref[g]
        return jnp.concatenate([qt[h * NSA_DH:(h + 1) * NSA_DH, :] for h in range(NSA_HPG)],
                               axis=1)

    def compressed(g, s):
        m = jnp.max(s, axis=0, keepdims=True)
        p = jnp.where(s > 0.5 * NEG_INF, jnp.exp(s - m), 0.0)
        l = jnp.sum(p, axis=0, keepdims=True)
        pc = p * (1.0 / jnp.maximum(l, 1e-30))
        o_cmp = _dot(vct_ref[g], pc.astype(BF16))
        ps = pc[:, 0:TQ] + pc[:, TQ:2 * TQ] + pc[:, 2 * TQ:3 * TQ] + pc[:, 3 * TQ:4 * TQ]
        ps_hi = ps.astype(BF16)
        ps_lo = (ps - ps_hi.astype(F32)).astype(BF16)
        imp = _dot(ovt, ps_hi) + _dot(ovt, ps_lo)
        return o_cmp, jnp.where(forced, BIG, jnp.where(future, -BIG, imp))

    def window(g, s):
        m, p, l = _softmax_cols(s)
        return _dot(vwt_ref[g, :, win], p.astype(BF16)) * (1.0 / l)

    def select(imp, q4):
        slabs = []
        for v in range(n_slc // 8):
            slab = imp[8 * v:8 * v + 8, :]
            jbs = jb[8 * v:8 * v + 8, :]
            cnt = jnp.zeros((8, TQ), F32)
            for jp in range(n_slc):
                row = imp[jp:jp + 1, :]
                if jp < 8 * v:
                    beats = row >= slab
                elif jp >= 8 * v + 8:
                    beats = row > slab
                else:
                    beats = (row > slab) | ((row == slab) & (jbs > jp))
                cnt = cnt + jnp.where(beats, 1.0, 0.0)
            slabs.append(jnp.where(cnt < top_k, 0.0, 1.0))
        unsel = jnp.concatenate(slabs + [jnp.ones((NSA_MASK_ROWS - n_slc, TQ), F32)], axis=0)
        return jnp.concatenate([heads(unsel.astype(BF16)), q4], axis=0)

    def near_tiles(g, s):
        m, p, l = _softmax_cols(s)
        return m, l, _dot(vst_ref[g, :, near], p.astype(BF16))

    win = pl.ds(pl.multiple_of(i * TQ, TQ), WIN + TQ)
    pad_row = jnp.where(lax.broadcasted_iota(jnp.int32, (NSA_WIN_MASK_ROWS, NSA_HPG * TQ), 0) == 0,
                        1.0, 0.0).astype(BF16)
    gs = range(GG)
    q4 = [queries(g) for g in gs]
    s_cmp = [_dot(kc_ref[g], q4[g]) + bc_ref[g] for g in gs]
    s_win = [_dot(kw_ref[g, win, :], jnp.concatenate([pad_row, q4[g]], axis=0)) + bw_ref[g] for g in gs]
    cmp_out = [compressed(g, s_cmp[g]) for g in gs]
    o_win = [window(g, s_win[g]) for g in gs]
    qa = [select(cmp_out[g][1], q4[g]) for g in gs]
    s_near = [_dot(ks_ref[g, near, :], qa[g]) + bn_ref[g] for g in gs]
    start = tuple(near_tiles(g, s_near[g]) for g in gs)

    n_far = i - 1
    n_chunks = jnp.maximum((n_far + NSA_FAR - 1) // NSA_FAR, 0)
    far0 = NSA_SLC_PAD + TQ * (n_far - NSA_FAR * n_chunks)

    def far_step(c, carry):
        keys = pl.ds(pl.multiple_of(far0 + c * NSA_FAR * TQ, TQ), NSA_FAR * TQ)
        out = []
        logits = [_dot(ks_ref[g, keys, :], qa[g]) for g in gs]
        for g in gs:
            m, l, acc = carry[g]
            s = logits[g]
            m_new = jnp.maximum(m, jnp.max(s, axis=0, keepdims=True))
            alpha = jnp.exp(m - m_new)
            p = jnp.exp(s - m_new)
            l = alpha * l + jnp.sum(p, axis=0, keepdims=True)
            acc = alpha * acc + _dot(vst_ref[g, :, keys], p.astype(BF16))
            out.append((m_new, l, acc))
        return tuple(out)

    far = lax.fori_loop(0, n_chunks, far_step, start)

    for g in gs:
        _, l, acc = far[g]
        o_slc = acc * (1.0 / l)
        o_cmp = cmp_out[g][0]
        sg = jax.nn.sigmoid(gl_ref[g])
        outs = []
        for h in range(NSA_HPG):
            lanes = slice(h * TQ, (h + 1) * TQ)
            outs.append(sg[h:h + 1, :] * o_cmp[:, lanes]
                        + sg[NSA_HPG + h:NSA_HPG + h + 1, :] * o_slc[:, lanes]
                        + sg[2 * NSA_HPG + h:2 * NSA_HPG + h + 1, :] * o_win[g][:, lanes])
        width = NSA_HPG * NSA_DH
        o_ref[:, g * width:(g + 1) * width] = jnp.concatenate(outs, axis=0).T.astype(o_ref.dtype)


def _rel_bucket(dist):
    n = jnp.maximum(dist, 0)
    max_exact = REL_BUCKETS // 2
    nf = jnp.maximum(n, 1).astype(F32)
    large = max_exact + (jnp.log(nf / max_exact) / math.log(REL_MAX_DIST / max_exact)
                         * (REL_BUCKETS - max_exact)).astype(jnp.int32)
    large = jnp.minimum(large, REL_BUCKETS - 1)
    return jnp.where(n < max_exact, n, large)


def _bias_table(table, dist, valid):
    onehot = (_rel_bucket(jnp.asarray(dist, jnp.int32))[..., None]
              == jnp.arange(REL_BUCKETS, dtype=jnp.int32)).astype(F32)
    b = jnp.einsum('rmk,ghk->ghrm', onehot, table, precision=lax.Precision.HIGHEST)
    return jnp.where(jnp.asarray(valid), b, NEG_INF)


def _nsa(proj, glog, kc, vc, rel_bias, B, S):
    TQ = NSA_TQ
    G, HPG, DH = NSA_GROUPS, NSA_HPG, NSA_DH
    NH = S // CMP_STRIDE
    n_cmp = (S - CMP_LEN) // CMP_STRIDE + 1
    n_slc = S // SLC_LEN
    top_k = min(SLC_TOPK, n_slc)
    nt = S // TQ
    c0 = NSA_WIDTH

    MR = NSA_MASK_ROWS
    assert n_slc < MR and n_slc % 8 == 0

    def kv(idx, pad):
        t = proj[:, c0 + idx * NSA_KV:c0 + (idx + 1) * NSA_KV].reshape(B, S, G, DH)
        t = jnp.pad(t, ((0, 0), (pad, 0), (0, 0), (0, 0)))
        return t.transpose(0, 2, 1, 3)

    def keys_with_masks(idx, pad, with_blocks):
        pos = np.arange(pad + S) - pad
        if with_blocks:
            cols = np.zeros((pad + S, MR), np.float32)
            cols[:, :n_slc] = ((pos // SLC_LEN)[:, None] == np.arange(n_slc)[None, :]) & (pos >= 0)[:, None]
            cols[:, n_slc] = pos < 0
        else:
            cols = np.zeros((pad + S, NSA_WIN_MASK_ROWS), np.float32)
            cols[:, 0] = pos < 0
        m = jnp.broadcast_to(jnp.asarray(cols * NEG_INF, BF16), (B, G) + cols.shape)
        return jnp.concatenate([m, kv(idx, pad)], axis=3)

    SP = NSA_SLC_PAD
    ks, vst = keys_with_masks(2, SP, True), kv(3, SP).transpose(0, 1, 3, 2)
    kw, vwt = keys_with_masks(4, WIN, False), kv(5, WIN).transpose(0, 1, 3, 2)
    kcb = kc
    vct = vc.transpose(0, 1, 3, 2)
    qt = proj[:, :c0] * (DH ** -0.5)
    qt = qt.reshape(B, S, G, HPG * DH).transpose(0, 2, 3, 1)

    table = rel_bias.T.reshape(G, HPG, REL_BUCKETS)
    far = table[:, :, REL_BUCKETS - 1]
    r = np.arange(TQ)[:, None]

    def keys_major(b):
        return b.transpose(0, 3, 1, 2).reshape(G, b.shape[3], HPG * TQ)

    per_tile = TQ // CMP_STRIDE
    d_c = r - (CMP_LEN - 1) - CMP_STRIDE * (np.arange(2 * NH)[None, :] - NH)
    rel_cmp = keys_major(_bias_table(table, d_c, d_c >= 0))
    bias_cmp = jnp.stack([rel_cmp[:, NH - per_tile * i:2 * NH - per_tile * i] for i in range(nt)],
                         axis=1)
    d_w = r + WIN - np.arange(WIN + TQ)[None, :]
    bias_win = keys_major(_bias_table(table, d_w, (d_w >= 0) & (d_w < WIN)))
    d_n = r + TQ - np.arange(2 * TQ)[None, :]
    bias_near = keys_major(_bias_table(table - far[:, :, None], d_n, d_n >= 0))

    ci = np.arange(NH)[None, :] * CMP_STRIDE
    sj = np.arange(n_slc)[:, None] * SLC_LEN
    overlap_t = ((ci <= sj + SLC_LEN - 1) & (ci + CMP_LEN - 1 >= sj) & (np.arange(NH)[None, :] < n_cmp))

    gate_rows = 16
    glog_g = glog[:, :GATE_COLS].reshape(B * S, 3, G, HPG).transpose(2, 1, 3, 0).reshape(G, 3 * HPG, B * S)
    glog_g = jnp.pad(glog_g, ((0, 0), (0, gate_rows - 3 * HPG), (0, 0)))

    GG = NSA_GROUPS_PER_STEP
    whole = lambda shape: pl.BlockSpec((None, GG) + shape, lambda b, g, i: (b, g, 0, 0))
    tile = lambda k: pl.BlockSpec((GG, k, HPG * TQ), lambda b, g, i: (g, 0, 0))
    return pl.pallas_call(
        functools.partial(_nsa_body, top_k=top_k, n_slc=n_slc),
        grid=(B, G // GG, nt),
        in_specs=[
            pl.BlockSpec((None, GG, HPG * DH, TQ), lambda b, g, i: (b, g, 0, i)),
            pl.BlockSpec((GG, gate_rows, TQ), lambda b, g, i: (g, 0, b * nt + i)),
            whole((NH, DH)), whole((DH, NH)),
            whole((S + SP, MR + DH)), whole((DH, S + SP)),
            whole((S + WIN, NSA_WIN_MASK_ROWS + DH)), whole((DH, S + WIN)),
            pl.BlockSpec((GG, None, NH, HPG * TQ), lambda b, g, i: (g, i, 0, 0)),
            tile(WIN + TQ), tile(2 * TQ),
            pl.BlockSpec((n_slc, NH), lambda b, g, i: (0, 0)),
        ],
        out_specs=pl.BlockSpec((TQ, GG * HPG * DH), lambda b, g, i: (b * nt + i, g)),
        out_shape=jax.ShapeDtypeStruct((B * S, NSA_WIDTH), BF16),
        compiler_params=_cparams("arbitrary", "arbitrary", "arbitrary"),
        name="nsa_attention",
    )(qt, glog_g, kcb, vct, ks, vst, kw, vwt, bias_cmp, bias_win, bias_near,
      jnp.asarray(overlap_t, BF16))


ROUTE_E1, ROUTE_E2, ROUTE_W1, ROUTE_W2 = 0, 1, 2, 3


def _outproj_body(x_ref, yh_ref, yn_ref, wo_ref, g2_ref, wrh_ref, wrl_ref, br_ref,
                  x2_ref, h2_ref, rt_ref):
    x2 = (x_ref[...] + _dot(yh_ref[...], wo_ref[0:HG_WIDTH, :])
          + _dot(yn_ref[...], wo_ref[HG_WIDTH:HG_WIDTH + NSA_WIDTH, :]))
    x2_ref[...] = x2
    h = x2 * lax.rsqrt(jnp.mean(x2 * x2, axis=-1, keepdims=True) + RMS_EPS) * g2_ref[...]
    hb = h.astype(BF16)
    hbf = hb.astype(F32)
    half = h.shape[1] // 2
    bits = pltpu.bitcast(hbf, jnp.int32)
    h2_ref[...] = (lax.shift_right_logical(bits[:, :half], 16)
                   | (bits[:, half:] & jnp.int32(-65536)))
    hl = (h - hbf).astype(BF16)
    wrh = wrh_ref[...]
    lg = _dot(hb, wrh) + _dot(hl, wrh) + _dot(hb, wrl_ref[...]) + br_ref[...]
    lane = lax.broadcasted_iota(jnp.int32, lg.shape, 1)
    first = lambda hit: jnp.min(jnp.where(hit, lane, LANES), axis=-1, keepdims=True)
    lgg = jnp.where(lane < N_EXPERT_GROUPS, lg, NEG_INF)
    mg = jnp.max(lgg, axis=-1, keepdims=True)
    p_top = 1.0 / jnp.sum(jnp.exp(lgg - mg), axis=-1, keepdims=True)
    lo = N_EXPERT_GROUPS + EXPERTS_PER_GROUP * first(lgg == mg)
    le = jnp.where((lane >= lo) & (lane < lo + EXPERTS_PER_GROUP), lg, NEG_INF)
    v1 = jnp.max(le, axis=-1, keepdims=True)
    i1 = first(le == v1)
    le = jnp.where(lane == i1, NEG_INF, le)
    v2 = jnp.max(le, axis=-1, keepdims=True)
    i2 = first(le == v2)
    e21 = jnp.exp(v2 - v1)
    w1 = p_top / (1.0 + e21)
    rt = jnp.where(lane == ROUTE_E1, (i1 - N_EXPERT_GROUPS).astype(F32), 0.0)
    rt = jnp.where(lane == ROUTE_E2, (i2 - N_EXPERT_GROUPS).astype(F32), rt)
    rt = jnp.where(lane == ROUTE_W1, w1, rt)
    rt_ref[...] = jnp.where(lane == ROUTE_W2, w1 * e21, rt)


def _outproj(x2d, y_hg, y_nsa, w_out, g2, wr_hi, wr_lo, b_r, tm):
    T, D = x2d.shape
    row = lambda n: pl.BlockSpec((tm, n), lambda i: (i, 0))
    full = lambda a: pl.BlockSpec(a.shape, lambda i: (0, 0))
    return pl.pallas_call(
        _outproj_body,
        grid=(T // tm,),
        in_specs=[row(D), row(HG_WIDTH), row(NSA_WIDTH), full(w_out), full(g2),
                  full(wr_hi), full(wr_lo), full(b_r)],
        out_specs=[row(D), row(D // 2), row(LANES)],
        out_shape=[jax.ShapeDtypeStruct((T, D), F32), jax.ShapeDtypeStruct((T, D // 2), jnp.int32),
                   jax.ShapeDtypeStruct((T, LANES), F32)],
        compiler_params=_cparams("arbitrary"),
        name="outproj_router",
    )(x2d, y_hg, y_nsa, w_out, g2, wr_hi, wr_lo, b_r)


ROW_TILE = 256


def _dispatch_body(dest_ref, h_ref, xs0_hbm, xs_hbm, sem):
    del xs0_hbm
    tb = h_ref.shape[0]

    def start(t, carry):
        for k in range(2):
            pltpu.make_async_copy(h_ref.at[pl.ds(t, 1), :],
                                  xs_hbm.at[pl.ds(dest_ref[0, 2 * t + k], 1), :], sem).start()
        return carry

    lax.fori_loop(0, tb, start, 0, unroll=8)
    for _ in range(2):
        pltpu.make_async_copy(h_ref, xs_hbm.at[pl.ds(0, tb), :], sem).wait()


def _dispatch(h2p, dest, n_rows):
    T, W = h2p.shape
    tb = ROW_TILE
    return pl.pallas_call(
        _dispatch_body,
        grid=(T // tb,),
        in_specs=[pl.BlockSpec((None, 1, 2 * tb), lambda i: (i, 0, 0), memory_space=pltpu.SMEM),
                  pl.BlockSpec((tb, W), lambda i: (i, 0)),
                  pl.BlockSpec(memory_space=pl.ANY)],
        out_specs=pl.BlockSpec(memory_space=pl.ANY),
        out_shape=jax.ShapeDtypeStruct((n_rows, W), jnp.int32),
        scratch_shapes=[pltpu.SemaphoreType.DMA(())],
        input_output_aliases={2: 0},
        compiler_params=_cparams("arbitrary"),
        name="moe_dispatch",
    )(dest.reshape(T // tb, 1, 2 * tb), h2p, jnp.zeros((n_rows, W), jnp.int32))


MOE_ALIGN = 128
MOE_TM = 1024
MOE_SUB = 256
MOE_TF = 256


def _moe_body(ie_ref, ir_ref, in_ref, nb_ref, xs_hbm, wg_ref, wu_ref, wd_ref, y_hbm,
              xbuf, xlo, xhi, acc, sem_in, sem_out):
    del ie_ref
    w = pl.program_id(0)
    f = pl.program_id(1)
    n_w = pl.num_programs(0)
    last_f = pl.num_programs(1) - 1
    nsub = in_ref[w]
    row0 = pl.multiple_of(ir_ref[w] * MOE_ALIGN, MOE_ALIGN)
    slot = w % 2
    per_mm = MOE_SUB // MOE_ALIGN
    n_full = nsub // per_mm
    half = xlo.shape[1]

    def x_copy(item, s):
        r = pl.multiple_of(ir_ref[item] * MOE_ALIGN, MOE_ALIGN)
        return pltpu.make_async_copy(xs_hbm.at[pl.ds(r, MOE_TM), :], xbuf.at[s], sem_in.at[s])

    def y_copy(r, dst_row):
        return pltpu.make_async_copy(acc.at[pl.ds(r, MOE_ALIGN), :],
                                     y_hbm.at[pl.ds(dst_row, MOE_ALIGN), :], sem_out)

    def for_row_blocks(fn):
        def body(j, carry):
            fn(pl.multiple_of(j * MOE_SUB, MOE_SUB), MOE_SUB)
            return carry

        lax.fori_loop(0, n_full, body, 0)

        @pl.when(nsub % per_mm == 1)
        def _():
            fn(pl.multiple_of(n_full * MOE_SUB, MOE_ALIGN), MOE_ALIGN)

    @pl.when((f == 0) & (nsub > 0))
    def _():
        @pl.when(w == 0)
        def _():
            x_copy(0, 0).start()

        x_copy(w, slot).wait()

        def unpack(r, size):
            bits = xbuf[slot, pl.ds(r, size), :]
            xlo[pl.ds(r, size), :] = pltpu.bitcast(lax.shift_left(bits, 16), F32).astype(BF16)
            xhi[pl.ds(r, size), :] = pltpu.bitcast(bits & jnp.int32(-65536), F32).astype(BF16)

        for_row_blocks(unpack)

    def experts(first, last):
        wg = wg_ref[...].astype(BF16)
        wu = wu_ref[...].astype(BF16)
        wd = wd_ref[...].astype(BF16)

        def block(r, size):
            rows = pl.ds(r, size)
            lo = xlo[rows, :]
            hi = xhi[rows, :]
            hg = _dot(lo, wg[0:half, :]) + _dot(hi, wg[half:2 * half, :])
            hu = _dot(lo, wu[0:half, :]) + _dot(hi, wu[half:2 * half, :])
            y = _dot((jax.nn.silu(hg) * hu).astype(BF16), wd)
            if first:
                acc[rows, :] = y
            else:
                acc[rows, :] += y
            if last:
                for k in range(size // MOE_ALIGN):
                    y_copy(r + k * MOE_ALIGN, row0 + r + k * MOE_ALIGN).start()

        for_row_blocks(block)

    @pl.when((f == 0) & (nsub > 0))
    def _():
        experts(True, False)

    @pl.when((f > 0) & (f < last_f) & (nsub > 0))
    def _():
        experts(False, False)

    @pl.when((f == last_f) & (nsub > 0))
    def _():
        nxt = jnp.minimum(w + 1, n_w - 1)

        @pl.when((w + 1 < n_w) & (in_ref[nxt] > 0))
        def _():
            x_copy(nxt, 1 - slot).start()

        experts(False, True)

        def wait(j, carry):
            y_copy(0, 0).wait()
            return carry

        lax.fori_loop(0, nsub, wait, 0)

    @pl.when((f == last_f) & (w == n_w - 1))
    def _():
        acc[0:MOE_ALIGN, :] = jnp.zeros((MOE_ALIGN, acc.shape[1]), F32)
        used = nb_ref[0]
        total = y_hbm.shape[0] // MOE_ALIGN

        def fill(j, carry):
            y_copy(0, pl.multiple_of(j * MOE_ALIGN, MOE_ALIGN)).start()
            return carry

        def wait(j, carry):
            y_copy(0, 0).wait()
            return carry

        lax.fori_loop(used, total, fill, 0)
        lax.fori_loop(used, total, wait, 0)


def _moe(xs, item_e, item_r, item_n, n_blocks, w_gate, w_up, w_down):
    rows = xs.shape[0]
    D = 2 * xs.shape[1]
    n_items = item_e.shape[0]
    nf = EXPERT_DFF // MOE_TF
    assert nf >= 2 and MOE_SUB == 2 * MOE_ALIGN

    def fcol(f, nn, w):
        return jnp.where(nn[w] > 0, f, nf - 1)

    return pl.pallas_call(
        _moe_body,
        grid_spec=pltpu.PrefetchScalarGridSpec(
            num_scalar_prefetch=4,
            grid=(n_items, nf),
            in_specs=[
                pl.BlockSpec(memory_space=pl.ANY),
                pl.BlockSpec((None, D, MOE_TF), lambda w, f, ie, ir, nn, nb: (ie[w], 0, fcol(f, nn, w))),
                pl.BlockSpec((None, D, MOE_TF), lambda w, f, ie, ir, nn, nb: (ie[w], 0, fcol(f, nn, w))),
                pl.BlockSpec((None, MOE_TF, D), lambda w, f, ie, ir, nn, nb: (ie[w], fcol(f, nn, w), 0)),
            ],
            out_specs=pl.BlockSpec(memory_space=pl.ANY),
            scratch_shapes=[pltpu.VMEM((2, MOE_TM, D // 2), jnp.int32),
                            pltpu.VMEM((MOE_TM, D // 2), BF16), pltpu.VMEM((MOE_TM, D // 2), BF16),
                            pltpu.VMEM((MOE_TM, D), F32),
                            pltpu.SemaphoreType.DMA((2,)), pltpu.SemaphoreType.DMA(())],
        ),
        out_shape=jax.ShapeDtypeStruct((rows, D), F32),
        compiler_params=_cparams("arbitrary", "arbitrary"),
        name="moe_experts",
    )(item_e, item_r, item_n, n_blocks, xs, w_gate, w_up, w_down)


def _final_body(dcur_ref, dnext_ref, x2_ref, rt_ref, g_ref, ys_hbm, o_ref, ybuf, sems):
    i = pl.program_id(0)
    tb = x2_ref.shape[0]
    slot = i % 2

    def gather(d_ref, s):
        def body(t, carry):
            for k in range(2):
                pltpu.make_async_copy(ys_hbm.at[pl.ds(d_ref[0, 2 * t + k], 1), :],
                                      ybuf.at[s, k, pl.ds(t, 1), :], sems.at[s]).start()
            return carry

        lax.fori_loop(0, tb, body, 0, unroll=8)

    @pl.when(i == 0)
    def _():
        gather(dcur_ref, 0)

    @pl.when(i + 1 < pl.num_programs(0))
    def _():
        gather(dnext_ref, 1 - slot)

    for k in range(2):
        pltpu.make_async_copy(ys_hbm.at[pl.ds(0, tb), :], ybuf.at[slot, k], sems.at[slot]).wait()
    rt = rt_ref[...]
    x = (x2_ref[...] + rt[:, ROUTE_W1:ROUTE_W1 + 1] * ybuf[slot, 0]
         + rt[:, ROUTE_W2:ROUTE_W2 + 1] * ybuf[slot, 1])
    o_ref[...] = x * lax.rsqrt(jnp.mean(x * x, axis=-1, keepdims=True) + RMS_EPS) * g_ref[...]


def _final(x2, ys, dest, rt, g):
    T, D = x2.shape
    tb = ROW_TILE
    nb = T // tb
    dest3 = dest.reshape(nb, 1, 2 * tb)
    return pl.pallas_call(
        _final_body,
        grid=(nb,),
        in_specs=[pl.BlockSpec((None, 1, 2 * tb), lambda i: (i, 0, 0), memory_space=pltpu.SMEM),
                  pl.BlockSpec((None, 1, 2 * tb), lambda i: (jnp.minimum(i + 1, nb - 1), 0, 0),
                               memory_space=pltpu.SMEM),
                  pl.BlockSpec((tb, D), lambda i: (i, 0)),
                  pl.BlockSpec((tb, LANES), lambda i: (i, 0)),
                  pl.BlockSpec((1, D), lambda i: (0, 0)),
                  pl.BlockSpec(memory_space=pl.ANY)],
        out_specs=pl.BlockSpec((tb, D), lambda i: (i, 0)),
        out_shape=jax.ShapeDtypeStruct((T, D), F32),
        scratch_shapes=[pltpu.VMEM((2, 2, tb, D), F32), pltpu.SemaphoreType.DMA((2,))],
        compiler_params=_cparams("arbitrary"),
        name="combine_final_norm",
    )(dest3, dest3, x2, rt, g, ys)


def _moe_layout(e_flat):
    A = e_flat.shape[0]
    onehot = (e_flat[:, None] == jnp.arange(N_EXPERTS, dtype=jnp.int32)[None, :]).astype(jnp.int32)
    csum = jnp.cumsum(onehot, axis=0)
    counts = csum[-1]
    nblk = (counts + MOE_ALIGN - 1) // MOE_ALIGN
    blk0 = jnp.cumsum(nblk) - nblk
    dest = jnp.sum(onehot * (blk0[None, :] * MOE_ALIGN + csum - 1), axis=1).astype(jnp.int32)
    per_item = MOE_TM // MOE_ALIGN
    nitem = (nblk + per_item - 1) // per_item
    iend = jnp.cumsum(nitem)
    n_items = (A // MOE_ALIGN + N_EXPERTS + N_EXPERTS * (per_item - 1)) // per_item
    w = jnp.arange(n_items, dtype=jnp.int32)
    total = iend[-1]
    wv = jnp.minimum(w, total - 1)
    item_e = jnp.minimum(jnp.searchsorted(iend, wv, side='right'), N_EXPERTS - 1).astype(jnp.int32)
    sb = wv - (iend - nitem)[item_e]
    item_r = (blk0[item_e] + sb * per_item).astype(jnp.int32)
    item_n = jnp.where(w < total, jnp.minimum(per_item, nblk[item_e] - sb * per_item), 0).astype(jnp.int32)
    n_blocks = jnp.sum(nblk).astype(jnp.int32).reshape(1)
    return dest, item_e, item_r, item_n, n_blocks


def kernel(x, norm1_g, w_in, hg_lb_logits, hg_norm_g, cmp_pe_k, cmp_w1_k, cmp_w2_k, cmp_pe_v,
           cmp_w1_v, cmp_w2_v, rel_bias, w_out, norm2_g, w_router_group, b_router_group,
           w_router_expert, b_router_expert, w_expert_gate, w_expert_up, w_expert_down,
           final_norm_g):
    B, S, D = x.shape
    T = B * S
    assert w_in.shape[0] == 1, "single-layer block"
    x2d = x.reshape(T, D)

    lower = jax.nn.softmax(hg_lb_logits.astype(F32), axis=0)[0].reshape(HG_HEADS, 1, HG_DIM)

    w_main = w_in[0, :, :MAIN_COLS].astype(BF16)
    w_gate_cols = jnp.pad(w_in[0, :, MAIN_COLS:].astype(BF16), ((0, 0), (0, LANES - GATE_COLS)))
    proj_hg, proj_nsa, glog = _inproj(x2d, norm1_g[0].reshape(1, D), w_main, w_gate_cols,
                                      tm=1024, tn=512)

    y_hg = _hgrn(proj_hg, lower, hg_norm_g[0].reshape(1, HG_DIM), B, S, tb=512)
    kcvc = _nsa_compress_all(proj_nsa, cmp_pe_k[0], cmp_w1_k[0], cmp_w2_k[0],
                             cmp_pe_v[0], cmp_w1_v[0], cmp_w2_v[0], B, S)
    y_nsa = _nsa(proj_nsa, glog, kcvc[0], kcvc[1], rel_bias, B, S)

    w_r = jnp.concatenate([w_router_group[0], w_router_expert[0]], axis=1)
    w_r = jnp.pad(w_r, ((0, 0), (0, LANES - w_r.shape[1])))
    wr_hi = w_r.astype(BF16)
    wr_lo = (w_r - wr_hi.astype(F32)).astype(BF16)
    b_r = jnp.concatenate([b_router_group[0], b_router_expert[0]])
    b_r = jnp.pad(b_r, (0, LANES - b_r.shape[0])).reshape(1, LANES)
    x2, h2, rt = _outproj(x2d, y_hg, y_nsa, w_out[0].astype(BF16), norm2_g[0].reshape(1, D),
                          wr_hi, wr_lo, b_r, tm=512)

    e_flat = rt[:, ROUTE_E1:ROUTE_E2 + 1].astype(jnp.int32).reshape(2 * T)
    dest, item_e, item_r, item_n, n_blocks = _moe_layout(e_flat)
    n_rows = (2 * T // MOE_ALIGN + N_EXPERTS) * MOE_ALIGN + MOE_TM
    xs = _dispatch(h2, dest, n_rows)
    ys = _moe(xs, item_e, item_r, item_n, n_blocks,
              w_expert_gate[0], w_expert_up[0], w_expert_down[0])

    out = _final(x2, ys, dest, rt, final_norm_g.reshape(1, D))
    return out.reshape(B, S, D)
```

```python
import functools
import math

import jax
import jax.numpy as jnp
import numpy as np
from jax import lax
from jax.experimental import pallas as pl
from jax.experimental.pallas import tpu as pltpu

F32 = jnp.float32
BF16 = jnp.bfloat16

D_MODEL = 2048
HG_HEADS = 8
HG_DIM = 128
HG_WIDTH = HG_HEADS * HG_DIM
HG_CHUNK = 64
NSA_HEADS = 16
NSA_GROUPS = 4
NSA_HPG = NSA_HEADS // NSA_GROUPS
NSA_DH = 64
NSA_WIDTH = NSA_HEADS * NSA_DH
NSA_KV = NSA_GROUPS * NSA_DH
CMP_LEN = 32
CMP_STRIDE = 16
CMP_HIDDEN = 256
SLC_LEN = 64
SLC_TOPK = 16
WIN = 512
REL_BUCKETS = 32
REL_MAX_DIST = 128
N_EXPERT_GROUPS = 8
EXPERTS_PER_GROUP = 8
N_EXPERTS = N_EXPERT_GROUPS * EXPERTS_PER_GROUP
EXPERT_DFF = 1024
RMS_EPS = 1e-6
NEG_INF = -1e30
LOG2E = math.log2(math.e)
NSA_Q_SCALE = NSA_DH ** -0.5 * LOG2E
BIG = 1e9

MAIN_COLS = 4 * HG_WIDTH + NSA_WIDTH + 6 * NSA_KV
GATE_COLS = 3 * NSA_HEADS
LANES = 128
VMEM_LIMIT = 56 * 1024 * 1024


def _cparams(*sem):
    return pltpu.CompilerParams(dimension_semantics=sem, vmem_limit_bytes=VMEM_LIMIT)


def _nt(a, b):
    return lax.dot_general(a, b, (((1,), (1,)), ((), ())), preferred_element_type=F32)


def _tn(a, b):
    return lax.dot_general(a, b, (((0,), (0,)), ((), ())), preferred_element_type=F32)


def _dot(a, b):
    return jnp.dot(a, b, preferred_element_type=F32)


def _inproj_body(x_ref, g_ref, w_ref, wg_ref, oh_ref, on_ref, og_ref, h_scr, *, hg_tiles, q_tiles):
    j = pl.program_id(1)

    @pl.when(j == 0)
    def _():
        x = x_ref[...]
        y = x * lax.rsqrt(jnp.mean(x * x, axis=-1, keepdims=True) + RMS_EPS) * g_ref[...]
        h_scr[...] = y.astype(BF16)
        og_ref[...] = _dot(h_scr[...], wg_ref[...])

    @pl.when(j < hg_tiles)
    def _():
        oh_ref[...] = _dot(h_scr[...], w_ref[...])

    @pl.when(j >= hg_tiles)
    def _():
        scale = jnp.where(j < hg_tiles + q_tiles, NSA_Q_SCALE, 1.0)
        on_ref[...] = (_dot(h_scr[...], w_ref[...]) * scale).astype(on_ref.dtype)


def _inproj(x2d, g, w_main, w_gate, tm, tn):
    T, D = x2d.shape
    N = w_main.shape[1]
    n_hg = 4 * HG_WIDTH
    hg_tiles = n_hg // tn
    return pl.pallas_call(
        functools.partial(_inproj_body, hg_tiles=hg_tiles, q_tiles=NSA_WIDTH // tn),
        grid=(T // tm, N // tn),
        in_specs=[
            pl.BlockSpec((tm, D), lambda i, j: (i, 0)),
            pl.BlockSpec((1, D), lambda i, j: (0, 0)),
            pl.BlockSpec((D, tn), lambda i, j: (0, j)),
            pl.BlockSpec((D, LANES), lambda i, j: (0, 0)),
        ],
        out_specs=[
            pl.BlockSpec((tm, tn), lambda i, j: (i, jnp.minimum(j, hg_tiles - 1))),
            pl.BlockSpec((tm, tn), lambda i, j: (i, jnp.maximum(j - hg_tiles, 0))),
            pl.BlockSpec((tm, LANES), lambda i, j: (i, 0)),
        ],
        out_shape=[jax.ShapeDtypeStruct((T, n_hg), F32),
                   jax.ShapeDtypeStruct((T, N - n_hg), BF16),
                   jax.ShapeDtypeStruct((T, LANES), F32)],
        scratch_shapes=[pltpu.VMEM((tm, D), BF16)],
        compiler_params=_cparams("arbitrary", "arbitrary"),
        name="inproj",
    )(x2d, g, w_main, w_gate)


def _hgrn_tables(C):
    t = np.arange(C)
    levels = []
    m = C // 2
    while m >= 1:
        levels.append(m)
        m //= 2
    rows = [np.tril(np.ones((C, C), np.float32)),
            np.triu(np.ones((C, C), np.float32), 1)]
    lev = np.full((C, C), -1, np.int32)
    for li, m in enumerate(levels):
        r = t % (2 * m)
        mid = t - r + m - 1
        M = np.zeros((C, C), np.float32)
        for i in range(C):
            if r[i] >= m:
                M[i, mid[i] + 1:i + 1] = 1.0
            else:
                M[i, i + 1:mid[i] + 1] = 1.0
        rows.append(M)
        same = (t[:, None] // (2 * m)) == (t[None, :] // (2 * m))
        lev[same & (r[:, None] >= m) & (r[None, :] < m)] = li
    lev[t[:, None] == t[None, :]] = len(levels)
    return np.concatenate(rows, 0), lev, len(levels)


HG_HEADS_PER_STEP = 8


def _hgrn_body(q_ref, f_ref, i_ref, g_ref, lb_ref, ng_ref, mst_ref, lev_ref, o_ref, st_ref,
               *, C, n_lev):
    @pl.when(pl.program_id(2) == 0)
    def _():
        st_ref[...] = jnp.zeros_like(st_ref)

    ng = ng_ref[...]
    mst = mst_ref[...]
    lev = lev_ref[...]
    n_chunks = q_ref.shape[0] // C

    def gates(rows, hb):
        cols = slice(hb * HG_DIM, (hb + 1) * HG_DIM)
        lb = lb_ref[hb]
        q = jax.nn.silu(q_ref[rows, cols])
        f = lb + (1.0 - lb) * jax.nn.sigmoid(f_ref[rows, cols])
        lf = jnp.log(f)
        lf0 = lf.astype(BF16)
        r1 = lf - lf0.astype(F32)
        lf1 = r1.astype(BF16)
        lf2 = (r1 - lf1.astype(F32)).astype(BF16)
        e = _dot(mst, jnp.concatenate([lf0, lf1, lf2], axis=0))
        return q, 1.0 - f, i_ref[rows, cols].astype(BF16), e

    def scores(q, k, gdec):
        a = jnp.where(lev == n_lev, _nt(q.astype(BF16), k.astype(BF16)), 0.0)
        for li in range(n_lev):
            g_m = gdec[(2 + li) * C:(3 + li) * C]
            a_m = _nt((q * g_m).astype(BF16), (k * g_m).astype(BF16))
            a = jnp.where(lev == li, a_m, a)
        return a.astype(BF16)

    def finish(rows, hb, q, k, v, gdec, a):
        cols = slice(hb * HG_DIM, (hb + 1) * HG_DIM)
        g_b = gdec[0:C]
        g_r = gdec[C:2 * C]
        st = st_ref[hb]
        o = _nt((q * g_b).astype(BF16), st.astype(BF16)) + _dot(a, v)
        g_last = g_b[C - 1:C, :]
        st_ref[hb] = st * g_last + _tn(v, (k * g_r).astype(BF16))
        o = o * lax.rsqrt(jnp.mean(o * o, axis=-1, keepdims=True) + RMS_EPS) * ng
        o_ref[rows, cols] = (o * jax.nn.silu(g_ref[rows, cols])).astype(o_ref.dtype)

    def chunk(c, carry):
        rows = pl.ds(pl.multiple_of(c * C, C), C)
        hs = range(HG_HEADS_PER_STEP)
        qkve = [gates(rows, hb) for hb in hs]
        gdec = [jnp.exp(x[3]) for x in qkve]
        a = [scores(qkve[hb][0], qkve[hb][1], gdec[hb]) for hb in hs]
        for hb in hs:
            finish(rows, hb, qkve[hb][0], qkve[hb][1], qkve[hb][2], gdec[hb], a[hb])
        return carry

    lax.fori_loop(0, n_chunks, chunk, 0)


def _hgrn(proj, lb, ng, B, S, tb):
    C = HG_CHUNK
    mst, lev, n_lev = _hgrn_tables(C)
    mst = np.concatenate([mst] * 3, axis=1)
    nb = S // tb
    HB = HG_HEADS_PER_STEP
    hsteps = HG_HEADS // HB

    def col(group):
        return pl.BlockSpec((tb, HB * HG_DIM), lambda b, h, t: (b * nb + t, group * hsteps + h))

    return pl.pallas_call(
        functools.partial(_hgrn_body, C=C, n_lev=n_lev),
        grid=(B, hsteps, nb),
        in_specs=[
            col(0), col(1), col(2), col(3),
            pl.BlockSpec((HB, 1, HG_DIM), lambda b, h, t: (h, 0, 0)),
            pl.BlockSpec((1, HG_DIM), lambda b, h, t: (0, 0)),
            pl.BlockSpec(mst.shape, lambda b, h, t: (0, 0)),
            pl.BlockSpec(lev.shape, lambda b, h, t: (0, 0)),
        ],
        out_specs=pl.BlockSpec((tb, HB * HG_DIM), lambda b, h, t: (b * nb + t, h)),
        out_shape=jax.ShapeDtypeStruct((B * S, HG_WIDTH), BF16),
        scratch_shapes=[pltpu.VMEM((HB, HG_DIM, HG_DIM), F32)],
        compiler_params=_cparams("arbitrary", "arbitrary", "arbitrary"),
        name="hgrn2",
    )(proj, proj, proj, proj, lb, ng, jnp.asarray(mst, BF16), jnp.asarray(lev))


def _compress_body(h_ref, pe_ref, w1_ref, w2_ref, o_ref):
    hv = h_ref[...]
    half = hv.shape[1]
    nh = hv.shape[0]
    u = _dot((hv + pe_ref[0:1, :]).astype(BF16), w1_ref[0:half, :].astype(BF16))
    v = _dot((hv + pe_ref[1:2, :]).astype(BF16), w1_ref[half:2 * half, :].astype(BF16))
    pre = u + pltpu.roll(v, nh - 1, axis=0)
    o_ref[...] = _dot(jax.nn.silu(pre).astype(BF16), w2_ref[...].astype(BF16)).astype(o_ref.dtype)


def _compress(halves, pe, w1, w2):
    _, B, G, NH, HW = halves.shape
    return pl.pallas_call(
        _compress_body,
        grid=(2, B, G),
        in_specs=[
            pl.BlockSpec((None, None, None, NH, HW), lambda s, b, g: (s, b, g, 0, 0)),
            pl.BlockSpec((None, 2, HW), lambda s, b, g: (s, 0, 0)),
            pl.BlockSpec((None, 2 * HW, CMP_HIDDEN), lambda s, b, g: (s, 0, 0)),
            pl.BlockSpec((None, CMP_HIDDEN, NSA_DH), lambda s, b, g: (s, 0, 0)),
        ],
        out_specs=pl.BlockSpec((None, None, None, NH, NSA_DH), lambda s, b, g: (s, b, g, 0, 0)),
        out_shape=jax.ShapeDtypeStruct((2, B, G, NH, NSA_DH), BF16),
        compiler_params=_cparams("arbitrary", "arbitrary", "arbitrary"),
        name="nsa_compress",
    )(halves, pe, w1, w2)


def _nsa_compress_all(proj, pe_k, w1_k, w2_k, pe_v, w1_v, w2_v, B, S):
    NH = S // CMP_STRIDE
    c0 = NSA_WIDTH

    def halves(idx):
        t = proj[:, c0 + idx * NSA_KV:c0 + (idx + 1) * NSA_KV]
        t = t.reshape(B, NH, CMP_STRIDE, NSA_GROUPS, NSA_DH).transpose(0, 3, 1, 2, 4)
        return t.reshape(B, NSA_GROUPS, NH, CMP_STRIDE * NSA_DH)

    hw = CMP_STRIDE * NSA_DH
    return _compress(jnp.stack([halves(0), halves(1)]),
                     jnp.stack([pe_k.reshape(2, hw), pe_v.reshape(2, hw)]),
                     jnp.stack([w1_k, w1_v]), jnp.stack([w2_k, w2_v]))


NSA_TQ = 128


NSA_MASK_ROWS = 64
NSA_WIN_MASK_ROWS = 16
NSA_FAR = 4
NSA_SLC_PAD = (NSA_FAR - 1) * NSA_TQ


NSA_V_ROWS = NSA_DH + 16


def _softmax_cols(s):
    m = jnp.max(s, axis=0, keepdims=True)
    return m, jnp.exp2(s - m).astype(BF16)


def _normalise(acc):
    return acc[0:NSA_DH] * (1.0 / acc[NSA_DH:NSA_DH + 1])


NSA_GROUPS_PER_STEP = 4


def _nsa_body(qt_ref, gl_ref, kc_ref, vct_ref, ks_ref, vst_ref, kw_ref, vwt_ref,
              bc_ref, bw_ref, bn_ref, ovt_ref, o_ref, *, top_k, n_slc):
    TQ = NSA_TQ
    GG = NSA_GROUPS_PER_STEP
    i = pl.program_id(2)
    ovt = ovt_ref[...]
    jb = lax.broadcasted_iota(jnp.int32, (n_slc, TQ), 0)
    qblk = (i * TQ + lax.broadcasted_iota(jnp.int32, (n_slc, TQ), 1)) // SLC_LEN
    forced = (jb == 0) | (jb == qblk) | (jb == qblk - 1)
    future = jb > qblk
    near = pl.ds(pl.multiple_of(NSA_SLC_PAD + (i - 1) * TQ, TQ), 2 * TQ)

    def heads(a):
        return jnp.concatenate([a] * NSA_HPG, axis=1)

    def queries(g):
        qt = qt_ref[g]
        return jnp.concatenate([qt[h * NSA_DH:(h + 1) * NSA_DH, :] for h in range(NSA_HPG)],
                               axis=1)

    def compressed(g, s):
        m = jnp.max(s, axis=0, keepdims=True)
        p = jnp.where(s > 0.5 * NEG_INF, jnp.exp2(s - m), 0.0)
        acc = _dot(vct_ref[g], p.astype(BF16))
        inv_l = 1.0 / jnp.maximum(acc[NSA_DH:NSA_DH + 1], 1e-30)
        o_cmp = acc[0:NSA_DH] * inv_l
        pc = p * inv_l
        ps = pc[:, 0:TQ] + pc[:, TQ:2 * TQ] + pc[:, 2 * TQ:3 * TQ] + pc[:, 3 * TQ:4 * TQ]
        ps_hi = ps.astype(BF16)
        ps_lo = (ps - ps_hi.astype(F32)).astype(BF16)
        imp = _dot(ovt, ps_hi) + _dot(ovt, ps_lo)
        return o_cmp, jnp.where(forced, BIG, jnp.where(future, -BIG, imp))

    def window(g, s):
        _, p = _softmax_cols(s)
        return _normalise(_dot(vwt_ref[g, :, win], p))

    def select(imp, q4):
        slabs = []
        for v in range(n_slc // 8):
            slab = imp[8 * v:8 * v + 8, :]
            jbs = jb[8 * v:8 * v + 8, :]
            cnt = jnp.zeros((8, TQ), F32)
            for jp in range(n_slc):
                row = imp[jp:jp + 1, :]
                if jp < 8 * v:
                    beats = row >= slab
                elif jp >= 8 * v + 8:
                    beats = row > slab
                else:
                    beats = (row > slab) | ((row == slab) & (jbs > jp))
                cnt = cnt + jnp.where(beats, 1.0, 0.0)
            slabs.append(jnp.where(cnt < top_k, 0.0, 1.0))
        if n_slc < NSA_MASK_ROWS:
            slabs.append(jnp.ones((NSA_MASK_ROWS - n_slc, TQ), F32))
        unsel = jnp.concatenate(slabs, axis=0)
        return jnp.concatenate([heads(unsel.astype(BF16)), q4], axis=0)

    def near_tiles(g, s):
        m, p = _softmax_cols(s)
        return m, _dot(vst_ref[g, :, near], p)

    win = pl.ds(pl.multiple_of(i * TQ, TQ), WIN + TQ)
    pad_row = jnp.where(lax.broadcasted_iota(jnp.int32, (NSA_WIN_MASK_ROWS, NSA_HPG * TQ), 0) == 0,
                        1.0, 0.0).astype(BF16)
    gs = range(GG)
    q4 = [queries(g) for g in gs]
    nh = kc_ref.shape[1]
    cmp_rows = pl.ds(pl.multiple_of(nh - (TQ // CMP_STRIDE) * i, TQ // CMP_STRIDE), nh)
    s_cmp = [_dot(kc_ref[g], q4[g]) + bc_ref[g, cmp_rows, :] for g in gs]
    s_win = [_dot(kw_ref[g, win, :], jnp.concatenate([pad_row, q4[g]], axis=0)) + bw_ref[g] for g in gs]
    cmp_out = [compressed(g, s_cmp[g]) for g in gs]
    o_win = [window(g, s_win[g]) for g in gs]
    qa = [select(cmp_out[g][1], q4[g]) for g in gs]
    s_near = [_dot(ks_ref[g, near, :], qa[g]) + bn_ref[g] for g in gs]
    start = tuple(near_tiles(g, s_near[g]) for g in gs)

    n_far = i - 1
    n_chunks = jnp.maximum((n_far + NSA_FAR - 1) // NSA_FAR, 0)
    far0 = NSA_SLC_PAD + TQ * (n_far - NSA_FAR * n_chunks)

    def far_step(c, carry):
        keys = pl.ds(pl.multiple_of(far0 + c * NSA_FAR * TQ, TQ), NSA_FAR * TQ)
        out = []
        logits = [_dot(ks_ref[g, keys, :], qa[g]) for g in gs]
        for g in gs:
            m, acc = carry[g]
            s = logits[g]
            m_new = jnp.maximum(m, jnp.max(s, axis=0, keepdims=True))
            p = jnp.exp2(s - m_new).astype(BF16)
            acc = jnp.exp2(m - m_new) * acc + _dot(vst_ref[g, :, keys], p)
            out.append((m_new, acc))
        return tuple(out)

    far = lax.fori_loop(0, n_chunks, far_step, start)

    for g in gs:
        o_slc = _normalise(far[g][1])
        o_cmp = cmp_out[g][0]
        sg = jax.nn.sigmoid(gl_ref[g])
        outs = []
        for h in range(NSA_HPG):
            lanes = slice(h * TQ, (h + 1) * TQ)
            outs.append(sg[h:h + 1, :] * o_cmp[:, lanes]
                        + sg[NSA_HPG + h:NSA_HPG + h + 1, :] * o_slc[:, lanes]
                        + sg[2 * NSA_HPG + h:2 * NSA_HPG + h + 1, :] * o_win[g][:, lanes])
        width = NSA_HPG * NSA_DH
        o_ref[:, g * width:(g + 1) * width] = jnp.concatenate(outs, axis=0).T.astype(o_ref.dtype)


def _rel_bucket(dist):
    n = jnp.maximum(dist, 0)
    max_exact = REL_BUCKETS // 2
    nf = jnp.maximum(n, 1).astype(F32)
    large = max_exact + (jnp.log(nf / max_exact) / math.log(REL_MAX_DIST / max_exact)
                         * (REL_BUCKETS - max_exact)).astype(jnp.int32)
    large = jnp.minimum(large, REL_BUCKETS - 1)
    return jnp.where(n < max_exact, n, large)


def _bias_table(table, dist, valid):
    onehot = (_rel_bucket(jnp.asarray(dist, jnp.int32))[..., None]
              == jnp.arange(REL_BUCKETS, dtype=jnp.int32)).astype(F32)
    b = jnp.einsum('rmk,ghk->ghrm', onehot, table, precision=lax.Precision.HIGHEST)
    return jnp.where(jnp.asarray(valid), b, NEG_INF)


def _nsa(proj, glog, kc, vc, rel_bias, B, S):
    TQ = NSA_TQ
    G, HPG, DH = NSA_GROUPS, NSA_HPG, NSA_DH
    NH = S // CMP_STRIDE
    n_cmp = (S - CMP_LEN) // CMP_STRIDE + 1
    n_slc = S // SLC_LEN
    top_k = min(SLC_TOPK, n_slc)
    nt = S // TQ
    c0 = NSA_WIDTH

    MR = NSA_MASK_ROWS
    assert n_slc % 8 == 0 and (n_slc < MR or (n_slc == MR and top_k < n_slc))

    def kv(idx, pad):
        t = proj[:, c0 + idx * NSA_KV:c0 + (idx + 1) * NSA_KV].reshape(B, S, G, DH)
        t = jnp.pad(t, ((0, 0), (pad, 0), (0, 0), (0, 0)))
        return t.transpose(0, 2, 1, 3)

    def keys_with_masks(idx, pad, with_blocks):
        pos = np.arange(pad + S) - pad
        if with_blocks:
            cols = np.zeros((pad + S, MR), np.float32)
            cols[:, :n_slc] = ((pos // SLC_LEN)[:, None] == np.arange(n_slc)[None, :]) & (pos >= 0)[:, None]
            if n_slc < MR:
                cols[:, n_slc] = pos < 0
            else:
                cols[pos < 0, :] = 1.0
        else:
            cols = np.zeros((pad + S, NSA_WIN_MASK_ROWS), np.float32)
            cols[:, 0] = pos < 0
        m = jnp.broadcast_to(jnp.asarray(cols * NEG_INF, BF16), (B, G) + cols.shape)
        return jnp.concatenate([m, kv(idx, pad)], axis=3)

    def values_t(v, real):
        ones = jnp.zeros((NSA_V_ROWS - DH, v.shape[2]), BF16).at[0].set(jnp.asarray(real, BF16))
        return jnp.concatenate([v.transpose(0, 1, 3, 2), jnp.broadcast_to(ones, (B, G) + ones.shape)],
                               axis=2)

    SP = NSA_SLC_PAD
    ks, vst = keys_with_masks(2, SP, True), values_t(kv(3, SP), np.arange(SP + S) >= SP)
    kw, vwt = keys_with_masks(4, WIN, False), values_t(kv(5, WIN), np.arange(WIN + S) >= WIN)
    kcb = kc
    vct = values_t(vc, np.ones(NH, bool))
    qt = proj[:, :c0].reshape(B, S, G, HPG * DH).transpose(0, 2, 3, 1)

    table = rel_bias.T.reshape(G, HPG, REL_BUCKETS) * LOG2E
    far = table[:, :, REL_BUCKETS - 1]
    r = np.arange(TQ)[:, None]

    def keys_major(b):
        return b.transpose(0, 3, 1, 2).reshape(G, b.shape[3], HPG * TQ)

    d_c = r - (CMP_LEN - 1) - CMP_STRIDE * (np.arange(2 * NH)[None, :] - NH)
    bias_cmp = keys_major(_bias_table(table, d_c, d_c >= 0))
    d_w = r + WIN - np.arange(WIN + TQ)[None, :]
    bias_win = keys_major(_bias_table(table, d_w, (d_w >= 0) & (d_w < WIN)))
    d_n = r + TQ - np.arange(2 * TQ)[None, :]
    bias_near = keys_major(_bias_table(table - far[:, :, None], d_n, d_n >= 0))

    ci = np.arange(NH)[None, :] * CMP_STRIDE
    sj = np.arange(n_slc)[:, None] * SLC_LEN
    overlap_t = ((ci <= sj + SLC_LEN - 1) & (ci + CMP_LEN - 1 >= sj) & (np.arange(NH)[None, :] < n_cmp))

    gate_rows = 16
    glog_g = glog[:, :GATE_COLS].reshape(B * S, 3, G, HPG).transpose(2, 1, 3, 0).reshape(G, 3 * HPG, B * S)
    glog_g = jnp.pad(glog_g, ((0, 0), (0, gate_rows - 3 * HPG), (0, 0)))

    GG = NSA_GROUPS_PER_STEP
    whole = lambda shape: pl.BlockSpec((None, GG) + shape, lambda b, g, i: (b, g, 0, 0))
    tile = lambda k: pl.BlockSpec((GG, k, HPG * TQ), lambda b, g, i: (g, 0, 0))
    return pl.pallas_call(
        functools.partial(_nsa_body, top_k=top_k, n_slc=n_slc),
        grid=(B, G // GG, nt),
        in_specs=[
            pl.BlockSpec((None, GG, HPG * DH, TQ), lambda b, g, i: (b, g, 0, i)),
            pl.BlockSpec((GG, gate_rows, TQ), lambda b, g, i: (g, 0, b * nt + i)),
            whole((NH, DH)), whole((NSA_V_ROWS, NH)),
            whole((S + SP, MR + DH)), whole((NSA_V_ROWS, S + SP)),
            whole((S + WIN, NSA_WIN_MASK_ROWS + DH)), whole((NSA_V_ROWS, S + WIN)),
            tile(2 * NH), tile(WIN + TQ), tile(2 * TQ),
            pl.BlockSpec((n_slc, NH), lambda b, g, i: (0, 0)),
        ],
        out_specs=pl.BlockSpec((TQ, GG * HPG * DH), lambda b, g, i: (b * nt + i, g)),
        out_shape=jax.ShapeDtypeStruct((B * S, NSA_WIDTH), BF16),
        compiler_params=_cparams("arbitrary", "arbitrary", "arbitrary"),
        name="nsa_attention",
    )(qt, glog_g, kcb, vct, ks, vst, kw, vwt, bias_cmp, bias_win, bias_near,
      jnp.asarray(overlap_t, BF16))


ROUTE_E1, ROUTE_E2, ROUTE_W1, ROUTE_W2 = 0, 1, 2, 3


def _outproj_body(x_ref, yh_ref, yn_ref, wo_ref, g2_ref, wrh_ref, wrl_ref, br_ref,
                  x2_ref, h2_ref, rt_ref):
    x2 = (x_ref[...] + _dot(yh_ref[...], wo_ref[0:HG_WIDTH, :])
          + _dot(yn_ref[...], wo_ref[HG_WIDTH:HG_WIDTH + NSA_WIDTH, :]))
    x2_ref[...] = x2
    h = x2 * lax.rsqrt(jnp.mean(x2 * x2, axis=-1, keepdims=True) + RMS_EPS) * g2_ref[...]
    hb = h.astype(BF16)
    hbf = hb.astype(F32)
    half = h.shape[1] // 2
    bits = pltpu.bitcast(hbf, jnp.int32)
    h2_ref[...] = (lax.shift_right_logical(bits[:, :half], 16)
                   | (bits[:, half:] & jnp.int32(-65536)))
    hl = (h - hbf).astype(BF16)
    wrh = wrh_ref[...]
    lg = _dot(hb, wrh) + _dot(hl, wrh) + _dot(hb, wrl_ref[...]) + br_ref[...]
    lane = lax.broadcasted_iota(jnp.int32, lg.shape, 1)
    first = lambda hit: jnp.min(jnp.where(hit, lane, LANES), axis=-1, keepdims=True)
    lgg = jnp.where(lane < N_EXPERT_GROUPS, lg, NEG_INF)
    mg = jnp.max(lgg, axis=-1, keepdims=True)
    p_top = 1.0 / jnp.sum(jnp.exp(lgg - mg), axis=-1, keepdims=True)
    lo = N_EXPERT_GROUPS + EXPERTS_PER_GROUP * first(lgg == mg)
    le = jnp.where((lane >= lo) & (lane < lo + EXPERTS_PER_GROUP), lg, NEG_INF)
    v1 = jnp.max(le, axis=-1, keepdims=True)
    i1 = first(le == v1)
    le = jnp.where(lane == i1, NEG_INF, le)
    v2 = jnp.max(le, axis=-1, keepdims=True)
    i2 = first(le == v2)
    e21 = jnp.exp(v2 - v1)
    w1 = p_top / (1.0 + e21)
    rt = jnp.where(lane == ROUTE_E1, (i1 - N_EXPERT_GROUPS).astype(F32), 0.0)
    rt = jnp.where(lane == ROUTE_E2, (i2 - N_EXPERT_GROUPS).astype(F32), rt)
    rt = jnp.where(lane == ROUTE_W1, w1, rt)
    rt_ref[...] = jnp.where(lane == ROUTE_W2, w1 * e21, rt)


def _outproj(x2d, y_hg, y_nsa, w_out, g2, wr_hi, wr_lo, b_r, tm):
    T, D = x2d.shape
    row = lambda n: pl.BlockSpec((tm, n), lambda i: (i, 0))
    full = lambda a: pl.BlockSpec(a.shape, lambda i: (0, 0))
    return pl.pallas_call(
        _outproj_body,
        grid=(T // tm,),
        in_specs=[row(D), row(HG_WIDTH), row(NSA_WIDTH), full(w_out), full(g2),
                  full(wr_hi), full(wr_lo), full(b_r)],
        out_specs=[row(D), row(D // 2), row(LANES)],
        out_shape=[jax.ShapeDtypeStruct((T, D), F32), jax.ShapeDtypeStruct((T, D // 2), jnp.int32),
                   jax.ShapeDtypeStruct((T, LANES), F32)],
        compiler_params=_cparams("arbitrary"),
        name="outproj_router",
    )(x2d, y_hg, y_nsa, w_out, g2, wr_hi, wr_lo, b_r)


ROW_TILE = 256


def _dispatch_body(dest_ref, h_ref, xs0_hbm, xs_hbm, sem):
    del xs0_hbm
    tb = h_ref.shape[0]

    def start(t, carry):
        for k in range(2):
            pltpu.make_async_copy(h_ref.at[pl.ds(t, 1), :],
                                  xs_hbm.at[pl.ds(dest_ref[0, 2 * t + k], 1), :], sem).start()
        return carry

    lax.fori_loop(0, tb, start, 0, unroll=8)
    for _ in range(2):
        pltpu.make_async_copy(h_ref, xs_hbm.at[pl.ds(0, tb), :], sem).wait()


def _dispatch(h2p, dest, n_rows):
    T, W = h2p.shape
    tb = ROW_TILE
    return pl.pallas_call(
        _dispatch_body,
        grid=(T // tb,),
        in_specs=[pl.BlockSpec((None, 1, 2 * tb), lambda i: (i, 0, 0), memory_space=pltpu.SMEM),
                  pl.BlockSpec((tb, W), lambda i: (i, 0)),
                  pl.BlockSpec(memory_space=pl.ANY)],
        out_specs=pl.BlockSpec(memory_space=pl.ANY),
        out_shape=jax.ShapeDtypeStruct((n_rows, W), jnp.int32),
        scratch_shapes=[pltpu.SemaphoreType.DMA(())],
        input_output_aliases={2: 0},
        compiler_params=_cparams("arbitrary"),
        name="moe_dispatch",
    )(dest.reshape(T // tb, 1, 2 * tb), h2p, jnp.zeros((n_rows, W), jnp.int32))


MOE_ALIGN = 128
MOE_TM = 1024
MOE_SUB = 256
MOE_TF = 256


def _moe_body(ie_ref, ir_ref, in_ref, nb_ref, xs_hbm, wg_ref, wu_ref, wd_ref, y_hbm,
              xbuf, xlo, xhi, acc, sem_in, sem_out):
    del ie_ref
    w = pl.program_id(0)
    f = pl.program_id(1)
    n_w = pl.num_programs(0)
    last_f = pl.num_programs(1) - 1
    nsub = in_ref[w]
    row0 = pl.multiple_of(ir_ref[w] * MOE_ALIGN, MOE_ALIGN)
    slot = w % 2
    per_mm = MOE_SUB // MOE_ALIGN
    n_full = nsub // per_mm
    half = xlo.shape[1]

    def x_copy(item, s):
        r = pl.multiple_of(ir_ref[item] * MOE_ALIGN, MOE_ALIGN)
        return pltpu.make_async_copy(xs_hbm.at[pl.ds(r, MOE_TM), :], xbuf.at[s], sem_in.at[s])

    def y_copy(r, dst_row):
        return pltpu.make_async_copy(acc.at[pl.ds(r, MOE_ALIGN), :],
                                     y_hbm.at[pl.ds(dst_row, MOE_ALIGN), :], sem_out)

    def for_row_blocks(fn):
        def body(j, carry):
            fn(pl.multiple_of(j * MOE_SUB, MOE_SUB), MOE_SUB)
            return carry

        lax.fori_loop(0, n_full, body, 0)

        @pl.when(nsub % per_mm == 1)
        def _():
            fn(pl.multiple_of(n_full * MOE_SUB, MOE_ALIGN), MOE_ALIGN)

    @pl.when((f == 0) & (nsub > 0))
    def _():
        @pl.when(w == 0)
        def _():
            x_copy(0, 0).start()

        x_copy(w, slot).wait()

        def unpack(r, size):
            bits = xbuf[slot, pl.ds(r, size), :]
            xlo[pl.ds(r, size), :] = pltpu.bitcast(lax.shift_left(bits, 16), F32).astype(BF16)
            xhi[pl.ds(r, size), :] = pltpu.bitcast(bits & jnp.int32(-65536), F32).astype(BF16)

        for_row_blocks(unpack)

    def experts(first, last):
        wg = wg_ref[...].astype(BF16)
        wu = wu_ref[...].astype(BF16)
        wd = wd_ref[...].astype(BF16)

        def block(r, size):
            rows = pl.ds(r, size)
            lo = xlo[rows, :]
            hi = xhi[rows, :]
            hg = _dot(lo, wg[0:half, :]) + _dot(hi, wg[half:2 * half, :])
            hu = _dot(lo, wu[0:half, :]) + _dot(hi, wu[half:2 * half, :])
            y = _dot((jax.nn.silu(hg) * hu).astype(BF16), wd)
            if first:
                acc[rows, :] = y
            else:
                acc[rows, :] += y
            if last:
                for k in range(size // MOE_ALIGN):
                    y_copy(r + k * MOE_ALIGN, row0 + r + k * MOE_ALIGN).start()

        for_row_blocks(block)

    @pl.when((f == 0) & (nsub > 0))
    def _():
        experts(True, False)

    @pl.when((f > 0) & (f < last_f) & (nsub > 0))
    def _():
        experts(False, False)

    @pl.when((f == last_f) & (nsub > 0))
    def _():
        nxt = jnp.minimum(w + 1, n_w - 1)

        @pl.when((w + 1 < n_w) & (in_ref[nxt] > 0))
        def _():
            x_copy(nxt, 1 - slot).start()

        experts(False, True)

        def wait(j, carry):
            y_copy(0, 0).wait()
            return carry

        lax.fori_loop(0, nsub, wait, 0)

    @pl.when((f == last_f) & (w == n_w - 1))
    def _():
        acc[0:MOE_ALIGN, :] = jnp.zeros((MOE_ALIGN, acc.shape[1]), F32)
        used = nb_ref[0]
        total = y_hbm.shape[0] // MOE_ALIGN

        def fill(j, carry):
            y_copy(0, pl.multiple_of(j * MOE_ALIGN, MOE_ALIGN)).start()
            return carry

        def wait(j, carry):
            y_copy(0, 0).wait()
            return carry

        lax.fori_loop(used, total, fill, 0)
        lax.fori_loop(used, total, wait, 0)


def _moe(xs, item_e, item_r, item_n, n_blocks, w_gate, w_up, w_down):
    rows = xs.shape[0]
    D = 2 * xs.shape[1]
    n_items = item_e.shape[0]
    nf = EXPERT_DFF // MOE_TF
    assert nf >= 2 and MOE_SUB == 2 * MOE_ALIGN

    def fcol(f, nn, w):
        return jnp.where(nn[w] > 0, f, nf - 1)

    return pl.pallas_call(
        _moe_body,
        grid_spec=pltpu.PrefetchScalarGridSpec(
            num_scalar_prefetch=4,
            grid=(n_items, nf),
            in_specs=[
                pl.BlockSpec(memory_space=pl.ANY),
                pl.BlockSpec((None, D, MOE_TF), lambda w, f, ie, ir, nn, nb: (ie[w], 0, fcol(f, nn, w))),
                pl.BlockSpec((None, D, MOE_TF), lambda w, f, ie, ir, nn, nb: (ie[w], 0, fcol(f, nn, w))),
                pl.BlockSpec((None, MOE_TF, D), lambda w, f, ie, ir, nn, nb: (ie[w], fcol(f, nn, w), 0)),
            ],
            out_specs=pl.BlockSpec(memory_space=pl.ANY),
            scratch_shapes=[pltpu.VMEM((2, MOE_TM, D // 2), jnp.int32),
                            pltpu.VMEM((MOE_TM, D // 2), BF16), pltpu.VMEM((MOE_TM, D // 2), BF16),
                            pltpu.VMEM((MOE_TM, D), F32),
                            pltpu.SemaphoreType.DMA((2,)), pltpu.SemaphoreType.DMA(())],
        ),
        out_shape=jax.ShapeDtypeStruct((rows, D), F32),
        compiler_params=_cparams("arbitrary", "arbitrary"),
        name="moe_experts",
    )(item_e, item_r, item_n, n_blocks, xs, w_gate, w_up, w_down)


def _final_body(dcur_ref, dnext_ref, x2_ref, rt_ref, g_ref, ys_hbm, o_ref, ybuf, sems):
    i = pl.program_id(0)
    tb = x2_ref.shape[0]
    slot = i % 2

    def gather(d_ref, s):
        def body(t, carry):
            for k in range(2):
                pltpu.make_async_copy(ys_hbm.at[pl.ds(d_ref[0, 2 * t + k], 1), :],
                                      ybuf.at[s, k, pl.ds(t, 1), :], sems.at[s]).start()
            return carry

        lax.fori_loop(0, tb, body, 0, unroll=8)

    @pl.when(i == 0)
    def _():
        gather(dcur_ref, 0)

    @pl.when(i + 1 < pl.num_programs(0))
    def _():
        gather(dnext_ref, 1 - slot)

    for k in range(2):
        pltpu.make_async_copy(ys_hbm.at[pl.ds(0, tb), :], ybuf.at[slot, k], sems.at[slot]).wait()
    rt = rt_ref[...]
    x = (x2_ref[...] + rt[:, ROUTE_W1:ROUTE_W1 + 1] * ybuf[slot, 0]
         + rt[:, ROUTE_W2:ROUTE_W2 + 1] * ybuf[slot, 1])
    o_ref[...] = x * lax.rsqrt(jnp.mean(x * x, axis=-1, keepdims=True) + RMS_EPS) * g_ref[...]


def _final(x2, ys, dest, rt, g):
    T, D = x2.shape
    tb = ROW_TILE
    nb = T // tb
    dest3 = dest.reshape(nb, 1, 2 * tb)
    return pl.pallas_call(
        _final_body,
        grid=(nb,),
        in_specs=[pl.BlockSpec((None, 1, 2 * tb), lambda i: (i, 0, 0), memory_space=pltpu.SMEM),
                  pl.BlockSpec((None, 1, 2 * tb), lambda i: (jnp.minimum(i + 1, nb - 1), 0, 0),
                               memory_space=pltpu.SMEM),
                  pl.BlockSpec((tb, D), lambda i: (i, 0)),
                  pl.BlockSpec((tb, LANES), lambda i: (i, 0)),
                  pl.BlockSpec((1, D), lambda i: (0, 0)),
                  pl.BlockSpec(memory_space=pl.ANY)],
        out_specs=pl.BlockSpec((tb, D), lambda i: (i, 0)),
        out_shape=jax.ShapeDtypeStruct((T, D), F32),
        scratch_shapes=[pltpu.VMEM((2, 2, tb, D), F32), pltpu.SemaphoreType.DMA((2,))],
        compiler_params=_cparams("arbitrary"),
        name="combine_final_norm",
    )(dest3, dest3, x2, rt, g, ys)


def _moe_layout(e_flat):
    A = e_flat.shape[0]
    onehot = (e_flat[:, None] == jnp.arange(N_EXPERTS, dtype=jnp.int32)[None, :]).astype(jnp.int32)
    csum = jnp.cumsum(onehot, axis=0)
    counts = csum[-1]
    nblk = (counts + MOE_ALIGN - 1) // MOE_ALIGN
    blk0 = jnp.cumsum(nblk) - nblk
    dest = jnp.sum(onehot * (blk0[None, :] * MOE_ALIGN + csum - 1), axis=1).astype(jnp.int32)
    per_item = MOE_TM // MOE_ALIGN
    nitem = (nblk + per_item - 1) // per_item
    iend = jnp.cumsum(nitem)
    n_items = (A // MOE_ALIGN + N_EXPERTS + N_EXPERTS * (per_item - 1)) // per_item
    w = jnp.arange(n_items, dtype=jnp.int32)
    total = iend[-1]
    wv = jnp.minimum(w, total - 1)
    item_e = jnp.minimum(jnp.searchsorted(iend, wv, side='right'), N_EXPERTS - 1).astype(jnp.int32)
    sb = wv - (iend - nitem)[item_e]
    item_r = (blk0[item_e] + sb * per_item).astype(jnp.int32)
    item_n = jnp.where(w < total, jnp.minimum(per_item, nblk[item_e] - sb * per_item), 0).astype(jnp.int32)
    n_blocks = jnp.sum(nblk).astype(jnp.int32).reshape(1)
    return dest, item_e, item_r, item_n, n_blocks


def kernel(x, norm1_g, w_in, hg_lb_logits, hg_norm_g, cmp_pe_k, cmp_w1_k, cmp_w2_k, cmp_pe_v,
           cmp_w1_v, cmp_w2_v, rel_bias, w_out, norm2_g, w_router_group, b_router_group,
           w_router_expert, b_router_expert, w_expert_gate, w_expert_up, w_expert_down,
           final_norm_g):
    B, S, D = x.shape
    T = B * S
    assert w_in.shape[0] == 1, "single-layer block"
    x2d = x.reshape(T, D)

    lower = jax.nn.softmax(hg_lb_logits.astype(F32), axis=0)[0].reshape(HG_HEADS, 1, HG_DIM)

    w_main = w_in[0, :, :MAIN_COLS].astype(BF16)
    w_gate_cols = jnp.pad(w_in[0, :, MAIN_COLS:].astype(BF16), ((0, 0), (0, LANES - GATE_COLS)))
    proj_hg, proj_nsa, glog = _inproj(x2d, norm1_g[0].reshape(1, D), w_main, w_gate_cols,
                                      tm=1024, tn=512)

    y_hg = _hgrn(proj_hg, lower, hg_norm_g[0].reshape(1, HG_DIM), B, S, tb=512)
    kcvc = _nsa_compress_all(proj_nsa, cmp_pe_k[0], cmp_w1_k[0], cmp_w2_k[0],
                             cmp_pe_v[0], cmp_w1_v[0], cmp_w2_v[0], B, S)
    y_nsa = _nsa(proj_nsa, glog, kcvc[0], kcvc[1], rel_bias, B, S)

    w_r = jnp.concatenate([w_router_group[0], w_router_expert[0]], axis=1)
    w_r = jnp.pad(w_r, ((0, 0), (0, LANES - w_r.shape[1])))
    wr_hi = w_r.astype(BF16)
    wr_lo = (w_r - wr_hi.astype(F32)).astype(BF16)
    b_r = jnp.concatenate([b_router_group[0], b_router_expert[0]])
    b_r = jnp.pad(b_r, (0, LANES - b_r.shape[0])).reshape(1, LANES)
    x2, h2, rt = _outproj(x2d, y_hg, y_nsa, w_out[0].astype(BF16), norm2_g[0].reshape(1, D),
                          wr_hi, wr_lo, b_r, tm=512)

    e_flat = rt[:, ROUTE_E1:ROUTE_E2 + 1].astype(jnp.int32).reshape(2 * T)
    dest, item_e, item_r, item_n, n_blocks = _moe_layout(e_flat)
    n_rows = (2 * T // MOE_ALIGN + N_EXPERTS) * MOE_ALIGN + MOE_TM
    xs = _dispatch(h2, dest, n_rows)
    ys = _moe(xs, item_e, item_r, item_n, n_blocks,
              w_expert_gate[0], w_expert_up[0], w_expert_down[0])

    out = _final(x2, ys, dest, rt, final_norm_g.reshape(1, D))
    return out.reshape(B, S, D)
```

```python
import functools
import math

import jax
import jax.numpy as jnp
import numpy as np
from jax import lax
from jax.experimental import pallas as pl
from jax.experimental.pallas import tpu as pltpu

F32 = jnp.float32
BF16 = jnp.bfloat16

D_MODEL = 2048
HG_HEADS = 8
HG_DIM = 128
HG_WIDTH = HG_HEADS * HG_DIM
HG_CHUNK = 64
NSA_HEADS = 16
NSA_GROUPS = 4
NSA_HPG = NSA_HEADS // NSA_GROUPS
NSA_DH = 64
NSA_WIDTH = NSA_HEADS * NSA_DH
NSA_KV = NSA_GROUPS * NSA_DH
CMP_LEN = 32
CMP_STRIDE = 16
CMP_HIDDEN = 256
SLC_LEN = 64
SLC_TOPK = 16
WIN = 512
REL_BUCKETS = 32
REL_MAX_DIST = 128
N_EXPERT_GROUPS = 8
EXPERTS_PER_GROUP = 8
N_EXPERTS = N_EXPERT_GROUPS * EXPERTS_PER_GROUP
EXPERT_DFF = 1024
RMS_EPS = 1e-6
NEG_INF = -1e30
LOG2E = math.log2(math.e)
NSA_Q_SCALE = NSA_DH ** -0.5 * LOG2E
BIG = 1e9

MAIN_COLS = 4 * HG_WIDTH + NSA_WIDTH + 6 * NSA_KV
GATE_COLS = 3 * NSA_HEADS
LANES = 128
VMEM_LIMIT = 56 * 1024 * 1024


def _cparams(*sem):
    return pltpu.CompilerParams(dimension_semantics=sem, vmem_limit_bytes=VMEM_LIMIT)


def _nt(a, b):
    return lax.dot_general(a, b, (((1,), (1,)), ((), ())), preferred_element_type=F32)


def _tn(a, b):
    return lax.dot_general(a, b, (((0,), (0,)), ((), ())), preferred_element_type=F32)


def _dot(a, b):
    return jnp.dot(a, b, preferred_element_type=F32)


def _inproj_body(x_ref, g_ref, w_ref, wg_ref, oh_ref, on_ref, og_ref, h_scr, *, hg_tiles, q_tiles):
    j = pl.program_id(1)

    @pl.when(j == 0)
    def _():
        x = x_ref[...]
        y = x * lax.rsqrt(jnp.mean(x * x, axis=-1, keepdims=True) + RMS_EPS) * g_ref[...]
        h_scr[...] = y.astype(BF16)
        og_ref[...] = _dot(h_scr[...], wg_ref[...])

    @pl.when(j < hg_tiles)
    def _():
        oh_ref[...] = _dot(h_scr[...], w_ref[...])

    @pl.when(j >= hg_tiles)
    def _():
        scale = jnp.where(j < hg_tiles + q_tiles, NSA_Q_SCALE, 1.0)
        on_ref[...] = (_dot(h_scr[...], w_ref[...]) * scale).astype(on_ref.dtype)


def _inproj(x2d, g, w_main, w_gate, tm, tn):
    T, D = x2d.shape
    N = w_main.shape[1]
    n_hg = 4 * HG_WIDTH
    hg_tiles = n_hg // tn
    return pl.pallas_call(
        functools.partial(_inproj_body, hg_tiles=hg_tiles, q_tiles=NSA_WIDTH // tn),
        grid=(T // tm, N // tn),
        in_specs=[
            pl.BlockSpec((tm, D), lambda i, j: (i, 0)),
            pl.BlockSpec((1, D), lambda i, j: (0, 0)),
            pl.BlockSpec((D, tn), lambda i, j: (0, j)),
            pl.BlockSpec((D, LANES), lambda i, j: (0, 0)),
        ],
        out_specs=[
            pl.BlockSpec((tm, tn), lambda i, j: (i, jnp.minimum(j, hg_tiles - 1))),
            pl.BlockSpec((tm, tn), lambda i, j: (i, jnp.maximum(j - hg_tiles, 0))),
            pl.BlockSpec((tm, LANES), lambda i, j: (i, 0)),
        ],
        out_shape=[jax.ShapeDtypeStruct((T, n_hg), F32),
                   jax.ShapeDtypeStruct((T, N - n_hg), BF16),
                   jax.ShapeDtypeStruct((T, LANES), F32)],
        scratch_shapes=[pltpu.VMEM((tm, D), BF16)],
        compiler_params=_cparams("arbitrary", "arbitrary"),
        name="inproj",
    )(x2d, g, w_main, w_gate)


def _hgrn_tables(C):
    t = np.arange(C)
    levels = []
    m = C // 2
    while m >= 1:
        levels.append(m)
        m //= 2
    rows = [np.tril(np.ones((C, C), np.float32)),
            np.triu(np.ones((C, C), np.float32), 1)]
    lev = np.full((C, C), -1, np.int32)
    for li, m in enumerate(levels):
        r = t % (2 * m)
        mid = t - r + m - 1
        M = np.zeros((C, C), np.float32)
        for i in range(C):
            if r[i] >= m:
                M[i, mid[i] + 1:i + 1] = 1.0
            else:
                M[i, i + 1:mid[i] + 1] = 1.0
        rows.append(M)
        same = (t[:, None] // (2 * m)) == (t[None, :] // (2 * m))
        lev[same & (r[:, None] >= m) & (r[None, :] < m)] = li
    lev[t[:, None] == t[None, :]] = len(levels)
    return np.concatenate(rows, 0), lev, len(levels)


HG_HEADS_PER_STEP = 8


def _hgrn_body(q_ref, f_ref, i_ref, g_ref, lb_ref, ng_ref, mst_ref, lev_ref, o_ref, st_ref,
               *, C, n_lev):
    @pl.when(pl.program_id(2) == 0)
    def _():
        st_ref[...] = jnp.zeros_like(st_ref)

    ng = ng_ref[...]
    mst = mst_ref[...]
    lev = lev_ref[...]
    n_chunks = q_ref.shape[0] // C

    def gates(rows, hb):
        cols = slice(hb * HG_DIM, (hb + 1) * HG_DIM)
        lb = lb_ref[hb]
        q = jax.nn.silu(q_ref[rows, cols])
        f = lb + (1.0 - lb) * jax.nn.sigmoid(f_ref[rows, cols])
        lf = jnp.log(f)
        lf0 = lf.astype(BF16)
        r1 = lf - lf0.astype(F32)
        lf1 = r1.astype(BF16)
        lf2 = (r1 - lf1.astype(F32)).astype(BF16)
        e = _dot(mst, jnp.concatenate([lf0, lf1, lf2], axis=0))
        return q, 1.0 - f, i_ref[rows, cols].astype(BF16), e

    def scores(q, k, gdec):
        a = jnp.where(lev == n_lev, _nt(q.astype(BF16), k.astype(BF16)), 0.0)
        for li in range(n_lev):
            g_m = gdec[(2 + li) * C:(3 + li) * C]
            a_m = _nt((q * g_m).astype(BF16), (k * g_m).astype(BF16))
            a = jnp.where(lev == li, a_m, a)
        return a.astype(BF16)

    def finish(rows, hb, q, k, v, gdec, a):
        cols = slice(hb * HG_DIM, (hb + 1) * HG_DIM)
        g_b = gdec[0:C]
        g_r = gdec[C:2 * C]
        st = st_ref[hb]
        o = _nt((q * g_b).astype(BF16), st.astype(BF16)) + _dot(a, v)
        g_last = g_b[C - 1:C, :]
        st_ref[hb] = st * g_last + _tn(v, (k * g_r).astype(BF16))
        o = o * lax.rsqrt(jnp.mean(o * o, axis=-1, keepdims=True) + RMS_EPS) * ng
        o_ref[rows, cols] = (o * jax.nn.silu(g_ref[rows, cols])).astype(o_ref.dtype)

    def chunk(c, carry):
        rows = pl.ds(pl.multiple_of(c * C, C), C)
        hs = range(HG_HEADS_PER_STEP)
        qkve = [gates(rows, hb) for hb in hs]
        gdec = [jnp.exp(x[3]) for x in qkve]
        a = [scores(qkve[hb][0], qkve[hb][1], gdec[hb]) for hb in hs]
        for hb in hs:
            finish(rows, hb, qkve[hb][0], qkve[hb][1], qkve[hb][2], gdec[hb], a[hb])
        return carry

    lax.fori_loop(0, n_chunks, chunk, 0)


def _hgrn(proj, lb, ng, B, S, tb):
    C = HG_CHUNK
    mst, lev, n_lev = _hgrn_tables(C)
    mst = np.concatenate([mst] * 3, axis=1)
    nb = S // tb
    HB = HG_HEADS_PER_STEP
    hsteps = HG_HEADS // HB

    def col(group):
        return pl.BlockSpec((tb, HB * HG_DIM), lambda b, h, t: (b * nb + t, group * hsteps + h))

    return pl.pallas_call(
        functools.partial(_hgrn_body, C=C, n_lev=n_lev),
        grid=(B, hsteps, nb),
        in_specs=[
            col(0), col(1), col(2), col(3),
            pl.BlockSpec((HB, 1, HG_DIM), lambda b, h, t: (h, 0, 0)),
            pl.BlockSpec((1, HG_DIM), lambda b, h, t: (0, 0)),
            pl.BlockSpec(mst.shape, lambda b, h, t: (0, 0)),
            pl.BlockSpec(lev.shape, lambda b, h, t: (0, 0)),
        ],
        out_specs=pl.BlockSpec((tb, HB * HG_DIM), lambda b, h, t: (b * nb + t, h)),
        out_shape=jax.ShapeDtypeStruct((B * S, HG_WIDTH), BF16),
        scratch_shapes=[pltpu.VMEM((HB, HG_DIM, HG_DIM), F32)],
        compiler_params=_cparams("arbitrary", "arbitrary", "arbitrary"),
        name="hgrn2",
    )(proj, proj, proj, proj, lb, ng, jnp.asarray(mst, BF16), jnp.asarray(lev))


def _compress_body(h_ref, pe_ref, w1_ref, w2_ref, o_ref):
    hv = h_ref[...]
    half = hv.shape[1]
    nh = hv.shape[0]
    u = _dot((hv + pe_ref[0:1, :]).astype(BF16), w1_ref[0:half, :].astype(BF16))
    v = _dot((hv + pe_ref[1:2, :]).astype(BF16), w1_ref[half:2 * half, :].astype(BF16))
    pre = u + pltpu.roll(v, nh - 1, axis=0)
    o_ref[...] = _dot(jax.nn.silu(pre).astype(BF16), w2_ref[...].astype(BF16)).astype(o_ref.dtype)


def _compress(halves, pe, w1, w2):
    _, B, G, NH, HW = halves.shape
    return pl.pallas_call(
        _compress_body,
        grid=(2, B, G),
        in_specs=[
            pl.BlockSpec((None, None, None, NH, HW), lambda s, b, g: (s, b, g, 0, 0)),
            pl.BlockSpec((None, 2, HW), lambda s, b, g: (s, 0, 0)),
            pl.BlockSpec((None, 2 * HW, CMP_HIDDEN), lambda s, b, g: (s, 0, 0)),
            pl.BlockSpec((None, CMP_HIDDEN, NSA_DH), lambda s, b, g: (s, 0, 0)),
        ],
        out_specs=pl.BlockSpec((None, None, None, NH, NSA_DH), lambda s, b, g: (s, b, g, 0, 0)),
        out_shape=jax.ShapeDtypeStruct((2, B, G, NH, NSA_DH), BF16),
        compiler_params=_cparams("arbitrary", "arbitrary", "arbitrary"),
        name="nsa_compress",
    )(halves, pe, w1, w2)


def _nsa_compress_all(proj, pe_k, w1_k, w2_k, pe_v, w1_v, w2_v, B, S):
    NH = S // CMP_STRIDE
    c0 = NSA_WIDTH

    def halves(idx):
        t = proj[:, c0 + idx * NSA_KV:c0 + (idx + 1) * NSA_KV]
        t = t.reshape(B, NH, CMP_STRIDE, NSA_GROUPS, NSA_DH).transpose(0, 3, 1, 2, 4)
        return t.reshape(B, NSA_GROUPS, NH, CMP_STRIDE * NSA_DH)

    hw = CMP_STRIDE * NSA_DH
    return _compress(jnp.stack([halves(0), halves(1)]),
                     jnp.stack([pe_k.reshape(2, hw), pe_v.reshape(2, hw)]),
                     jnp.stack([w1_k, w1_v]), jnp.stack([w2_k, w2_v]))


NSA_TQ = 128


NSA_MASK_ROWS = 64
NSA_WIN_MASK_ROWS = 16
NSA_FAR = 4
NSA_SLC_PAD = (NSA_FAR - 1) * NSA_TQ


NSA_V_ROWS = NSA_DH + 16


def _softmax_cols(s):
    m = jnp.max(s, axis=0, keepdims=True)
    return m, jnp.exp2(s - m).astype(BF16)


def _normalise(acc):
    return acc[0:NSA_DH] * (1.0 / acc[NSA_DH:NSA_DH + 1])


NSA_GROUPS_PER_STEP = 4


def _nsa_body(qt_ref, gl_ref, kc_ref, vct_ref, ks_ref, vst_ref, kw_ref, vwt_ref,
              bc_ref, bw_ref, bn_ref, ovt_ref, o_ref, *, top_k, n_slc):
    TQ = NSA_TQ
    GG = NSA_GROUPS_PER_STEP
    i = pl.program_id(2)
    ovt = ovt_ref[...]
    jb = lax.broadcasted_iota(jnp.int32, (n_slc, TQ), 0)
    qblk = (i * TQ + lax.broadcasted_iota(jnp.int32, (n_slc, TQ), 1)) // SLC_LEN
    forced = (jb == 0) | (jb == qblk) | (jb == qblk - 1)
    future = jb > qblk
    near = pl.ds(pl.multiple_of(NSA_SLC_PAD + (i - 1) * TQ, TQ), 2 * TQ)

    def heads(a):
        return jnp.concatenate([a] * NSA_HPG, axis=1)

    def queries(g):
        qt = qt_ref[g]
        return jnp.concatenate([qt[h * NSA_DH:(h + 1) * NSA_DH, :] for h in range(NSA_HPG)],
                               axis=1)

    def compressed(g, s):
        m = jnp.max(s, axis=0, keepdims=True)
        p = jnp.where(s > 0.5 * NEG_INF, jnp.exp2(s - m), 0.0)
        acc = _dot(vct_ref[g], p.astype(BF16))
        inv_l = 1.0 / jnp.maximum(acc[NSA_DH:NSA_DH + 1], 1e-30)
        o_cmp = acc[0:NSA_DH] * inv_l
        pc = p * inv_l
        ps = pc[:, 0:TQ] + pc[:, TQ:2 * TQ] + pc[:, 2 * TQ:3 * TQ] + pc[:, 3 * TQ:4 * TQ]
        ps_hi = ps.astype(BF16)
        ps_lo = (ps - ps_hi.astype(F32)).astype(BF16)
        imp = _dot(ovt, ps_hi) + _dot(ovt, ps_lo)
        return o_cmp, jnp.where(forced, BIG, jnp.where(future, -BIG, imp))

    def window(g, s):
        _, p = _softmax_cols(s)
        return _normalise(_dot(vwt_ref[g, :, win], p))

    def select(imp, q4):
        slabs = []
        for v in range(n_slc // 8):
            slab = imp[8 * v:8 * v + 8, :]
            jbs = jb[8 * v:8 * v + 8, :]
            cnt = jnp.zeros((8, TQ), F32)
            for jp in range(n_slc):
                row = imp[jp:jp + 1, :]
                if jp < 8 * v:
                    beats = row >= slab
                elif jp >= 8 * v + 8:
                    beats = row > slab
                else:
                    beats = (row > slab) | ((row == slab) & (jbs > jp))
                cnt = cnt + jnp.where(beats, 1.0, 0.0)
            slabs.append(jnp.where(cnt < top_k, 0.0, 1.0))
        if n_slc < NSA_MASK_ROWS:
            slabs.append(jnp.ones((NSA_MASK_ROWS - n_slc, TQ), F32))
        unsel = jnp.concatenate(slabs, axis=0)
        return jnp.concatenate([heads(unsel.astype(BF16)), q4], axis=0)

    def near_tiles(g, s):
        m, p = _softmax_cols(s)
        return m, _dot(vst_ref[g, :, near], p)

    win = pl.ds(pl.multiple_of(i * TQ, TQ), WIN + TQ)
    pad_row = jnp.where(lax.broadcasted_iota(jnp.int32, (NSA_WIN_MASK_ROWS, NSA_HPG * TQ), 0) == 0,
                        1.0, 0.0).astype(BF16)
    gs = range(GG)
    q4 = [queries(g) for g in gs]
    nh = kc_ref.shape[1]
    cmp_rows = pl.ds(pl.multiple_of(nh - (TQ // CMP_STRIDE) * i, TQ // CMP_STRIDE), nh)
    s_cmp = [_dot(kc_ref[g], q4[g]) + bc_ref[g, cmp_rows, :] for g in gs]
    s_win = [_dot(kw_ref[g, win, :], jnp.concatenate([pad_row, q4[g]], axis=0)) + bw_ref[g] for g in gs]
    cmp_out = [compressed(g, s_cmp[g]) for g in gs]
    o_win = [window(g, s_win[g]) for g in gs]
    qa = [select(cmp_out[g][1], q4[g]) for g in gs]
    s_near = [_dot(ks_ref[g, near, :], qa[g]) + bn_ref[g] for g in gs]
    start = tuple(near_tiles(g, s_near[g]) for g in gs)

    n_far = i - 1
    n_chunks = jnp.maximum((n_far + NSA_FAR - 1) // NSA_FAR, 0)
    far0 = NSA_SLC_PAD + TQ * (n_far - NSA_FAR * n_chunks)

    def far_step(c, carry):
        keys = pl.ds(pl.multiple_of(far0 + c * NSA_FAR * TQ, TQ), NSA_FAR * TQ)
        out = []
        logits = [_dot(ks_ref[g, keys, :], qa[g]) for g in gs]
        for g in gs:
            m, acc = carry[g]
            s = logits[g]
            m_new = jnp.maximum(m, jnp.max(s, axis=0, keepdims=True))
            p = jnp.exp2(s - m_new).astype(BF16)
            acc = jnp.exp2(m - m_new) * acc + _dot(vst_ref[g, :, keys], p)
            out.append((m_new, acc))
        return tuple(out)

    far = lax.fori_loop(0, n_chunks, far_step, start)

    for g in gs:
        o_slc = _normalise(far[g][1])
        o_cmp = cmp_out[g][0]
        sg = jax.nn.sigmoid(gl_ref[g])
        outs = []
        for h in range(NSA_HPG):
            lanes = slice(h * TQ, (h + 1) * TQ)
            outs.append(sg[h:h + 1, :] * o_cmp[:, lanes]
                        + sg[NSA_HPG + h:NSA_HPG + h + 1, :] * o_slc[:, lanes]
                        + sg[2 * NSA_HPG + h:2 * NSA_HPG + h + 1, :] * o_win[g][:, lanes])
        width = NSA_HPG * NSA_DH
        o_ref[:, g * width:(g + 1) * width] = jnp.concatenate(outs, axis=0).T.astype(o_ref.dtype)


def _rel_bucket(dist):
    n = jnp.maximum(dist, 0)
    max_exact = REL_BUCKETS // 2
    nf = jnp.maximum(n, 1).astype(F32)
    large = max_exact + (jnp.log(nf / max_exact) / math.log(REL_MAX_DIST / max_exact)
                         * (REL_BUCKETS - max_exact)).astype(jnp.int32)
    large = jnp.minimum(large, REL_BUCKETS - 1)
    return jnp.where(n < max_exact, n, large)


def _bias_table(table, dist, valid):
    onehot = (_rel_bucket(jnp.asarray(dist, jnp.int32))[..., None]
              == jnp.arange(REL_BUCKETS, dtype=jnp.int32)).astype(F32)
    b = jnp.einsum('rmk,ghk->ghrm', onehot, table, precision=lax.Precision.HIGHEST)
    return jnp.where(jnp.asarray(valid), b, NEG_INF)


def _relayout_body(x_ref, ms_ref, qt_ref, ks_ref, vst_ref, kw_ref, vwt_ref, *, lead):
    TQ = NSA_TQ
    real = pl.program_id(1) >= lead
    x = x_ref[...]

    def columns(idx):
        lo = NSA_WIDTH + idx * NSA_KV
        return jnp.where(real, x[:, lo:lo + NSA_KV], 0.0)

    @pl.when(real)
    def _():
        qt = x[:, 0:NSA_WIDTH].astype(F32).T.astype(BF16)
        qt_ref[...] = qt.reshape(qt_ref.shape)

    ones = jnp.where((lax.broadcasted_iota(jnp.int32, (NSA_V_ROWS - NSA_DH, TQ), 0) == 0) & real,
                     1.0, 0.0).astype(BF16)
    pad_col = jnp.where((lax.broadcasted_iota(jnp.int32, (TQ, NSA_WIN_MASK_ROWS), 1) == 0) & ~real,
                        NEG_INF, 0.0).astype(BF16)
    ms = ms_ref[...]
    k_slc, k_win = columns(2), columns(4)
    v_slc = columns(3).astype(F32).T.astype(BF16)
    v_win = columns(5).astype(F32).T.astype(BF16)
    for g in range(NSA_GROUPS):
        d = slice(g * NSA_DH, (g + 1) * NSA_DH)
        ks_ref[g] = jnp.concatenate([ms, k_slc[:, d]], axis=1)
        kw_ref[g] = jnp.concatenate([pad_col, k_win[:, d]], axis=1)
        vst_ref[g] = jnp.concatenate([v_slc[d, :], ones], axis=0)
        vwt_ref[g] = jnp.concatenate([v_win[d, :], ones], axis=0)


def _nsa_relayout(proj, slc_masks, B, S):
    TQ = NSA_TQ
    G, DH, SP = NSA_GROUPS, NSA_DH, NSA_SLC_PAD
    nt = S // TQ
    lead = WIN // TQ
    slc_lead = SP // TQ
    tok = lambda u: jnp.maximum(u - lead, 0)
    slc = lambda u: jnp.maximum(u - (lead - slc_lead), 0)
    return pl.pallas_call(
        functools.partial(_relayout_body, lead=lead),
        grid=(B, lead + nt),
        in_specs=[pl.BlockSpec((TQ, proj.shape[1]), lambda b, u: (b * nt + tok(u), 0)),
                  pl.BlockSpec((TQ, NSA_MASK_ROWS), lambda b, u: (slc(u), 0))],
        out_specs=[
            pl.BlockSpec((None, G, NSA_HPG * DH, TQ), lambda b, u: (b, 0, 0, tok(u))),
            pl.BlockSpec((None, G, TQ, NSA_MASK_ROWS + DH), lambda b, u: (b, 0, slc(u), 0)),
            pl.BlockSpec((None, G, NSA_V_ROWS, TQ), lambda b, u: (b, 0, 0, slc(u))),
            pl.BlockSpec((None, G, TQ, NSA_WIN_MASK_ROWS + DH), lambda b, u: (b, 0, u, 0)),
            pl.BlockSpec((None, G, NSA_V_ROWS, TQ), lambda b, u: (b, 0, 0, u)),
        ],
        out_shape=[
            jax.ShapeDtypeStruct((B, G, NSA_HPG * DH, S), BF16),
            jax.ShapeDtypeStruct((B, G, S + SP, NSA_MASK_ROWS + DH), BF16),
            jax.ShapeDtypeStruct((B, G, NSA_V_ROWS, S + SP), BF16),
            jax.ShapeDtypeStruct((B, G, S + WIN, NSA_WIN_MASK_ROWS + DH), BF16),
            jax.ShapeDtypeStruct((B, G, NSA_V_ROWS, S + WIN), BF16),
        ],
        compiler_params=_cparams("arbitrary", "arbitrary"),
        name="nsa_relayout",
    )(proj, slc_masks)


def _nsa(proj, glog, kc, vc, rel_bias, B, S):
    TQ = NSA_TQ
    G, HPG, DH = NSA_GROUPS, NSA_HPG, NSA_DH
    NH = S // CMP_STRIDE
    n_cmp = (S - CMP_LEN) // CMP_STRIDE + 1
    n_slc = S // SLC_LEN
    top_k = min(SLC_TOPK, n_slc)
    nt = S // TQ
    c0 = NSA_WIDTH

    MR = NSA_MASK_ROWS
    assert n_slc % 8 == 0 and (n_slc < MR or (n_slc == MR and top_k < n_slc))

    SP = NSA_SLC_PAD
    pos = np.arange(SP + S) - SP
    cols = np.zeros((SP + S, MR), np.float32)
    cols[:, :n_slc] = ((pos // SLC_LEN)[:, None] == np.arange(n_slc)[None, :]) & (pos >= 0)[:, None]
    if n_slc < MR:
        cols[:, n_slc] = pos < 0
    else:
        cols[pos < 0, :] = 1.0
    qt, ks, vst, kw, vwt = _nsa_relayout(proj, jnp.asarray(cols * NEG_INF, BF16), B, S)
    kcb = kc
    ones = jnp.zeros((NSA_V_ROWS - DH, NH), BF16).at[0].set(1.0)
    vct = jnp.concatenate([vc.transpose(0, 1, 3, 2), jnp.broadcast_to(ones, (B, G) + ones.shape)],
                          axis=2)

    table = rel_bias.T.reshape(G, HPG, REL_BUCKETS) * LOG2E
    far = table[:, :, REL_BUCKETS - 1]
    r = np.arange(TQ)[:, None]

    def keys_major(b):
        return b.transpose(0, 3, 1, 2).reshape(G, b.shape[3], HPG * TQ)

    d_c = r - (CMP_LEN - 1) - CMP_STRIDE * (np.arange(2 * NH)[None, :] - NH)
    bias_cmp = keys_major(_bias_table(table, d_c, d_c >= 0))
    d_w = r + WIN - np.arange(WIN + TQ)[None, :]
    bias_win = keys_major(_bias_table(table, d_w, (d_w >= 0) & (d_w < WIN)))
    d_n = r + TQ - np.arange(2 * TQ)[None, :]
    bias_near = keys_major(_bias_table(table - far[:, :, None], d_n, d_n >= 0))

    ci = np.arange(NH)[None, :] * CMP_STRIDE
    sj = np.arange(n_slc)[:, None] * SLC_LEN
    overlap_t = ((ci <= sj + SLC_LEN - 1) & (ci + CMP_LEN - 1 >= sj) & (np.arange(NH)[None, :] < n_cmp))

    gate_rows = 16
    glog_g = glog[:, :GATE_COLS].reshape(B * S, 3, G, HPG).transpose(2, 1, 3, 0).reshape(G, 3 * HPG, B * S)
    glog_g = jnp.pad(glog_g, ((0, 0), (0, gate_rows - 3 * HPG), (0, 0)))

    GG = NSA_GROUPS_PER_STEP
    whole = lambda shape: pl.BlockSpec((None, GG) + shape, lambda b, g, i: (b, g, 0, 0))
    tile = lambda k: pl.BlockSpec((GG, k, HPG * TQ), lambda b, g, i: (g, 0, 0))
    return pl.pallas_call(
        functools.partial(_nsa_body, top_k=top_k, n_slc=n_slc),
        grid=(B, G // GG, nt),
        in_specs=[
            pl.BlockSpec((None, GG, HPG * DH, TQ), lambda b, g, i: (b, g, 0, i)),
            pl.BlockSpec((GG, gate_rows, TQ), lambda b, g, i: (g, 0, b * nt + i)),
            whole((NH, DH)), whole((NSA_V_ROWS, NH)),
            whole((S + SP, MR + DH)), whole((NSA_V_ROWS, S + SP)),
            whole((S + WIN, NSA_WIN_MASK_ROWS + DH)), whole((NSA_V_ROWS, S + WIN)),
            tile(2 * NH), tile(WIN + TQ), tile(2 * TQ),
            pl.BlockSpec((n_slc, NH), lambda b, g, i: (0, 0)),
        ],
        out_specs=pl.BlockSpec((TQ, GG * HPG * DH), lambda b, g, i: (b * nt + i, g)),
        out_shape=jax.ShapeDtypeStruct((B * S, NSA_WIDTH), BF16),
        compiler_params=_cparams("arbitrary", "arbitrary", "arbitrary"),
        name="nsa_attention",
    )(qt, glog_g, kcb, vct, ks, vst, kw, vwt, bias_cmp, bias_win, bias_near,
      jnp.asarray(overlap_t, BF16))


ROUTE_E1, ROUTE_E2, ROUTE_W1, ROUTE_W2 = 0, 1, 2, 3


def _outproj_body(x_ref, yh_ref, yn_ref, wo_ref, g2_ref, wrh_ref, wrl_ref, br_ref,
                  x2_ref, h2_ref, rt_ref):
    x2 = (x_ref[...] + _dot(yh_ref[...], wo_ref[0:HG_WIDTH, :])
          + _dot(yn_ref[...], wo_ref[HG_WIDTH:HG_WIDTH + NSA_WIDTH, :]))
    x2_ref[...] = x2
    h = x2 * lax.rsqrt(jnp.mean(x2 * x2, axis=-1, keepdims=True) + RMS_EPS) * g2_ref[...]
    hb = h.astype(BF16)
    hbf = hb.astype(F32)
    half = h.shape[1] // 2
    bits = pltpu.bitcast(hbf, jnp.int32)
    h2_ref[...] = (lax.shift_right_logical(bits[:, :half], 16)
                   | (bits[:, half:] & jnp.int32(-65536)))
    hl = (h - hbf).astype(BF16)
    wrh = wrh_ref[...]
    lg = _dot(hb, wrh) + _dot(hl, wrh) + _dot(hb, wrl_ref[...]) + br_ref[...]
    lane = lax.broadcasted_iota(jnp.int32, lg.shape, 1)
    first = lambda hit: jnp.min(jnp.where(hit, lane, LANES), axis=-1, keepdims=True)
    lgg = jnp.where(lane < N_EXPERT_GROUPS, lg, NEG_INF)
    mg = jnp.max(lgg, axis=-1, keepdims=True)
    p_top = 1.0 / jnp.sum(jnp.exp(lgg - mg), axis=-1, keepdims=True)
    lo = N_EXPERT_GROUPS + EXPERTS_PER_GROUP * first(lgg == mg)
    le = jnp.where((lane >= lo) & (lane < lo + EXPERTS_PER_GROUP), lg, NEG_INF)
    v1 = jnp.max(le, axis=-1, keepdims=True)
    i1 = first(le == v1)
    le = jnp.where(lane == i1, NEG_INF, le)
    v2 = jnp.max(le, axis=-1, keepdims=True)
    i2 = first(le == v2)
    e21 = jnp.exp(v2 - v1)
    w1 = p_top / (1.0 + e21)
    rt = jnp.where(lane == ROUTE_E1, (i1 - N_EXPERT_GROUPS).astype(F32), 0.0)
    rt = jnp.where(lane == ROUTE_E2, (i2 - N_EXPERT_GROUPS).astype(F32), rt)
    rt = jnp.where(lane == ROUTE_W1, w1, rt)
    rt_ref[...] = jnp.where(lane == ROUTE_W2, w1 * e21, rt)


def _outproj(x2d, y_hg, y_nsa, w_out, g2, wr_hi, wr_lo, b_r, tm):
    T, D = x2d.shape
    row = lambda n: pl.BlockSpec((tm, n), lambda i: (i, 0))
    full = lambda a: pl.BlockSpec(a.shape, lambda i: (0, 0))
    return pl.pallas_call(
        _outproj_body,
        grid=(T // tm,),
        in_specs=[row(D), row(HG_WIDTH), row(NSA_WIDTH), full(w_out), full(g2),
                  full(wr_hi), full(wr_lo), full(b_r)],
        out_specs=[row(D), row(D // 2), row(LANES)],
        out_shape=[jax.ShapeDtypeStruct((T, D), F32), jax.ShapeDtypeStruct((T, D // 2), jnp.int32),
                   jax.ShapeDtypeStruct((T, LANES), F32)],
        compiler_params=_cparams("arbitrary"),
        name="outproj_router",
    )(x2d, y_hg, y_nsa, w_out, g2, wr_hi, wr_lo, b_r)


ROW_TILE = 256


def _dispatch_body(dest_ref, h_ref, xs0_hbm, xs_hbm, sem):
    del xs0_hbm
    tb = h_ref.shape[0]

    def start(t, carry):
        for k in range(2):
            pltpu.make_async_copy(h_ref.at[pl.ds(t, 1), :],
                                  xs_hbm.at[pl.ds(dest_ref[0, 2 * t + k], 1), :], sem).start()
        return carry

    lax.fori_loop(0, tb, start, 0, unroll=8)
    for _ in range(2):
        pltpu.make_async_copy(h_ref, xs_hbm.at[pl.ds(0, tb), :], sem).wait()


def _dispatch(h2p, dest, n_rows):
    T, W = h2p.shape
    tb = ROW_TILE
    return pl.pallas_call(
        _dispatch_body,
        grid=(T // tb,),
        in_specs=[pl.BlockSpec((None, 1, 2 * tb), lambda i: (i, 0, 0), memory_space=pltpu.SMEM),
                  pl.BlockSpec((tb, W), lambda i: (i, 0)),
                  pl.BlockSpec(memory_space=pl.ANY)],
        out_specs=pl.BlockSpec(memory_space=pl.ANY),
        out_shape=jax.ShapeDtypeStruct((n_rows, W), jnp.int32),
        scratch_shapes=[pltpu.SemaphoreType.DMA(())],
        input_output_aliases={2: 0},
        compiler_params=_cparams("arbitrary"),
        name="moe_dispatch",
    )(dest.reshape(T // tb, 1, 2 * tb), h2p, jnp.zeros((n_rows, W), jnp.int32))


MOE_ALIGN = 128
MOE_TM = 1024
MOE_SUB = 256
MOE_TF = 256
MOE_CAST_PIECES = 4


def _moe_body(ie_ref, ir_ref, in_ref, nb_ref, xs_hbm, wg_ref, wu_ref, wd_ref, y_hbm,
              xbuf, xlo, xhi, acc, wgb, wub, wdb, sem_in, sem_out):
    del ie_ref
    w = pl.program_id(0)
    f = pl.program_id(1)
    n_w = pl.num_programs(0)
    n_f = pl.num_programs(1)
    last_f = n_f - 1
    nsub = in_ref[w]
    row0 = pl.multiple_of(ir_ref[w] * MOE_ALIGN, MOE_ALIGN)
    slot = w % 2
    per_mm = MOE_SUB // MOE_ALIGN
    n_full = nsub // per_mm
    n_iter = n_full + nsub % per_mm
    half = xlo.shape[1]
    cur = (w * n_f + f) % 2
    nxt_item = jnp.minimum(jnp.where(f == last_f, w + 1, w), n_w - 1)
    cast_next = ((f < last_f) | (w + 1 < n_w)) & (in_ref[nxt_item] > 0)

    def cast_piece(p):
        rk = wg_ref.shape[0] // MOE_CAST_PIECES
        rows = pl.ds(pl.multiple_of(p * rk, rk), rk)
        wgb[1 - cur, rows, :] = wg_ref[rows, :].astype(BF16)
        wub[1 - cur, rows, :] = wu_ref[rows, :].astype(BF16)
        rf = wd_ref.shape[0] // MOE_CAST_PIECES
        rows = pl.ds(pl.multiple_of(p * rf, rf), rf)
        wdb[1 - cur, rows, :] = wd_ref[rows, :].astype(BF16)

    def x_copy(item, s):
        r = pl.multiple_of(ir_ref[item] * MOE_ALIGN, MOE_ALIGN)
        return pltpu.make_async_copy(xs_hbm.at[pl.ds(r, MOE_TM), :], xbuf.at[s], sem_in.at[s])

    def y_copy(r, dst_row):
        return pltpu.make_async_copy(acc.at[pl.ds(r, MOE_ALIGN), :],
                                     y_hbm.at[pl.ds(dst_row, MOE_ALIGN), :], sem_out)

    def for_row_blocks(fn):
        def body(j, carry):
            fn(pl.multiple_of(j * MOE_SUB, MOE_SUB), MOE_SUB, j)
            return carry

        lax.fori_loop(0, n_full, body, 0)

        @pl.when(nsub % per_mm == 1)
        def _():
            fn(pl.multiple_of(n_full * MOE_SUB, MOE_ALIGN), MOE_ALIGN, n_full)

    @pl.when((f == 0) & (nsub > 0))
    def _():
        @pl.when(in_ref[jnp.maximum(w - 1, 0)] == 0)
        def _():
            x_copy(w, slot).start()

        x_copy(w, slot).wait()

        def unpack(r, size, _):
            bits = xbuf[slot, pl.ds(r, size), :]
            xlo[pl.ds(r, size), :] = pltpu.bitcast(lax.shift_left(bits, 16), F32).astype(BF16)
            xhi[pl.ds(r, size), :] = pltpu.bitcast(bits & jnp.int32(-65536), F32).astype(BF16)

        for_row_blocks(unpack)

    def experts(first, last):
        def block(r, size, it):
            rows = pl.ds(r, size)
            lo = xlo[rows, :]
            hi = xhi[rows, :]
            hg = _dot(lo, wgb[cur, 0:half, :]) + _dot(hi, wgb[cur, half:2 * half, :])
            hu = _dot(lo, wub[cur, 0:half, :]) + _dot(hi, wub[cur, half:2 * half, :])
            y = _dot((jax.nn.silu(hg) * hu).astype(BF16), wdb[cur])
            if first:
                acc[rows, :] = y
            else:
                acc[rows, :] += y
            if last:
                for k in range(size // MOE_ALIGN):
                    y_copy(r + k * MOE_ALIGN, row0 + r + k * MOE_ALIGN).start()

            @pl.when(cast_next & (it < MOE_CAST_PIECES))
            def _():
                cast_piece(it)

        for_row_blocks(block)

        @pl.when(cast_next)
        def _():
            lax.fori_loop(jnp.minimum(n_iter, MOE_CAST_PIECES), MOE_CAST_PIECES,
                          lambda p, c: (cast_piece(p), c)[1], 0)

    @pl.when((nsub == 0) & cast_next)
    def _():
        lax.fori_loop(0, MOE_CAST_PIECES, lambda p, c: (cast_piece(p), c)[1], 0)

    @pl.when((f == 0) & (nsub > 0))
    def _():
        experts(True, False)

    @pl.when((f > 0) & (f < last_f) & (nsub > 0))
    def _():
        experts(False, False)

    @pl.when((f == last_f) & (nsub > 0))
    def _():
        nxt = jnp.minimum(w + 1, n_w - 1)

        @pl.when((w + 1 < n_w) & (in_ref[nxt] > 0))
        def _():
            x_copy(nxt, 1 - slot).start()

        experts(False, True)

        def wait(j, carry):
            y_copy(0, 0).wait()
            return carry

        lax.fori_loop(0, nsub, wait, 0)

    @pl.when((f == last_f) & (w == n_w - 1))
    def _():
        acc[0:MOE_ALIGN, :] = jnp.zeros((MOE_ALIGN, acc.shape[1]), F32)
        used = nb_ref[0]
        total = y_hbm.shape[0] // MOE_ALIGN

        def fill(j, carry):
            y_copy(0, pl.multiple_of(j * MOE_ALIGN, MOE_ALIGN)).start()
            return carry

        def wait(j, carry):
            y_copy(0, 0).wait()
            return carry

        lax.fori_loop(used, total, fill, 0)
        lax.fori_loop(used, total, wait, 0)


def _moe(xs, item_e, item_r, item_n, n_blocks, w_gate, w_up, w_down):
    rows = xs.shape[0]
    D = 2 * xs.shape[1]
    n_items = item_e.shape[0]
    nf = EXPERT_DFF // MOE_TF
    assert nf >= 2 and MOE_SUB == 2 * MOE_ALIGN

    def nxt(w, f, ie, nn):
        w1 = jnp.minimum(jnp.where(f == nf - 1, w + 1, w), n_items - 1)
        f1 = jnp.where(f == nf - 1, 0, f + 1)
        return ie[w1], jnp.where(nn[w1] > 0, f1, nf - 1)

    def w_in_map(w, f, ie, ir, nn, nb):
        e, t = nxt(w, f, ie, nn)
        return e, 0, t

    def w_out_map(w, f, ie, ir, nn, nb):
        e, t = nxt(w, f, ie, nn)
        return e, t, 0

    return pl.pallas_call(
        _moe_body,
        grid_spec=pltpu.PrefetchScalarGridSpec(
            num_scalar_prefetch=4,
            grid=(n_items, nf),
            in_specs=[
                pl.BlockSpec(memory_space=pl.ANY),
                pl.BlockSpec((None, D, MOE_TF), w_in_map),
                pl.BlockSpec((None, D, MOE_TF), w_in_map),
                pl.BlockSpec((None, MOE_TF, D), w_out_map),
            ],
            out_specs=pl.BlockSpec(memory_space=pl.ANY),
            scratch_shapes=[pltpu.VMEM((2, MOE_TM, D // 2), jnp.int32),
                            pltpu.VMEM((MOE_TM, D // 2), BF16), pltpu.VMEM((MOE_TM, D // 2), BF16),
                            pltpu.VMEM((MOE_TM, D), F32),
                            pltpu.VMEM((2, D, MOE_TF), BF16), pltpu.VMEM((2, D, MOE_TF), BF16),
                            pltpu.VMEM((2, MOE_TF, D), BF16),
                            pltpu.SemaphoreType.DMA((2,)), pltpu.SemaphoreType.DMA(())],
        ),
        out_shape=jax.ShapeDtypeStruct((rows, D), F32),
        compiler_params=_cparams("arbitrary", "arbitrary"),
        name="moe_experts",
    )(item_e, item_r, item_n, n_blocks, xs, w_gate, w_up, w_down)


def _final_body(dcur_ref, dnext_ref, x2_ref, rt_ref, g_ref, ys_hbm, o_ref, ybuf, sems):
    i = pl.program_id(0)
    tb = x2_ref.shape[0]
    slot = i % 2

    def gather(d_ref, s):
        def body(t, carry):
            for k in range(2):
                pltpu.make_async_copy(ys_hbm.at[pl.ds(d_ref[0, 2 * t + k], 1), :],
                                      ybuf.at[s, k, pl.ds(t, 1), :], sems.at[s]).start()
            return carry

        lax.fori_loop(0, tb, body, 0, unroll=8)

    @pl.when(i == 0)
    def _():
        gather(dcur_ref, 0)

    @pl.when(i + 1 < pl.num_programs(0))
    def _():
        gather(dnext_ref, 1 - slot)

    for k in range(2):
        pltpu.make_async_copy(ys_hbm.at[pl.ds(0, tb), :], ybuf.at[slot, k], sems.at[slot]).wait()
    rt = rt_ref[...]
    x = (x2_ref[...] + rt[:, ROUTE_W1:ROUTE_W1 + 1] * ybuf[slot, 0]
         + rt[:, ROUTE_W2:ROUTE_W2 + 1] * ybuf[slot, 1])
    o_ref[...] = x * lax.rsqrt(jnp.mean(x * x, axis=-1, keepdims=True) + RMS_EPS) * g_ref[...]


def _final(x2, ys, dest, rt, g):
    T, D = x2.shape
    tb = ROW_TILE
    nb = T // tb
    dest3 = dest.reshape(nb, 1, 2 * tb)
    return pl.pallas_call(
        _final_body,
        grid=(nb,),
        in_specs=[pl.BlockSpec((None, 1, 2 * tb), lambda i: (i, 0, 0), memory_space=pltpu.SMEM),
                  pl.BlockSpec((None, 1, 2 * tb), lambda i: (jnp.minimum(i + 1, nb - 1), 0, 0),
                               memory_space=pltpu.SMEM),
                  pl.BlockSpec((tb, D), lambda i: (i, 0)),
                  pl.BlockSpec((tb, LANES), lambda i: (i, 0)),
                  pl.BlockSpec((1, D), lambda i: (0, 0)),
                  pl.BlockSpec(memory_space=pl.ANY)],
        out_specs=pl.BlockSpec((tb, D), lambda i: (i, 0)),
        out_shape=jax.ShapeDtypeStruct((T, D), F32),
        scratch_shapes=[pltpu.VMEM((2, 2, tb, D), F32), pltpu.SemaphoreType.DMA((2,))],
        compiler_params=_cparams("arbitrary"),
        name="combine_final_norm",
    )(dest3, dest3, x2, rt, g, ys)


def _moe_layout(e_flat):
    A = e_flat.shape[0]
    onehot = (e_flat[:, None] == jnp.arange(N_EXPERTS, dtype=jnp.int32)[None, :]).astype(jnp.int32)
    csum = jnp.cumsum(onehot, axis=0)
    counts = csum[-1]
    nblk = (counts + MOE_ALIGN - 1) // MOE_ALIGN
    blk0 = jnp.cumsum(nblk) - nblk
    dest = jnp.sum(onehot * (blk0[None, :] * MOE_ALIGN + csum - 1), axis=1).astype(jnp.int32)
    per_item = MOE_TM // MOE_ALIGN
    nitem = (nblk + per_item - 1) // per_item
    iend = jnp.cumsum(nitem)
    n_items = (A // MOE_ALIGN + N_EXPERTS + N_EXPERTS * (per_item - 1)) // per_item
    w = jnp.arange(n_items, dtype=jnp.int32)
    total = iend[-1]
    wv = jnp.minimum(w, total - 1)
    item_e = jnp.minimum(jnp.searchsorted(iend, wv, side='right'), N_EXPERTS - 1).astype(jnp.int32)
    sb = wv - (iend - nitem)[item_e]
    item_r = (blk0[item_e] + sb * per_item).astype(jnp.int32)
    item_n = jnp.where(w < total, jnp.minimum(per_item, nblk[item_e] - sb * per_item), 0).astype(jnp.int32)
    n_blocks = jnp.sum(nblk).astype(jnp.int32).reshape(1)
    lead = lambda a, v: jnp.concatenate([jnp.asarray(v, jnp.int32).reshape(1), a])
    return dest, lead(item_e, item_e[0]), lead(item_r, 0), lead(item_n, 0), n_blocks


def kernel(x, norm1_g, w_in, hg_lb_logits, hg_norm_g, cmp_pe_k, cmp_w1_k, cmp_w2_k, cmp_pe_v,
           cmp_w1_v, cmp_w2_v, rel_bias, w_out, norm2_g, w_router_group, b_router_group,
           w_router_expert, b_router_expert, w_expert_gate, w_expert_up, w_expert_down,
           final_norm_g):
    B, S, D = x.shape
    T = B * S
    assert w_in.shape[0] == 1, "single-layer block"
    x2d = x.reshape(T, D)

    lower = jax.nn.softmax(hg_lb_logits.astype(F32), axis=0)[0].reshape(HG_HEADS, 1, HG_DIM)

    w_main = w_in[0, :, :MAIN_COLS].astype(BF16)
    w_gate_cols = jnp.pad(w_in[0, :, MAIN_COLS:].astype(BF16), ((0, 0), (0, LANES - GATE_COLS)))
    proj_hg, proj_nsa, glog = _inproj(x2d, norm1_g[0].reshape(1, D), w_main, w_gate_cols,
                                      tm=1024, tn=512)

    y_hg = _hgrn(proj_hg, lower, hg_norm_g[0].reshape(1, HG_DIM), B, S, tb=512)
    kcvc = _nsa_compress_all(proj_nsa, cmp_pe_k[0], cmp_w1_k[0], cmp_w2_k[0],
                             cmp_pe_v[0], cmp_w1_v[0], cmp_w2_v[0], B, S)
    y_nsa = _nsa(proj_nsa, glog, kcvc[0], kcvc[1], rel_bias, B, S)

    w_r = jnp.concatenate([w_router_group[0], w_router_expert[0]], axis=1)
    w_r = jnp.pad(w_r, ((0, 0), (0, LANES - w_r.shape[1])))
    wr_hi = w_r.astype(BF16)
    wr_lo = (w_r - wr_hi.astype(F32)).astype(BF16)
    b_r = jnp.concatenate([b_router_group[0], b_router_expert[0]])
    b_r = jnp.pad(b_r, (0, LANES - b_r.shape[0])).reshape(1, LANES)
    x2, h2, rt = _outproj(x2d, y_hg, y_nsa, w_out[0].astype(BF16), norm2_g[0].reshape(1, D),
                          wr_hi, wr_lo, b_r, tm=512)

    e_flat = rt[:, ROUTE_E1:ROUTE_E2 + 1].astype(jnp.int32).reshape(2 * T)
    dest, item_e, item_r, item_n, n_blocks = _moe_layout(e_flat)
    n_rows = (2 * T // MOE_ALIGN + N_EXPERTS) * MOE_ALIGN + MOE_TM
    xs = _dispatch(h2, dest, n_rows)
    ys = _moe(xs, item_e, item_r, item_n, n_blocks,
              w_expert_gate[0], w_expert_up[0], w_expert_down[0])

    out = _final(x2, ys, dest, rt, final_norm_g.reshape(1, D))
    return out.reshape(B, S, D)
```

```python
import functools
import math

import jax
import jax.numpy as jnp
import numpy as np
from jax import lax
from jax.experimental import pallas as pl
from jax.experimental.pallas import tpu as pltpu

F32 = jnp.float32
BF16 = jnp.bfloat16

D_MODEL = 2048
HG_HEADS = 8
HG_DIM = 128
HG_WIDTH = HG_HEADS * HG_DIM
HG_CHUNK = 64
NSA_HEADS = 16
NSA_GROUPS = 4
NSA_HPG = NSA_HEADS // NSA_GROUPS
NSA_DH = 64
NSA_WIDTH = NSA_HEADS * NSA_DH
NSA_KV = NSA_GROUPS * NSA_DH
CMP_LEN = 32
CMP_STRIDE = 16
CMP_HIDDEN = 256
SLC_LEN = 64
SLC_TOPK = 16
WIN = 512
REL_BUCKETS = 32
REL_MAX_DIST = 128
N_EXPERT_GROUPS = 8
EXPERTS_PER_GROUP = 8
N_EXPERTS = N_EXPERT_GROUPS * EXPERTS_PER_GROUP
EXPERT_DFF = 1024
RMS_EPS = 1e-6
NEG_INF = -1e30
LOG2E = math.log2(math.e)
NSA_Q_SCALE = NSA_DH ** -0.5 * LOG2E
BIG = 1e9

MAIN_COLS = 4 * HG_WIDTH + NSA_WIDTH + 6 * NSA_KV
GATE_COLS = 3 * NSA_HEADS
LANES = 128
VMEM_LIMIT = 56 * 1024 * 1024


def _cparams(*sem):
    return pltpu.CompilerParams(dimension_semantics=sem, vmem_limit_bytes=VMEM_LIMIT)


def _nt(a, b):
    return lax.dot_general(a, b, (((1,), (1,)), ((), ())), preferred_element_type=F32)


def _tn(a, b):
    return lax.dot_general(a, b, (((0,), (0,)), ((), ())), preferred_element_type=F32)


def _dot(a, b):
    return jnp.dot(a, b, preferred_element_type=F32)


def _inproj_body(x_ref, g_ref, w_ref, wg_ref, oh_ref, on_ref, og_ref, h_scr, *, hg_tiles, q_tiles):
    j = pl.program_id(1)

    @pl.when(j == 0)
    def _():
        x = x_ref[...]
        y = x * lax.rsqrt(jnp.mean(x * x, axis=-1, keepdims=True) + RMS_EPS) * g_ref[...]
        h_scr[...] = y.astype(BF16)
        og_ref[...] = _dot(h_scr[...], wg_ref[...])

    @pl.when(j < hg_tiles)
    def _():
        oh_ref[...] = _dot(h_scr[...], w_ref[...])

    @pl.when(j >= hg_tiles)
    def _():
        scale = jnp.where(j < hg_tiles + q_tiles, NSA_Q_SCALE, 1.0)
        on_ref[...] = (_dot(h_scr[...], w_ref[...]) * scale).astype(on_ref.dtype)


def _inproj(x2d, g, w_main, w_gate, tm, tn):
    T, D = x2d.shape
    N = w_main.shape[1]
    n_hg = 4 * HG_WIDTH
    hg_tiles = n_hg // tn
    return pl.pallas_call(
        functools.partial(_inproj_body, hg_tiles=hg_tiles, q_tiles=NSA_WIDTH // tn),
        grid=(T // tm, N // tn),
        in_specs=[
            pl.BlockSpec((tm, D), lambda i, j: (i, 0)),
            pl.BlockSpec((1, D), lambda i, j: (0, 0)),
            pl.BlockSpec((D, tn), lambda i, j: (0, j)),
            pl.BlockSpec((D, LANES), lambda i, j: (0, 0)),
        ],
        out_specs=[
            pl.BlockSpec((tm, tn), lambda i, j: (i, jnp.minimum(j, hg_tiles - 1))),
            pl.BlockSpec((tm, tn), lambda i, j: (i, jnp.maximum(j - hg_tiles, 0))),
            pl.BlockSpec((tm, LANES), lambda i, j: (i, 0)),
        ],
        out_shape=[jax.ShapeDtypeStruct((T, n_hg), F32),
                   jax.ShapeDtypeStruct((T, N - n_hg), BF16),
                   jax.ShapeDtypeStruct((T, LANES), F32)],
        scratch_shapes=[pltpu.VMEM((tm, D), BF16)],
        compiler_params=_cparams("arbitrary", "arbitrary"),
        name="inproj",
    )(x2d, g, w_main, w_gate)


def _hgrn_tables(C):
    t = np.arange(C)
    levels = []
    m = C // 2
    while m >= 1:
        levels.append(m)
        m //= 2
    rows = [np.tril(np.ones((C, C), np.float32)),
            np.triu(np.ones((C, C), np.float32), 1)]
    lev = np.full((C, C), -1, np.int32)
    for li, m in enumerate(levels):
        r = t % (2 * m)
        mid = t - r + m - 1
        M = np.zeros((C, C), np.float32)
        for i in range(C):
            if r[i] >= m:
                M[i, mid[i] + 1:i + 1] = 1.0
            else:
                M[i, i + 1:mid[i] + 1] = 1.0
        rows.append(M)
        same = (t[:, None] // (2 * m)) == (t[None, :] // (2 * m))
        lev[same & (r[:, None] >= m) & (r[None, :] < m)] = li
    lev[t[:, None] == t[None, :]] = len(levels)
    return np.concatenate(rows, 0), lev, len(levels)


HG_HEADS_PER_STEP = 8


def _hgrn_body(q_ref, f_ref, i_ref, g_ref, lb_ref, ng_ref, mst_ref, lev_ref, o_ref, st_ref,
               *, C, n_lev):
    @pl.when(pl.program_id(2) == 0)
    def _():
        st_ref[...] = jnp.zeros_like(st_ref)

    ng = ng_ref[...]
    mst = mst_ref[...]
    lev = lev_ref[...]
    n_chunks = q_ref.shape[0] // C

    def gates(rows, hb):
        cols = slice(hb * HG_DIM, (hb + 1) * HG_DIM)
        lb = lb_ref[hb]
        q = jax.nn.silu(q_ref[rows, cols])
        f = lb + (1.0 - lb) * jax.nn.sigmoid(f_ref[rows, cols])
        lf = jnp.log(f)
        lf0 = lf.astype(BF16)
        r1 = lf - lf0.astype(F32)
        lf1 = r1.astype(BF16)
        lf2 = (r1 - lf1.astype(F32)).astype(BF16)
        e = _dot(mst, jnp.concatenate([lf0, lf1, lf2], axis=0))
        return q, 1.0 - f, i_ref[rows, cols].astype(BF16), e

    def scores(q, k, gdec):
        a = jnp.where(lev == n_lev, _nt(q.astype(BF16), k.astype(BF16)), 0.0)
        for li in range(n_lev):
            g_m = gdec[(2 + li) * C:(3 + li) * C]
            a_m = _nt((q * g_m).astype(BF16), (k * g_m).astype(BF16))
            a = jnp.where(lev == li, a_m, a)
        return a.astype(BF16)

    def finish(rows, hb, q, k, v, gdec, a):
        cols = slice(hb * HG_DIM, (hb + 1) * HG_DIM)
        g_b = gdec[0:C]
        g_r = gdec[C:2 * C]
        st = st_ref[hb]
        o = _nt((q * g_b).astype(BF16), st.astype(BF16)) + _dot(a, v)
        g_last = g_b[C - 1:C, :]
        st_ref[hb] = st * g_last + _tn(v, (k * g_r).astype(BF16))
        o = o * lax.rsqrt(jnp.mean(o * o, axis=-1, keepdims=True) + RMS_EPS) * ng
        o_ref[rows, cols] = (o * jax.nn.silu(g_ref[rows, cols])).astype(o_ref.dtype)

    def chunk(c, carry):
        rows = pl.ds(pl.multiple_of(c * C, C), C)
        hs = range(HG_HEADS_PER_STEP)
        qkve = [gates(rows, hb) for hb in hs]
        gdec = [jnp.exp(x[3]) for x in qkve]
        a = [scores(qkve[hb][0], qkve[hb][1], gdec[hb]) for hb in hs]
        for hb in hs:
            finish(rows, hb, qkve[hb][0], qkve[hb][1], qkve[hb][2], gdec[hb], a[hb])
        return carry

    lax.fori_loop(0, n_chunks, chunk, 0)


def _hgrn(proj, lb, ng, B, S, tb):
    C = HG_CHUNK
    mst, lev, n_lev = _hgrn_tables(C)
    mst = np.concatenate([mst] * 3, axis=1)
    nb = S // tb
    HB = HG_HEADS_PER_STEP
    hsteps = HG_HEADS // HB

    def col(group):
        return pl.BlockSpec((tb, HB * HG_DIM), lambda b, h, t: (b * nb + t, group * hsteps + h))

    return pl.pallas_call(
        functools.partial(_hgrn_body, C=C, n_lev=n_lev),
        grid=(B, hsteps, nb),
        in_specs=[
            col(0), col(1), col(2), col(3),
            pl.BlockSpec((HB, 1, HG_DIM), lambda b, h, t: (h, 0, 0)),
            pl.BlockSpec((1, HG_DIM), lambda b, h, t: (0, 0)),
            pl.BlockSpec(mst.shape, lambda b, h, t: (0, 0)),
            pl.BlockSpec(lev.shape, lambda b, h, t: (0, 0)),
        ],
        out_specs=pl.BlockSpec((tb, HB * HG_DIM), lambda b, h, t: (b * nb + t, h)),
        out_shape=jax.ShapeDtypeStruct((B * S, HG_WIDTH), BF16),
        scratch_shapes=[pltpu.VMEM((HB, HG_DIM, HG_DIM), F32)],
        compiler_params=_cparams("arbitrary", "arbitrary", "arbitrary"),
        name="hgrn2",
    )(proj, proj, proj, proj, lb, ng, jnp.asarray(mst, BF16), jnp.asarray(lev))


def _compress_body(h_ref, pe_ref, w1_ref, w2_ref, o_ref):
    hv = h_ref[...]
    half = hv.shape[1]
    nh = hv.shape[0]
    u = _dot((hv + pe_ref[0:1, :]).astype(BF16), w1_ref[0:half, :].astype(BF16))
    v = _dot((hv + pe_ref[1:2, :]).astype(BF16), w1_ref[half:2 * half, :].astype(BF16))
    pre = u + pltpu.roll(v, nh - 1, axis=0)
    o_ref[...] = _dot(jax.nn.silu(pre).astype(BF16), w2_ref[...].astype(BF16)).astype(o_ref.dtype)


def _compress(halves, pe, w1, w2):
    _, B, G, NH, HW = halves.shape
    return pl.pallas_call(
        _compress_body,
        grid=(2, B, G),
        in_specs=[
            pl.BlockSpec((None, None, None, NH, HW), lambda s, b, g: (s, b, g, 0, 0)),
            pl.BlockSpec((None, 2, HW), lambda s, b, g: (s, 0, 0)),
            pl.BlockSpec((None, 2 * HW, CMP_HIDDEN), lambda s, b, g: (s, 0, 0)),
            pl.BlockSpec((None, CMP_HIDDEN, NSA_DH), lambda s, b, g: (s, 0, 0)),
        ],
        out_specs=pl.BlockSpec((None, None, None, NH, NSA_DH), lambda s, b, g: (s, b, g, 0, 0)),
        out_shape=jax.ShapeDtypeStruct((2, B, G, NH, NSA_DH), BF16),
        compiler_params=_cparams("arbitrary", "arbitrary", "arbitrary"),
        name="nsa_compress",
    )(halves, pe, w1, w2)


def _nsa_compress_all(proj, pe_k, w1_k, w2_k, pe_v, w1_v, w2_v, B, S):
    NH = S // CMP_STRIDE
    c0 = NSA_WIDTH

    def halves(idx):
        t = proj[:, c0 + idx * NSA_KV:c0 + (idx + 1) * NSA_KV]
        t = t.reshape(B, NH, CMP_STRIDE, NSA_GROUPS, NSA_DH).transpose(0, 3, 1, 2, 4)
        return t.reshape(B, NSA_GROUPS, NH, CMP_STRIDE * NSA_DH)

    hw = CMP_STRIDE * NSA_DH
    return _compress(jnp.stack([halves(0), halves(1)]),
                     jnp.stack([pe_k.reshape(2, hw), pe_v.reshape(2, hw)]),
                     jnp.stack([w1_k, w1_v]), jnp.stack([w2_k, w2_v]))


NSA_TQ = 128


NSA_MASK_ROWS = 64
NSA_WIN_MASK_ROWS = 16
NSA_FAR = 4
NSA_SLC_PAD = (NSA_FAR - 1) * NSA_TQ


NSA_V_ROWS = NSA_DH + 16


def _softmax_cols(s):
    m = jnp.max(s, axis=0, keepdims=True)
    return m, jnp.exp2(s - m).astype(BF16)


def _normalise(acc):
    return acc[0:NSA_DH] * (1.0 / acc[NSA_DH:NSA_DH + 1])


NSA_GROUPS_PER_STEP = 4


def _nsa_body(qt_ref, gl_ref, kc_ref, vct_ref, ks_ref, vst_ref, kw_ref, vwt_ref,
              bc_ref, bw_ref, bn_ref, ovt_ref, o_ref, *, top_k, n_slc):
    TQ = NSA_TQ
    GG = NSA_GROUPS_PER_STEP
    i = pl.program_id(2)
    ovt = ovt_ref[...]
    jb = lax.broadcasted_iota(jnp.int32, (n_slc, TQ), 0)
    qblk = (i * TQ + lax.broadcasted_iota(jnp.int32, (n_slc, TQ), 1)) // SLC_LEN
    forced = (jb == 0) | (jb == qblk) | (jb == qblk - 1)
    future = jb > qblk
    near = pl.ds(pl.multiple_of(NSA_SLC_PAD + (i - 1) * TQ, TQ), 2 * TQ)

    def heads(a):
        return jnp.concatenate([a] * NSA_HPG, axis=1)

    def queries(g):
        qt = qt_ref[g]
        return jnp.concatenate([qt[h * NSA_DH:(h + 1) * NSA_DH, :] for h in range(NSA_HPG)],
                               axis=1)

    def compressed(g, s):
        m = jnp.max(s, axis=0, keepdims=True)
        p = jnp.where(s > 0.5 * NEG_INF, jnp.exp2(s - m), 0.0)
        acc = _dot(vct_ref[g], p.astype(BF16))
        inv_l = 1.0 / jnp.maximum(acc[NSA_DH:NSA_DH + 1], 1e-30)
        o_cmp = acc[0:NSA_DH] * inv_l
        pc = p * inv_l
        ps = pc[:, 0:TQ] + pc[:, TQ:2 * TQ] + pc[:, 2 * TQ:3 * TQ] + pc[:, 3 * TQ:4 * TQ]
        ps_hi = ps.astype(BF16)
        ps_lo = (ps - ps_hi.astype(F32)).astype(BF16)
        imp = _dot(ovt, ps_hi) + _dot(ovt, ps_lo)
        return o_cmp, jnp.where(forced, BIG, jnp.where(future, -BIG, imp))

    def window(g, s):
        _, p = _softmax_cols(s)
        return _normalise(_dot(vwt_ref[g, :, win], p))

    def select(imp, q4):
        slabs = []
        for v in range(n_slc // 8):
            slab = imp[8 * v:8 * v + 8, :]
            jbs = jb[8 * v:8 * v + 8, :]
            cnt = jnp.zeros((8, TQ), F32)
            for jp in range(n_slc):
                row = imp[jp:jp + 1, :]
                if jp < 8 * v:
                    beats = row >= slab
                elif jp >= 8 * v + 8:
                    beats = row > slab
                else:
                    beats = (row > slab) | ((row == slab) & (jbs > jp))
                cnt = cnt + jnp.where(beats, 1.0, 0.0)
            slabs.append(jnp.where(cnt < top_k, 0.0, 1.0))
        if n_slc < NSA_MASK_ROWS:
            slabs.append(jnp.ones((NSA_MASK_ROWS - n_slc, TQ), F32))
        unsel = jnp.concatenate(slabs, axis=0)
        return jnp.concatenate([heads(unsel.astype(BF16)), q4], axis=0)

    def near_tiles(g, s):
        m, p = _softmax_cols(s)
        return m, _dot(vst_ref[g, :, near], p)

    win = pl.ds(pl.multiple_of(i * TQ, TQ), WIN + TQ)
    pad_row = jnp.where(lax.broadcasted_iota(jnp.int32, (NSA_WIN_MASK_ROWS, NSA_HPG * TQ), 0) == 0,
                        1.0, 0.0).astype(BF16)
    gs = range(GG)
    q4 = [queries(g) for g in gs]
    nh = kc_ref.shape[1]
    cmp_rows = pl.ds(pl.multiple_of(nh - (TQ // CMP_STRIDE) * i, TQ // CMP_STRIDE), nh)
    s_cmp = [_dot(kc_ref[g], q4[g]) + bc_ref[g, cmp_rows, :] for g in gs]
    s_win = [_dot(kw_ref[g, win, :], jnp.concatenate([pad_row, q4[g]], axis=0)) + bw_ref[g] for g in gs]
    cmp_out = [compressed(g, s_cmp[g]) for g in gs]
    o_win = [window(g, s_win[g]) for g in gs]
    qa = [select(cmp_out[g][1], q4[g]) for g in gs]
    s_near = [_dot(ks_ref[g, near, :], qa[g]) + bn_ref[g] for g in gs]
    start = tuple(near_tiles(g, s_near[g]) for g in gs)

    n_far = i - 1
    n_chunks = jnp.maximum((n_far + NSA_FAR - 1) // NSA_FAR, 0)
    far0 = NSA_SLC_PAD + TQ * (n_far - NSA_FAR * n_chunks)

    def far_step(c, carry):
        keys = pl.ds(pl.multiple_of(far0 + c * NSA_FAR * TQ, TQ), NSA_FAR * TQ)
        out = []
        logits = [_dot(ks_ref[g, keys, :], qa[g]) for g in gs]
        for g in gs:
            m, acc = carry[g]
            s = logits[g]
            m_new = jnp.maximum(m, jnp.max(s, axis=0, keepdims=True))
            p = jnp.exp2(s - m_new).astype(BF16)
            acc = jnp.exp2(m - m_new) * acc + _dot(vst_ref[g, :, keys], p)
            out.append((m_new, acc))
        return tuple(out)

    far = lax.fori_loop(0, n_chunks, far_step, start)

    for g in gs:
        o_slc = _normalise(far[g][1])
        o_cmp = cmp_out[g][0]
        sg = jax.nn.sigmoid(gl_ref[g])
        outs = []
        for h in range(NSA_HPG):
            lanes = slice(h * TQ, (h + 1) * TQ)
            outs.append(sg[h:h + 1, :] * o_cmp[:, lanes]
                        + sg[NSA_HPG + h:NSA_HPG + h + 1, :] * o_slc[:, lanes]
                        + sg[2 * NSA_HPG + h:2 * NSA_HPG + h + 1, :] * o_win[g][:, lanes])
        width = NSA_HPG * NSA_DH
        o_ref[:, g * width:(g + 1) * width] = jnp.concatenate(outs, axis=0).T.astype(o_ref.dtype)


def _rel_bucket(dist):
    n = jnp.maximum(dist, 0)
    max_exact = REL_BUCKETS // 2
    nf = jnp.maximum(n, 1).astype(F32)
    large = max_exact + (jnp.log(nf / max_exact) / math.log(REL_MAX_DIST / max_exact)
                         * (REL_BUCKETS - max_exact)).astype(jnp.int32)
    large = jnp.minimum(large, REL_BUCKETS - 1)
    return jnp.where(n < max_exact, n, large)


def _bias_table(table, dist, valid):
    onehot = (_rel_bucket(jnp.asarray(dist, jnp.int32))[..., None]
              == jnp.arange(REL_BUCKETS, dtype=jnp.int32)).astype(F32)
    b = jnp.einsum('rmk,ghk->ghrm', onehot, table, precision=lax.Precision.HIGHEST)
    return jnp.where(jnp.asarray(valid), b, NEG_INF)


def _relayout_body(x_ref, ms_ref, qt_ref, ks_ref, vst_ref, kw_ref, vwt_ref, *, lead):
    TQ = NSA_TQ
    real = pl.program_id(1) >= lead
    x = x_ref[...]

    def columns(idx):
        lo = NSA_WIDTH + idx * NSA_KV
        return jnp.where(real, x[:, lo:lo + NSA_KV], 0.0)

    @pl.when(real)
    def _():
        qt = x[:, 0:NSA_WIDTH].astype(F32).T.astype(BF16)
        qt_ref[...] = qt.reshape(qt_ref.shape)

    ones = jnp.where((lax.broadcasted_iota(jnp.int32, (NSA_V_ROWS - NSA_DH, TQ), 0) == 0) & real,
                     1.0, 0.0).astype(BF16)
    pad_col = jnp.where((lax.broadcasted_iota(jnp.int32, (TQ, NSA_WIN_MASK_ROWS), 1) == 0) & ~real,
                        NEG_INF, 0.0).astype(BF16)
    ms = ms_ref[...]
    k_slc, k_win = columns(2), columns(4)
    v_slc = columns(3).astype(F32).T.astype(BF16)
    v_win = columns(5).astype(F32).T.astype(BF16)
    for g in range(NSA_GROUPS):
        d = slice(g * NSA_DH, (g + 1) * NSA_DH)
        ks_ref[g] = jnp.concatenate([ms, k_slc[:, d]], axis=1)
        kw_ref[g] = jnp.concatenate([pad_col, k_win[:, d]], axis=1)
        vst_ref[g] = jnp.concatenate([v_slc[d, :], ones], axis=0)
        vwt_ref[g] = jnp.concatenate([v_win[d, :], ones], axis=0)


def _nsa_relayout(proj, slc_masks, B, S):
    TQ = NSA_TQ
    G, DH, SP = NSA_GROUPS, NSA_DH, NSA_SLC_PAD
    nt = S // TQ
    lead = WIN // TQ
    slc_lead = SP // TQ
    tok = lambda u: jnp.maximum(u - lead, 0)
    slc = lambda u: jnp.maximum(u - (lead - slc_lead), 0)
    return pl.pallas_call(
        functools.partial(_relayout_body, lead=lead),
        grid=(B, lead + nt),
        in_specs=[pl.BlockSpec((TQ, proj.shape[1]), lambda b, u: (b * nt + tok(u), 0)),
                  pl.BlockSpec((TQ, NSA_MASK_ROWS), lambda b, u: (slc(u), 0))],
        out_specs=[
            pl.BlockSpec((None, G, NSA_HPG * DH, TQ), lambda b, u: (b, 0, 0, tok(u))),
            pl.BlockSpec((None, G, TQ, NSA_MASK_ROWS + DH), lambda b, u: (b, 0, slc(u), 0)),
            pl.BlockSpec((None, G, NSA_V_ROWS, TQ), lambda b, u: (b, 0, 0, slc(u))),
            pl.BlockSpec((None, G, TQ, NSA_WIN_MASK_ROWS + DH), lambda b, u: (b, 0, u, 0)),
            pl.BlockSpec((None, G, NSA_V_ROWS, TQ), lambda b, u: (b, 0, 0, u)),
        ],
        out_shape=[
            jax.ShapeDtypeStruct((B, G, NSA_HPG * DH, S), BF16),
            jax.ShapeDtypeStruct((B, G, S + SP, NSA_MASK_ROWS + DH), BF16),
            jax.ShapeDtypeStruct((B, G, NSA_V_ROWS, S + SP), BF16),
            jax.ShapeDtypeStruct((B, G, S + WIN, NSA_WIN_MASK_ROWS + DH), BF16),
            jax.ShapeDtypeStruct((B, G, NSA_V_ROWS, S + WIN), BF16),
        ],
        compiler_params=_cparams("arbitrary", "arbitrary"),
        name="nsa_relayout",
    )(proj, slc_masks)


def _nsa(proj, glog, kc, vc, rel_bias, B, S):
    TQ = NSA_TQ
    G, HPG, DH = NSA_GROUPS, NSA_HPG, NSA_DH
    NH = S // CMP_STRIDE
    n_cmp = (S - CMP_LEN) // CMP_STRIDE + 1
    n_slc = S // SLC_LEN
    top_k = min(SLC_TOPK, n_slc)
    nt = S // TQ
    c0 = NSA_WIDTH

    MR = NSA_MASK_ROWS
    assert n_slc % 8 == 0 and (n_slc < MR or (n_slc == MR and top_k < n_slc))

    SP = NSA_SLC_PAD
    pos = np.arange(SP + S) - SP
    cols = np.zeros((SP + S, MR), np.float32)
    cols[:, :n_slc] = ((pos // SLC_LEN)[:, None] == np.arange(n_slc)[None, :]) & (pos >= 0)[:, None]
    if n_slc < MR:
        cols[:, n_slc] = pos < 0
    else:
        cols[pos < 0, :] = 1.0
    qt, ks, vst, kw, vwt = _nsa_relayout(proj, jnp.asarray(cols * NEG_INF, BF16), B, S)
    kcb = kc
    ones = jnp.zeros((NSA_V_ROWS - DH, NH), BF16).at[0].set(1.0)
    vct = jnp.concatenate([vc.transpose(0, 1, 3, 2), jnp.broadcast_to(ones, (B, G) + ones.shape)],
                          axis=2)

    table = rel_bias.T.reshape(G, HPG, REL_BUCKETS) * LOG2E
    far = table[:, :, REL_BUCKETS - 1]
    r = np.arange(TQ)[:, None]

    def keys_major(b):
        return b.transpose(0, 3, 1, 2).reshape(G, b.shape[3], HPG * TQ)

    d_c = r - (CMP_LEN - 1) - CMP_STRIDE * (np.arange(2 * NH)[None, :] - NH)
    bias_cmp = keys_major(_bias_table(table, d_c, d_c >= 0))
    d_w = r + WIN - np.arange(WIN + TQ)[None, :]
    bias_win = keys_major(_bias_table(table, d_w, (d_w >= 0) & (d_w < WIN)))
    d_n = r + TQ - np.arange(2 * TQ)[None, :]
    bias_near = keys_major(_bias_table(table - far[:, :, None], d_n, d_n >= 0))

    ci = np.arange(NH)[None, :] * CMP_STRIDE
    sj = np.arange(n_slc)[:, None] * SLC_LEN
    overlap_t = ((ci <= sj + SLC_LEN - 1) & (ci + CMP_LEN - 1 >= sj) & (np.arange(NH)[None, :] < n_cmp))

    gate_rows = 16
    glog_g = glog[:, :GATE_COLS].reshape(B * S, 3, G, HPG).transpose(2, 1, 3, 0).reshape(G, 3 * HPG, B * S)
    glog_g = jnp.pad(glog_g, ((0, 0), (0, gate_rows - 3 * HPG), (0, 0)))

    GG = NSA_GROUPS_PER_STEP
    whole = lambda shape: pl.BlockSpec((None, GG) + shape, lambda b, g, i: (b, g, 0, 0))
    tile = lambda k: pl.BlockSpec((GG, k, HPG * TQ), lambda b, g, i: (g, 0, 0))
    return pl.pallas_call(
        functools.partial(_nsa_body, top_k=top_k, n_slc=n_slc),
        grid=(B, G // GG, nt),
        in_specs=[
            pl.BlockSpec((None, GG, HPG * DH, TQ), lambda b, g, i: (b, g, 0, i)),
            pl.BlockSpec((GG, gate_rows, TQ), lambda b, g, i: (g, 0, b * nt + i)),
            whole((NH, DH)), whole((NSA_V_ROWS, NH)),
            whole((S + SP, MR + DH)), whole((NSA_V_ROWS, S + SP)),
            whole((S + WIN, NSA_WIN_MASK_ROWS + DH)), whole((NSA_V_ROWS, S + WIN)),
            tile(2 * NH), tile(WIN + TQ), tile(2 * TQ),
            pl.BlockSpec((n_slc, NH), lambda b, g, i: (0, 0)),
        ],
        out_specs=pl.BlockSpec((TQ, GG * HPG * DH), lambda b, g, i: (b * nt + i, g)),
        out_shape=jax.ShapeDtypeStruct((B * S, NSA_WIDTH), BF16),
        compiler_params=_cparams("arbitrary", "arbitrary", "arbitrary"),
        name="nsa_attention",
    )(qt, glog_g, kcb, vct, ks, vst, kw, vwt, bias_cmp, bias_win, bias_near,
      jnp.asarray(overlap_t, BF16))


ROUTE_E1, ROUTE_E2, ROUTE_W1, ROUTE_W2 = 0, 1, 2, 3


def _outproj_body(x_ref, yh_ref, yn_ref, wo_ref, g2_ref, wrh_ref, wrl_ref, br_ref,
                  x2_ref, h2_ref, rt_ref):
    x2 = (x_ref[...] + _dot(yh_ref[...], wo_ref[0:HG_WIDTH, :])
          + _dot(yn_ref[...], wo_ref[HG_WIDTH:HG_WIDTH + NSA_WIDTH, :]))
    x2_ref[...] = x2
    h = x2 * lax.rsqrt(jnp.mean(x2 * x2, axis=-1, keepdims=True) + RMS_EPS) * g2_ref[...]
    hb = h.astype(BF16)
    hbf = hb.astype(F32)
    half = h.shape[1] // 2
    bits = pltpu.bitcast(hbf, jnp.int32)
    h2_ref[...] = (lax.shift_right_logical(bits[:, :half], 16)
                   | (bits[:, half:] & jnp.int32(-65536)))
    hl = (h - hbf).astype(BF16)
    wrh = wrh_ref[...]
    lg = _dot(hb, wrh) + _dot(hl, wrh) + _dot(hb, wrl_ref[...]) + br_ref[...]
    lane = lax.broadcasted_iota(jnp.int32, lg.shape, 1)
    first = lambda hit: jnp.min(jnp.where(hit, lane, LANES), axis=-1, keepdims=True)
    lgg = jnp.where(lane < N_EXPERT_GROUPS, lg, NEG_INF)
    mg = jnp.max(lgg, axis=-1, keepdims=True)
    p_top = 1.0 / jnp.sum(jnp.exp(lgg - mg), axis=-1, keepdims=True)
    lo = N_EXPERT_GROUPS + EXPERTS_PER_GROUP * first(lgg == mg)
    le = jnp.where((lane >= lo) & (lane < lo + EXPERTS_PER_GROUP), lg, NEG_INF)
    v1 = jnp.max(le, axis=-1, keepdims=True)
    i1 = first(le == v1)
    le = jnp.where(lane == i1, NEG_INF, le)
    v2 = jnp.max(le, axis=-1, keepdims=True)
    i2 = first(le == v2)
    e21 = jnp.exp(v2 - v1)
    w1 = p_top / (1.0 + e21)
    rt = jnp.where(lane == ROUTE_E1, (i1 - N_EXPERT_GROUPS).astype(F32), 0.0)
    rt = jnp.where(lane == ROUTE_E2, (i2 - N_EXPERT_GROUPS).astype(F32), rt)
    rt = jnp.where(lane == ROUTE_W1, w1, rt)
    rt_ref[...] = jnp.where(lane == ROUTE_W2, w1 * e21, rt)


def _outproj(x2d, y_hg, y_nsa, w_out, g2, wr_hi, wr_lo, b_r, tm):
    T, D = x2d.shape
    row = lambda n: pl.BlockSpec((tm, n), lambda i: (i, 0))
    full = lambda a: pl.BlockSpec(a.shape, lambda i: (0, 0))
    return pl.pallas_call(
        _outproj_body,
        grid=(T // tm,),
        in_specs=[row(D), row(HG_WIDTH), row(NSA_WIDTH), full(w_out), full(g2),
                  full(wr_hi), full(wr_lo), full(b_r)],
        out_specs=[row(D), row(D // 2), row(LANES)],
        out_shape=[jax.ShapeDtypeStruct((T, D), F32), jax.ShapeDtypeStruct((T, D // 2), jnp.int32),
                   jax.ShapeDtypeStruct((T, LANES), F32)],
        compiler_params=_cparams("arbitrary"),
        name="outproj_router",
    )(x2d, y_hg, y_nsa, w_out, g2, wr_hi, wr_lo, b_r)


ROW_TILE = 256


def _dispatch_body(dest_ref, h_ref, xs0_hbm, xs_hbm, sem):
    del xs0_hbm
    tb = h_ref.shape[0]

    def start(t, carry):
        for k in range(2):
            pltpu.make_async_copy(h_ref.at[pl.ds(t, 1), :],
                                  xs_hbm.at[pl.ds(dest_ref[0, 2 * t + k], 1), :], sem).start()
        return carry

    lax.fori_loop(0, tb, start, 0, unroll=8)
    for _ in range(2):
        pltpu.make_async_copy(h_ref, xs_hbm.at[pl.ds(0, tb), :], sem).wait()


def _dispatch(h2p, dest, n_rows):
    T, W = h2p.shape
    tb = ROW_TILE
    return pl.pallas_call(
        _dispatch_body,
        grid=(T // tb,),
        in_specs=[pl.BlockSpec((None, 1, 2 * tb), lambda i: (i, 0, 0), memory_space=pltpu.SMEM),
                  pl.BlockSpec((tb, W), lambda i: (i, 0)),
                  pl.BlockSpec(memory_space=pl.ANY)],
        out_specs=pl.BlockSpec(memory_space=pl.ANY),
        out_shape=jax.ShapeDtypeStruct((n_rows, W), jnp.int32),
        scratch_shapes=[pltpu.SemaphoreType.DMA(())],
        input_output_aliases={2: 0},
        compiler_params=_cparams("arbitrary"),
        name="moe_dispatch",
    )(dest.reshape(T // tb, 1, 2 * tb), h2p, jnp.zeros((n_rows, W), jnp.int32))


MOE_ALIGN = 128
MOE_TM = 1024
MOE_SUB = 256
MOE_TF = 256
MOE_CAST_PIECES = 4


def _moe_body(ie_ref, ir_ref, in_ref, nb_ref, xs_hbm, *refs):
    np_ = MOE_CAST_PIECES
    wg_refs, wu_refs, wd_refs = refs[0:np_], refs[np_:2 * np_], refs[2 * np_:3 * np_]
    y_hbm, xbuf, xlo, xhi, acc, wgb, wub, wdb, sem_in, sem_out = refs[3 * np_:]
    del ie_ref
    w = pl.program_id(0)
    f = pl.program_id(1)
    n_w = pl.num_programs(0)
    n_f = pl.num_programs(1)
    last_f = n_f - 1
    nsub = in_ref[w]
    row0 = pl.multiple_of(ir_ref[w] * MOE_ALIGN, MOE_ALIGN)
    slot = w % 2
    per_mm = MOE_SUB // MOE_ALIGN
    n_full = nsub // per_mm
    n_iter = n_full + nsub % per_mm
    half = xlo.shape[1]
    cur = (w * n_f + f) % 2
    nxt_item = jnp.minimum(jnp.where(f == last_f, w + 1, w), n_w - 1)
    cast_next = ((f < last_f) | (w + 1 < n_w)) & (in_ref[nxt_item] > 0)

    def cast_piece(p):
        rk = wg_refs[p].shape[0]
        wgb[1 - cur, p * rk:(p + 1) * rk, :] = wg_refs[p][...].astype(BF16)
        wub[1 - cur, p * rk:(p + 1) * rk, :] = wu_refs[p][...].astype(BF16)
        rf = wd_refs[p].shape[0]
        wdb[1 - cur, p * rf:(p + 1) * rf, :] = wd_refs[p][...].astype(BF16)

    def cast_pieces(which):
        for p in range(np_):
            @pl.when(which(p))
            def _():
                cast_piece(p)

    def x_copy(item, s):
        r = pl.multiple_of(ir_ref[item] * MOE_ALIGN, MOE_ALIGN)
        return pltpu.make_async_copy(xs_hbm.at[pl.ds(r, MOE_TM), :], xbuf.at[s], sem_in.at[s])

    def y_copy(r, dst_row):
        return pltpu.make_async_copy(acc.at[pl.ds(r, MOE_ALIGN), :],
                                     y_hbm.at[pl.ds(dst_row, MOE_ALIGN), :], sem_out)

    def for_row_blocks(fn):
        def body(j, carry):
            fn(pl.multiple_of(j * MOE_SUB, MOE_SUB), MOE_SUB, j)
            return carry

        lax.fori_loop(0, n_full, body, 0)

        @pl.when(nsub % per_mm == 1)
        def _():
            fn(pl.multiple_of(n_full * MOE_SUB, MOE_ALIGN), MOE_ALIGN, n_full)

    @pl.when((f == 0) & (nsub > 0))
    def _():
        @pl.when(in_ref[jnp.maximum(w - 1, 0)] == 0)
        def _():
            x_copy(w, slot).start()

        x_copy(w, slot).wait()

        def unpack(r, size, _):
            bits = xbuf[slot, pl.ds(r, size), :]
            xlo[pl.ds(r, size), :] = pltpu.bitcast(lax.shift_left(bits, 16), F32).astype(BF16)
            xhi[pl.ds(r, size), :] = pltpu.bitcast(bits & jnp.int32(-65536), F32).astype(BF16)

        for_row_blocks(unpack)

    def experts(first, last):
        def block(r, size, it):
            rows = pl.ds(r, size)
            lo = xlo[rows, :]
            hi = xhi[rows, :]
            hg = _dot(lo, wgb[cur, 0:half, :]) + _dot(hi, wgb[cur, half:2 * half, :])
            hu = _dot(lo, wub[cur, 0:half, :]) + _dot(hi, wub[cur, half:2 * half, :])
            y = _dot((jax.nn.silu(hg) * hu).astype(BF16), wdb[cur])
            if first:
                acc[rows, :] = y
            else:
                acc[rows, :] += y
            if last:
                for k in range(size // MOE_ALIGN):
                    y_copy(r + k * MOE_ALIGN, row0 + r + k * MOE_ALIGN).start()

            cast_pieces(lambda p: cast_next & (it == p))

        for_row_blocks(block)
        cast_pieces(lambda p: cast_next & (p >= n_iter))

    cast_pieces(lambda p: (nsub == 0) & cast_next)

    @pl.when((f == 0) & (nsub > 0))
    def _():
        experts(True, False)

    @pl.when((f > 0) & (f < last_f) & (nsub > 0))
    def _():
        experts(False, False)

    @pl.when((f == last_f) & (nsub > 0))
    def _():
        nxt = jnp.minimum(w + 1, n_w - 1)

        @pl.when((w + 1 < n_w) & (in_ref[nxt] > 0))
        def _():
            x_copy(nxt, 1 - slot).start()

        experts(False, True)

        def wait(j, carry):
            y_copy(0, 0).wait()
            return carry

        lax.fori_loop(0, nsub, wait, 0)

    @pl.when((f == last_f) & (w == n_w - 1))
    def _():
        acc[0:MOE_ALIGN, :] = jnp.zeros((MOE_ALIGN, acc.shape[1]), F32)
        used = nb_ref[0]
        total = y_hbm.shape[0] // MOE_ALIGN

        def fill(j, carry):
            y_copy(0, pl.multiple_of(j * MOE_ALIGN, MOE_ALIGN)).start()
            return carry

        def wait(j, carry):
            y_copy(0, 0).wait()
            return carry

        lax.fori_loop(used, total, fill, 0)
        lax.fori_loop(used, total, wait, 0)


def _moe(xs, item_e, item_r, item_n, n_blocks, w_gate, w_up, w_down):
    rows = xs.shape[0]
    D = 2 * xs.shape[1]
    n_items = item_e.shape[0]
    nf = EXPERT_DFF // MOE_TF
    assert nf >= 2 and MOE_SUB == 2 * MOE_ALIGN

    def nxt(w, f, ie, nn):
        w1 = jnp.minimum(jnp.where(f == nf - 1, w + 1, w), n_items - 1)
        f1 = jnp.where(f == nf - 1, 0, f + 1)
        return ie[w1], jnp.where(nn[w1] > 0, f1, nf - 1)

    npc = MOE_CAST_PIECES

    def w_in_spec(p):
        def index(w, f, ie, ir, nn, nb):
            e, t = nxt(w, f, ie, nn)
            return e, p, t
        return pl.BlockSpec((None, D // npc, MOE_TF), index)

    def w_out_spec(p):
        def index(w, f, ie, ir, nn, nb):
            e, t = nxt(w, f, ie, nn)
            return e, t * npc + p, 0
        return pl.BlockSpec((None, MOE_TF // npc, D), index)

    pieces = range(npc)
    return pl.pallas_call(
        _moe_body,
        grid_spec=pltpu.PrefetchScalarGridSpec(
            num_scalar_prefetch=4,
            grid=(n_items, nf),
            in_specs=([pl.BlockSpec(memory_space=pl.ANY)] + [w_in_spec(p) for p in pieces]
                      + [w_in_spec(p) for p in pieces] + [w_out_spec(p) for p in pieces]),
            out_specs=pl.BlockSpec(memory_space=pl.ANY),
            scratch_shapes=[pltpu.VMEM((2, MOE_TM, D // 2), jnp.int32),
                            pltpu.VMEM((MOE_TM, D // 2), BF16), pltpu.VMEM((MOE_TM, D // 2), BF16),
                            pltpu.VMEM((MOE_TM, D), F32),
                            pltpu.VMEM((2, D, MOE_TF), BF16), pltpu.VMEM((2, D, MOE_TF), BF16),
                            pltpu.VMEM((2, MOE_TF, D), BF16),
                            pltpu.SemaphoreType.DMA((2,)), pltpu.SemaphoreType.DMA(())],
        ),
        out_shape=jax.ShapeDtypeStruct((rows, D), F32),
        compiler_params=_cparams("arbitrary", "arbitrary"),
        name="moe_experts",
    )(item_e, item_r, item_n, n_blocks, xs, *([w_gate] * npc + [w_up] * npc + [w_down] * npc))


def _final_body(dcur_ref, dnext_ref, x2_ref, rt_ref, g_ref, ys_hbm, o_ref, ybuf, sems):
    i = pl.program_id(0)
    tb = x2_ref.shape[0]
    slot = i % 2

    def gather(d_ref, s):
        def body(t, carry):
            for k in range(2):
                pltpu.make_async_copy(ys_hbm.at[pl.ds(d_ref[0, 2 * t + k], 1), :],
                                      ybuf.at[s, k, pl.ds(t, 1), :], sems.at[s]).start()
            return carry

        lax.fori_loop(0, tb, body, 0, unroll=8)

    @pl.when(i == 0)
    def _():
        gather(dcur_ref, 0)

    @pl.when(i + 1 < pl.num_programs(0))
    def _():
        gather(dnext_ref, 1 - slot)

    for k in range(2):
        pltpu.make_async_copy(ys_hbm.at[pl.ds(0, tb), :], ybuf.at[slot, k], sems.at[slot]).wait()
    rt = rt_ref[...]
    x = (x2_ref[...] + rt[:, ROUTE_W1:ROUTE_W1 + 1] * ybuf[slot, 0]
         + rt[:, ROUTE_W2:ROUTE_W2 + 1] * ybuf[slot, 1])
    o_ref[...] = x * lax.rsqrt(jnp.mean(x * x, axis=-1, keepdims=True) + RMS_EPS) * g_ref[...]


def _final(x2, ys, dest, rt, g):
    T, D = x2.shape
    tb = ROW_TILE
    nb = T // tb
    dest3 = dest.reshape(nb, 1, 2 * tb)
    return pl.pallas_call(
        _final_body,
        grid=(nb,),
        in_specs=[pl.BlockSpec((None, 1, 2 * tb), lambda i: (i, 0, 0), memory_space=pltpu.SMEM),
                  pl.BlockSpec((None, 1, 2 * tb), lambda i: (jnp.minimum(i + 1, nb - 1), 0, 0),
                               memory_space=pltpu.SMEM),
                  pl.BlockSpec((tb, D), lambda i: (i, 0)),
                  pl.BlockSpec((tb, LANES), lambda i: (i, 0)),
                  pl.BlockSpec((1, D), lambda i: (0, 0)),
                  pl.BlockSpec(memory_space=pl.ANY)],
        out_specs=pl.BlockSpec((tb, D), lambda i: (i, 0)),
        out_shape=jax.ShapeDtypeStruct((T, D), F32),
        scratch_shapes=[pltpu.VMEM((2, 2, tb, D), F32), pltpu.SemaphoreType.DMA((2,))],
        compiler_params=_cparams("arbitrary"),
        name="combine_final_norm",
    )(dest3, dest3, x2, rt, g, ys)


def _moe_layout(e_flat):
    A = e_flat.shape[0]
    onehot = (e_flat[:, None] == jnp.arange(N_EXPERTS, dtype=jnp.int32)[None, :]).astype(jnp.int32)
    csum = jnp.cumsum(onehot, axis=0)
    counts = csum[-1]
    nblk = (counts + MOE_ALIGN - 1) // MOE_ALIGN
    blk0 = jnp.cumsum(nblk) - nblk
    dest = jnp.sum(onehot * (blk0[None, :] * MOE_ALIGN + csum - 1), axis=1).astype(jnp.int32)
    per_item = MOE_TM // MOE_ALIGN
    nitem = (nblk + per_item - 1) // per_item
    iend = jnp.cumsum(nitem)
    n_items = (A // MOE_ALIGN + N_EXPERTS + N_EXPERTS * (per_item - 1)) // per_item
    w = jnp.arange(n_items, dtype=jnp.int32)
    total = iend[-1]
    wv = jnp.minimum(w, total - 1)
    item_e = jnp.minimum(jnp.searchsorted(iend, wv, side='right'), N_EXPERTS - 1).astype(jnp.int32)
    sb = wv - (iend - nitem)[item_e]
    item_r = (blk0[item_e] + sb * per_item).astype(jnp.int32)
    item_n = jnp.where(w < total, jnp.minimum(per_item, nblk[item_e] - sb * per_item), 0).astype(jnp.int32)
    n_blocks = jnp.sum(nblk).astype(jnp.int32).reshape(1)
    lead = lambda a, v: jnp.concatenate([jnp.asarray(v, jnp.int32).reshape(1), a])
    return dest, lead(item_e, item_e[0]), lead(item_r, 0), lead(item_n, 0), n_blocks


def kernel(x, norm1_g, w_in, hg_lb_logits, hg_norm_g, cmp_pe_k, cmp_w1_k, cmp_w2_k, cmp_pe_v,
           cmp_w1_v, cmp_w2_v, rel_bias, w_out, norm2_g, w_router_group, b_router_group,
           w_router_expert, b_router_expert, w_expert_gate, w_expert_up, w_expert_down,
           final_norm_g):
    B, S, D = x.shape
    T = B * S
    assert w_in.shape[0] == 1, "single-layer block"
    x2d = x.reshape(T, D)

    lower = jax.nn.softmax(hg_lb_logits.astype(F32), axis=0)[0].reshape(HG_HEADS, 1, HG_DIM)

    w_main = w_in[0, :, :MAIN_COLS].astype(BF16)
    w_gate_cols = jnp.pad(w_in[0, :, MAIN_COLS:].astype(BF16), ((0, 0), (0, LANES - GATE_COLS)))
    proj_hg, proj_nsa, glog = _inproj(x2d, norm1_g[0].reshape(1, D), w_main, w_gate_cols,
                                      tm=1024, tn=512)

    y_hg = _hgrn(proj_hg, lower, hg_norm_g[0].reshape(1, HG_DIM), B, S, tb=512)
    kcvc = _nsa_compress_all(proj_nsa, cmp_pe_k[0], cmp_w1_k[0], cmp_w2_k[0],
                             cmp_pe_v[0], cmp_w1_v[0], cmp_w2_v[0], B, S)
    y_nsa = _nsa(proj_nsa, glog, kcvc[0], kcvc[1], rel_bias, B, S)

    w_r = jnp.concatenate([w_router_group[0], w_router_expert[0]], axis=1)
    w_r = jnp.pad(w_r, ((0, 0), (0, LANES - w_r.shape[1])))
    wr_hi = w_r.astype(BF16)
    wr_lo = (w_r - wr_hi.astype(F32)).astype(BF16)
    b_r = jnp.concatenate([b_router_group[0], b_router_expert[0]])
    b_r = jnp.pad(b_r, (0, LANES - b_r.shape[0])).reshape(1, LANES)
    x2, h2, rt = _outproj(x2d, y_hg, y_nsa, w_out[0].astype(BF16), norm2_g[0].reshape(1, D),
                          wr_hi, wr_lo, b_r, tm=512)

    e_flat = rt[:, ROUTE_E1:ROUTE_E2 + 1].astype(jnp.int32).reshape(2 * T)
    dest, item_e, item_r, item_n, n_blocks = _moe_layout(e_flat)
    n_rows = (2 * T // MOE_ALIGN + N_EXPERTS) * MOE_ALIGN + MOE_TM
    xs = _dispatch(h2, dest, n_rows)
    ys = _moe(xs, item_e, item_r, item_n, n_blocks,
              w_expert_gate[0], w_expert_up[0], w_expert_down[0])

    out = _final(x2, ys, dest, rt, final_norm_g.reshape(1, D))
    return out.reshape(B, S, D)
```

```python
import functools
import math

import jax
import jax.numpy as jnp
import numpy as np
from jax import lax
from jax.experimental import pallas as pl
from jax.experimental.pallas import tpu as pltpu

F32 = jnp.float32
BF16 = jnp.bfloat16

D_MODEL = 2048
HG_HEADS = 8
HG_DIM = 128
HG_WIDTH = HG_HEADS * HG_DIM
HG_CHUNK = 64
NSA_HEADS = 16
NSA_GROUPS = 4
NSA_HPG = NSA_HEADS // NSA_GROUPS
NSA_DH = 64
NSA_WIDTH = NSA_HEADS * NSA_DH
NSA_KV = NSA_GROUPS * NSA_DH
CMP_LEN = 32
CMP_STRIDE = 16
CMP_HIDDEN = 256
SLC_LEN = 64
SLC_TOPK = 16
WIN = 512
REL_BUCKETS = 32
REL_MAX_DIST = 128
N_EXPERT_GROUPS = 8
EXPERTS_PER_GROUP = 8
N_EXPERTS = N_EXPERT_GROUPS * EXPERTS_PER_GROUP
EXPERT_DFF = 1024
RMS_EPS = 1e-6
NEG_INF = -1e30
LOG2E = math.log2(math.e)
NSA_Q_SCALE = NSA_DH ** -0.5 * LOG2E
BIG = 1e9

MAIN_COLS = 4 * HG_WIDTH + NSA_WIDTH + 6 * NSA_KV
GATE_COLS = 3 * NSA_HEADS
LANES = 128
VMEM_LIMIT = 56 * 1024 * 1024


def _cparams(*sem):
    return pltpu.CompilerParams(dimension_semantics=sem, vmem_limit_bytes=VMEM_LIMIT)


def _nt(a, b):
    return lax.dot_general(a, b, (((1,), (1,)), ((), ())), preferred_element_type=F32)


def _tn(a, b):
    return lax.dot_general(a, b, (((0,), (0,)), ((), ())), preferred_element_type=F32)


def _dot(a, b):
    return jnp.dot(a, b, preferred_element_type=F32)


def _inproj_body(x_ref, g_ref, w_ref, wg_ref, oh_ref, on_ref, og_ref, h_scr, *, hg_tiles, q_tiles):
    j = pl.program_id(1)

    @pl.when(j == 0)
    def _():
        x = x_ref[...]
        y = x * lax.rsqrt(jnp.mean(x * x, axis=-1, keepdims=True) + RMS_EPS) * g_ref[...]
        h_scr[...] = y.astype(BF16)
        og_ref[...] = _dot(h_scr[...], wg_ref[...])

    @pl.when(j < hg_tiles)
    def _():
        oh_ref[...] = _dot(h_scr[...], w_ref[...])

    @pl.when(j >= hg_tiles)
    def _():
        scale = jnp.where(j < hg_tiles + q_tiles, NSA_Q_SCALE, 1.0)
        on_ref[...] = (_dot(h_scr[...], w_ref[...]) * scale).astype(on_ref.dtype)


def _inproj(x2d, g, w_main, w_gate, tm, tn):
    T, D = x2d.shape
    N = w_main.shape[1]
    n_hg = 4 * HG_WIDTH
    hg_tiles = n_hg // tn
    return pl.pallas_call(
        functools.partial(_inproj_body, hg_tiles=hg_tiles, q_tiles=NSA_WIDTH // tn),
        grid=(T // tm, N // tn),
        in_specs=[
            pl.BlockSpec((tm, D), lambda i, j: (i, 0)),
            pl.BlockSpec((1, D), lambda i, j: (0, 0)),
            pl.BlockSpec((D, tn), lambda i, j: (0, j)),
            pl.BlockSpec((D, LANES), lambda i, j: (0, 0)),
        ],
        out_specs=[
            pl.BlockSpec((tm, tn), lambda i, j: (i, jnp.minimum(j, hg_tiles - 1))),
            pl.BlockSpec((tm, tn), lambda i, j: (i, jnp.maximum(j - hg_tiles, 0))),
            pl.BlockSpec((tm, LANES), lambda i, j: (i, 0)),
        ],
        out_shape=[jax.ShapeDtypeStruct((T, n_hg), F32),
                   jax.ShapeDtypeStruct((T, N - n_hg), BF16),
                   jax.ShapeDtypeStruct((T, LANES), F32)],
        scratch_shapes=[pltpu.VMEM((tm, D), BF16)],
        compiler_params=_cparams("arbitrary", "arbitrary"),
        name="inproj",
    )(x2d, g, w_main, w_gate)


def _hgrn_tables(C):
    t = np.arange(C)
    levels = []
    m = C // 2
    while m >= 1:
        levels.append(m)
        m //= 2
    rows = [np.tril(np.ones((C, C), np.float32)),
            np.triu(np.ones((C, C), np.float32), 1)]
    lev = np.full((C, C), -1, np.int32)
    for li, m in enumerate(levels):
        r = t % (2 * m)
        mid = t - r + m - 1
        M = np.zeros((C, C), np.float32)
        for i in range(C):
            if r[i] >= m:
                M[i, mid[i] + 1:i + 1] = 1.0
            else:
                M[i, i + 1:mid[i] + 1] = 1.0
        rows.append(M)
        same = (t[:, None] // (2 * m)) == (t[None, :] // (2 * m))
        lev[same & (r[:, None] >= m) & (r[None, :] < m)] = li
    lev[t[:, None] == t[None, :]] = len(levels)
    return np.concatenate(rows, 0), lev, len(levels)


HG_HEADS_PER_STEP = 8


def _hgrn_body(q_ref, f_ref, i_ref, g_ref, lb_ref, ng_ref, mst_ref, lev_ref, o_ref, st_ref,
               *, C, n_lev):
    @pl.when(pl.program_id(2) == 0)
    def _():
        st_ref[...] = jnp.zeros_like(st_ref)

    ng = ng_ref[...]
    mst = mst_ref[...]
    lev = lev_ref[...]
    n_chunks = q_ref.shape[0] // C

    def gates(rows, hb):
        cols = slice(hb * HG_DIM, (hb + 1) * HG_DIM)
        lb = lb_ref[hb]
        q = jax.nn.silu(q_ref[rows, cols])
        f = lb + (1.0 - lb) * jax.nn.sigmoid(f_ref[rows, cols])
        lf = jnp.log(f)
        lf0 = lf.astype(BF16)
        r1 = lf - lf0.astype(F32)
        lf1 = r1.astype(BF16)
        lf2 = (r1 - lf1.astype(F32)).astype(BF16)
        e = _dot(mst, jnp.concatenate([lf0, lf1, lf2], axis=0))
        return q, 1.0 - f, i_ref[rows, cols].astype(BF16), e

    def scores(q, k, gdec):
        a = jnp.where(lev == n_lev, _nt(q.astype(BF16), k.astype(BF16)), 0.0)
        for li in range(n_lev):
            g_m = gdec[(2 + li) * C:(3 + li) * C]
            a_m = _nt((q * g_m).astype(BF16), (k * g_m).astype(BF16))
            a = jnp.where(lev == li, a_m, a)
        return a.astype(BF16)

    def finish(rows, hb, q, k, v, gdec, a):
        cols = slice(hb * HG_DIM, (hb + 1) * HG_DIM)
        g_b = gdec[0:C]
        g_r = gdec[C:2 * C]
        st = st_ref[hb]
        o = _nt((q * g_b).astype(BF16), st.astype(BF16)) + _dot(a, v)
        g_last = g_b[C - 1:C, :]
        st_ref[hb] = st * g_last + _tn(v, (k * g_r).astype(BF16))
        o = o * lax.rsqrt(jnp.mean(o * o, axis=-1, keepdims=True) + RMS_EPS) * ng
        o_ref[rows, cols] = (o * jax.nn.silu(g_ref[rows, cols])).astype(o_ref.dtype)

    def chunk(c, carry):
        rows = pl.ds(pl.multiple_of(c * C, C), C)
        hs = range(HG_HEADS_PER_STEP)
        qkve = [gates(rows, hb) for hb in hs]
        gdec = [jnp.exp(x[3]) for x in qkve]
        a = [scores(qkve[hb][0], qkve[hb][1], gdec[hb]) for hb in hs]
        for hb in hs:
            finish(rows, hb, qkve[hb][0], qkve[hb][1], qkve[hb][2], gdec[hb], a[hb])
        return carry

    lax.fori_loop(0, n_chunks, chunk, 0)


def _hgrn(proj, lb, ng, B, S, tb):
    C = HG_CHUNK
    mst, lev, n_lev = _hgrn_tables(C)
    mst = np.concatenate([mst] * 3, axis=1)
    nb = S // tb
    HB = HG_HEADS_PER_STEP
    hsteps = HG_HEADS // HB

    def col(group):
        return pl.BlockSpec((tb, HB * HG_DIM), lambda b, h, t: (b * nb + t, group * hsteps + h))

    return pl.pallas_call(
        functools.partial(_hgrn_body, C=C, n_lev=n_lev),
        grid=(B, hsteps, nb),
        in_specs=[
            col(0), col(1), col(2), col(3),
            pl.BlockSpec((HB, 1, HG_DIM), lambda b, h, t: (h, 0, 0)),
            pl.BlockSpec((1, HG_DIM), lambda b, h, t: (0, 0)),
            pl.BlockSpec(mst.shape, lambda b, h, t: (0, 0)),
            pl.BlockSpec(lev.shape, lambda b, h, t: (0, 0)),
        ],
        out_specs=pl.BlockSpec((tb, HB * HG_DIM), lambda b, h, t: (b * nb + t, h)),
        out_shape=jax.ShapeDtypeStruct((B * S, HG_WIDTH), BF16),
        scratch_shapes=[pltpu.VMEM((HB, HG_DIM, HG_DIM), F32)],
        compiler_params=_cparams("arbitrary", "arbitrary", "arbitrary"),
        name="hgrn2",
    )(proj, proj, proj, proj, lb, ng, jnp.asarray(mst, BF16), jnp.asarray(lev))


def _compress_body(h_ref, pe_ref, w1_ref, w2_ref, o_ref):
    hv = h_ref[...]
    half = hv.shape[1]
    nh = hv.shape[0]
    u = _dot((hv + pe_ref[0:1, :]).astype(BF16), w1_ref[0:half, :].astype(BF16))
    v = _dot((hv + pe_ref[1:2, :]).astype(BF16), w1_ref[half:2 * half, :].astype(BF16))
    pre = u + pltpu.roll(v, nh - 1, axis=0)
    o_ref[...] = _dot(jax.nn.silu(pre).astype(BF16), w2_ref[...].astype(BF16)).astype(o_ref.dtype)


def _compress(halves, pe, w1, w2):
    _, B, G, NH, HW = halves.shape
    return pl.pallas_call(
        _compress_body,
        grid=(2, B, G),
        in_specs=[
            pl.BlockSpec((None, None, None, NH, HW), lambda s, b, g: (s, b, g, 0, 0)),
            pl.BlockSpec((None, 2, HW), lambda s, b, g: (s, 0, 0)),
            pl.BlockSpec((None, 2 * HW, CMP_HIDDEN), lambda s, b, g: (s, 0, 0)),
            pl.BlockSpec((None, CMP_HIDDEN, NSA_DH), lambda s, b, g: (s, 0, 0)),
        ],
        out_specs=pl.BlockSpec((None, None, None, NH, NSA_DH), lambda s, b, g: (s, b, g, 0, 0)),
        out_shape=jax.ShapeDtypeStruct((2, B, G, NH, NSA_DH), BF16),
        compiler_params=_cparams("arbitrary", "arbitrary", "arbitrary"),
        name="nsa_compress",
    )(halves, pe, w1, w2)


def _nsa_compress_all(proj, pe_k, w1_k, w2_k, pe_v, w1_v, w2_v, B, S):
    NH = S // CMP_STRIDE
    c0 = NSA_WIDTH

    def halves(idx):
        t = proj[:, c0 + idx * NSA_KV:c0 + (idx + 1) * NSA_KV]
        t = t.reshape(B, NH, CMP_STRIDE, NSA_GROUPS, NSA_DH).transpose(0, 3, 1, 2, 4)
        return t.reshape(B, NSA_GROUPS, NH, CMP_STRIDE * NSA_DH)

    hw = CMP_STRIDE * NSA_DH
    return _compress(jnp.stack([halves(0), halves(1)]),
                     jnp.stack([pe_k.reshape(2, hw), pe_v.reshape(2, hw)]),
                     jnp.stack([w1_k, w1_v]), jnp.stack([w2_k, w2_v]))


NSA_TQ = 128


NSA_MASK_ROWS = 64
NSA_WIN_MASK_ROWS = 16
NSA_FAR = 4
NSA_SLC_PAD = (NSA_FAR - 1) * NSA_TQ


NSA_V_ROWS = NSA_DH + 16


def _softmax_cols(s):
    m = jnp.max(s, axis=0, keepdims=True)
    return m, jnp.exp2(s - m).astype(BF16)


def _normalise(acc):
    return acc[0:NSA_DH] * (1.0 / acc[NSA_DH:NSA_DH + 1])


NSA_GROUPS_PER_STEP = 4


def _nsa_body(qt_ref, gl_ref, kc_ref, vct_ref, ks_ref, vst_ref, kw_ref, vwt_ref,
              bc_ref, bw_ref, bn_ref, ovt_ref, o_ref, *, top_k, n_slc):
    TQ = NSA_TQ
    GG = NSA_GROUPS_PER_STEP
    i = pl.program_id(2)
    ovt = ovt_ref[...]
    jb = lax.broadcasted_iota(jnp.int32, (n_slc, TQ), 0)
    qblk = (i * TQ + lax.broadcasted_iota(jnp.int32, (n_slc, TQ), 1)) // SLC_LEN
    forced = (jb == 0) | (jb == qblk) | (jb == qblk - 1)
    future = jb > qblk
    near = pl.ds(pl.multiple_of(NSA_SLC_PAD + (i - 1) * TQ, TQ), 2 * TQ)

    def heads(a):
        return jnp.concatenate([a] * NSA_HPG, axis=1)

    def queries(g):
        qt = qt_ref[g]
        return jnp.concatenate([qt[h * NSA_DH:(h + 1) * NSA_DH, :] for h in range(NSA_HPG)],
                               axis=1)

    def compressed(g, s):
        m = jnp.max(s, axis=0, keepdims=True)
        p = jnp.where(s > 0.5 * NEG_INF, jnp.exp2(s - m), 0.0)
        acc = _dot(vct_ref[g], p.astype(BF16))
        inv_l = 1.0 / jnp.maximum(acc[NSA_DH:NSA_DH + 1], 1e-30)
        o_cmp = acc[0:NSA_DH] * inv_l
        pc = p * inv_l
        ps = pc[:, 0:TQ] + pc[:, TQ:2 * TQ] + pc[:, 2 * TQ:3 * TQ] + pc[:, 3 * TQ:4 * TQ]
        ps_hi = ps.astype(BF16)
        ps_lo = (ps - ps_hi.astype(F32)).astype(BF16)
        imp = _dot(ovt, ps_hi) + _dot(ovt, ps_lo)
        return o_cmp, jnp.where(forced, BIG, jnp.where(future, -BIG, imp))

    def window(g, s):
        _, p = _softmax_cols(s)
        return _normalise(_dot(vwt_ref[g, :, win], p))

    def select(imp, q4):
        slabs = []
        for v in range(n_slc // 8):
            slab = imp[8 * v:8 * v + 8, :]
            jbs = jb[8 * v:8 * v + 8, :]
            cnt = jnp.zeros((8, TQ), F32)
            for jp in range(n_slc):
                row = imp[jp:jp + 1, :]
                if jp < 8 * v:
                    beats = row >= slab
                elif jp >= 8 * v + 8:
                    beats = row > slab
                else:
                    beats = (row > slab) | ((row == slab) & (jbs > jp))
                cnt = cnt + jnp.where(beats, 1.0, 0.0)
            slabs.append(jnp.where(cnt < top_k, 0.0, 1.0))
        if n_slc < NSA_MASK_ROWS:
            slabs.append(jnp.ones((NSA_MASK_ROWS - n_slc, TQ), F32))
        unsel = jnp.concatenate(slabs, axis=0)
        return jnp.concatenate([heads(unsel.astype(BF16)), q4], axis=0)

    def near_tiles(g, s):
        m, p = _softmax_cols(s)
        return m, _dot(vst_ref[g, :, near], p)

    win = pl.ds(pl.multiple_of(i * TQ, TQ), WIN + TQ)
    pad_row = jnp.where(lax.broadcasted_iota(jnp.int32, (NSA_WIN_MASK_ROWS, NSA_HPG * TQ), 0) == 0,
                        1.0, 0.0).astype(BF16)
    gs = range(GG)
    q4 = [queries(g) for g in gs]
    nh = kc_ref.shape[1]
    cmp_rows = pl.ds(pl.multiple_of(nh - (TQ // CMP_STRIDE) * i, TQ // CMP_STRIDE), nh)
    s_cmp = [_dot(kc_ref[g], q4[g]) + bc_ref[g, cmp_rows, :] for g in gs]
    s_win = [_dot(kw_ref[g, win, :], jnp.concatenate([pad_row, q4[g]], axis=0)) + bw_ref[g] for g in gs]
    cmp_out = [compressed(g, s_cmp[g]) for g in gs]
    o_win = [window(g, s_win[g]) for g in gs]
    qa = [select(cmp_out[g][1], q4[g]) for g in gs]
    s_near = [_dot(ks_ref[g, near, :], qa[g]) + bn_ref[g] for g in gs]
    start = tuple(near_tiles(g, s_near[g]) for g in gs)

    n_far = i - 1
    n_chunks = jnp.maximum((n_far + NSA_FAR - 1) // NSA_FAR, 0)
    far0 = NSA_SLC_PAD + TQ * (n_far - NSA_FAR * n_chunks)

    def far_step(c, carry):
        keys = pl.ds(pl.multiple_of(far0 + c * NSA_FAR * TQ, TQ), NSA_FAR * TQ)
        out = []
        logits = [_dot(ks_ref[g, keys, :], qa[g]) for g in gs]
        for g in gs:
            m, acc = carry[g]
            s = logits[g]
            m_new = jnp.maximum(m, jnp.max(s, axis=0, keepdims=True))
            p = jnp.exp2(s - m_new).astype(BF16)
            acc = jnp.exp2(m - m_new) * acc + _dot(vst_ref[g, :, keys], p)
            out.append((m_new, acc))
        return tuple(out)

    far = lax.fori_loop(0, n_chunks, far_step, start)

    for g in gs:
        o_slc = _normalise(far[g][1])
        o_cmp = cmp_out[g][0]
        sg = jax.nn.sigmoid(gl_ref[g])
        outs = []
        for h in range(NSA_HPG):
            lanes = slice(h * TQ, (h + 1) * TQ)
            outs.append(sg[h:h + 1, :] * o_cmp[:, lanes]
                        + sg[NSA_HPG + h:NSA_HPG + h + 1, :] * o_slc[:, lanes]
                        + sg[2 * NSA_HPG + h:2 * NSA_HPG + h + 1, :] * o_win[g][:, lanes])
        width = NSA_HPG * NSA_DH
        o_ref[:, g * width:(g + 1) * width] = jnp.concatenate(outs, axis=0).T.astype(o_ref.dtype)


def _rel_bucket(dist):
    n = jnp.maximum(dist, 0)
    max_exact = REL_BUCKETS // 2
    nf = jnp.maximum(n, 1).astype(F32)
    large = max_exact + (jnp.log(nf / max_exact) / math.log(REL_MAX_DIST / max_exact)
                         * (REL_BUCKETS - max_exact)).astype(jnp.int32)
    large = jnp.minimum(large, REL_BUCKETS - 1)
    return jnp.where(n < max_exact, n, large)


def _bias_table(table, dist, valid):
    onehot = (_rel_bucket(jnp.asarray(dist, jnp.int32))[..., None]
              == jnp.arange(REL_BUCKETS, dtype=jnp.int32)).astype(F32)
    b = jnp.einsum('rmk,ghk->ghrm', onehot, table, precision=lax.Precision.HIGHEST)
    return jnp.where(jnp.asarray(valid), b, NEG_INF)


def _relayout_body(x_ref, ms_ref, qt_ref, ks_ref, vst_ref, kw_ref, vwt_ref, *, lead):
    TQ = NSA_TQ
    real = pl.program_id(1) >= lead
    x = x_ref[...]

    def columns(idx):
        lo = NSA_WIDTH + idx * NSA_KV
        return jnp.where(real, x[:, lo:lo + NSA_KV], 0.0)

    @pl.when(real)
    def _():
        qt = x[:, 0:NSA_WIDTH].astype(F32).T.astype(BF16)
        qt_ref[...] = qt.reshape(qt_ref.shape)

    ones = jnp.where((lax.broadcasted_iota(jnp.int32, (NSA_V_ROWS - NSA_DH, TQ), 0) == 0) & real,
                     1.0, 0.0).astype(BF16)
    pad_col = jnp.where((lax.broadcasted_iota(jnp.int32, (TQ, NSA_WIN_MASK_ROWS), 1) == 0) & ~real,
                        NEG_INF, 0.0).astype(BF16)
    ms = ms_ref[...]
    k_slc, k_win = columns(2), columns(4)
    v_slc = columns(3).astype(F32).T.astype(BF16)
    v_win = columns(5).astype(F32).T.astype(BF16)
    for g in range(NSA_GROUPS):
        d = slice(g * NSA_DH, (g + 1) * NSA_DH)
        ks_ref[g] = jnp.concatenate([ms, k_slc[:, d]], axis=1)
        kw_ref[g] = jnp.concatenate([pad_col, k_win[:, d]], axis=1)
        vst_ref[g] = jnp.concatenate([v_slc[d, :], ones], axis=0)
        vwt_ref[g] = jnp.concatenate([v_win[d, :], ones], axis=0)


def _nsa_relayout(proj, slc_masks, B, S):
    TQ = NSA_TQ
    G, DH, SP = NSA_GROUPS, NSA_DH, NSA_SLC_PAD
    nt = S // TQ
    lead = WIN // TQ
    slc_lead = SP // TQ
    tok = lambda u: jnp.maximum(u - lead, 0)
    slc = lambda u: jnp.maximum(u - (lead - slc_lead), 0)
    return pl.pallas_call(
        functools.partial(_relayout_body, lead=lead),
        grid=(B, lead + nt),
        in_specs=[pl.BlockSpec((TQ, proj.shape[1]), lambda b, u: (b * nt + tok(u), 0)),
                  pl.BlockSpec((TQ, NSA_MASK_ROWS), lambda b, u: (slc(u), 0))],
        out_specs=[
            pl.BlockSpec((None, G, NSA_HPG * DH, TQ), lambda b, u: (b, 0, 0, tok(u))),
            pl.BlockSpec((None, G, TQ, NSA_MASK_ROWS + DH), lambda b, u: (b, 0, slc(u), 0)),
            pl.BlockSpec((None, G, NSA_V_ROWS, TQ), lambda b, u: (b, 0, 0, slc(u))),
            pl.BlockSpec((None, G, TQ, NSA_WIN_MASK_ROWS + DH), lambda b, u: (b, 0, u, 0)),
            pl.BlockSpec((None, G, NSA_V_ROWS, TQ), lambda b, u: (b, 0, 0, u)),
        ],
        out_shape=[
            jax.ShapeDtypeStruct((B, G, NSA_HPG * DH, S), BF16),
            jax.ShapeDtypeStruct((B, G, S + SP, NSA_MASK_ROWS + DH), BF16),
            jax.ShapeDtypeStruct((B, G, NSA_V_ROWS, S + SP), BF16),
            jax.ShapeDtypeStruct((B, G, S + WIN, NSA_WIN_MASK_ROWS + DH), BF16),
            jax.ShapeDtypeStruct((B, G, NSA_V_ROWS, S + WIN), BF16),
        ],
        compiler_params=_cparams("arbitrary", "arbitrary"),
        name="nsa_relayout",
    )(proj, slc_masks)


def _nsa(proj, glog, kc, vc, rel_bias, B, S):
    TQ = NSA_TQ
    G, HPG, DH = NSA_GROUPS, NSA_HPG, NSA_DH
    NH = S // CMP_STRIDE
    n_cmp = (S - CMP_LEN) // CMP_STRIDE + 1
    n_slc = S // SLC_LEN
    top_k = min(SLC_TOPK, n_slc)
    nt = S // TQ
    c0 = NSA_WIDTH

    MR = NSA_MASK_ROWS
    assert n_slc % 8 == 0 and (n_slc < MR or (n_slc == MR and top_k < n_slc))

    SP = NSA_SLC_PAD
    pos = np.arange(SP + S) - SP
    cols = np.zeros((SP + S, MR), np.float32)
    cols[:, :n_slc] = ((pos // SLC_LEN)[:, None] == np.arange(n_slc)[None, :]) & (pos >= 0)[:, None]
    if n_slc < MR:
        cols[:, n_slc] = pos < 0
    else:
        cols[pos < 0, :] = 1.0
    qt, ks, vst, kw, vwt = _nsa_relayout(proj, jnp.asarray(cols * NEG_INF, BF16), B, S)
    kcb = kc
    ones = jnp.zeros((NSA_V_ROWS - DH, NH), BF16).at[0].set(1.0)
    vct = jnp.concatenate([vc.transpose(0, 1, 3, 2), jnp.broadcast_to(ones, (B, G) + ones.shape)],
                          axis=2)

    table = rel_bias.T.reshape(G, HPG, REL_BUCKETS) * LOG2E
    far = table[:, :, REL_BUCKETS - 1]
    r = np.arange(TQ)[:, None]

    def keys_major(b):
        return b.transpose(0, 3, 1, 2).reshape(G, b.shape[3], HPG * TQ)

    d_c = r - (CMP_LEN - 1) - CMP_STRIDE * (np.arange(2 * NH)[None, :] - NH)
    bias_cmp = keys_major(_bias_table(table, d_c, d_c >= 0))
    d_w = r + WIN - np.arange(WIN + TQ)[None, :]
    bias_win = keys_major(_bias_table(table, d_w, (d_w >= 0) & (d_w < WIN)))
    d_n = r + TQ - np.arange(2 * TQ)[None, :]
    bias_near = keys_major(_bias_table(table - far[:, :, None], d_n, d_n >= 0))

    ci = np.arange(NH)[None, :] * CMP_STRIDE
    sj = np.arange(n_slc)[:, None] * SLC_LEN
    overlap_t = ((ci <= sj + SLC_LEN - 1) & (ci + CMP_LEN - 1 >= sj) & (np.arange(NH)[None, :] < n_cmp))

    gate_rows = 16
    glog_g = glog[:, :GATE_COLS].reshape(B * S, 3, G, HPG).transpose(2, 1, 3, 0).reshape(G, 3 * HPG, B * S)
    glog_g = jnp.pad(glog_g, ((0, 0), (0, gate_rows - 3 * HPG), (0, 0)))

    GG = NSA_GROUPS_PER_STEP
    whole = lambda shape: pl.BlockSpec((None, GG) + shape, lambda b, g, i: (b, g, 0, 0))
    tile = lambda k: pl.BlockSpec((GG, k, HPG * TQ), lambda b, g, i: (g, 0, 0))
    return pl.pallas_call(
        functools.partial(_nsa_body, top_k=top_k, n_slc=n_slc),
        grid=(B, G // GG, nt),
        in_specs=[
            pl.BlockSpec((None, GG, HPG * DH, TQ), lambda b, g, i: (b, g, 0, i)),
            pl.BlockSpec((GG, gate_rows, TQ), lambda b, g, i: (g, 0, b * nt + i)),
            whole((NH, DH)), whole((NSA_V_ROWS, NH)),
            whole((S + SP, MR + DH)), whole((NSA_V_ROWS, S + SP)),
            whole((S + WIN, NSA_WIN_MASK_ROWS + DH)), whole((NSA_V_ROWS, S + WIN)),
            tile(2 * NH), tile(WIN + TQ), tile(2 * TQ),
            pl.BlockSpec((n_slc, NH), lambda b, g, i: (0, 0)),
        ],
        out_specs=pl.BlockSpec((TQ, GG * HPG * DH), lambda b, g, i: (b * nt + i, g)),
        out_shape=jax.ShapeDtypeStruct((B * S, NSA_WIDTH), BF16),
        compiler_params=_cparams("arbitrary", "arbitrary", "arbitrary"),
        name="nsa_attention",
    )(qt, glog_g, kcb, vct, ks, vst, kw, vwt, bias_cmp, bias_win, bias_near,
      jnp.asarray(overlap_t, BF16))


ROUTE_E1, ROUTE_E2, ROUTE_W1, ROUTE_W2 = 0, 1, 2, 3


def _outproj_body(x_ref, yh_ref, yn_ref, wo_ref, g2_ref, wrh_ref, wrl_ref, br_ref,
                  x2_ref, h2_ref, rt_ref):
    x2 = (x_ref[...] + _dot(yh_ref[...], wo_ref[0:HG_WIDTH, :])
          + _dot(yn_ref[...], wo_ref[HG_WIDTH:HG_WIDTH + NSA_WIDTH, :]))
    x2_ref[...] = x2
    h = x2 * lax.rsqrt(jnp.mean(x2 * x2, axis=-1, keepdims=True) + RMS_EPS) * g2_ref[...]
    hb = h.astype(BF16)
    hbf = hb.astype(F32)
    half = h.shape[1] // 2
    bits = pltpu.bitcast(hbf, jnp.int32)
    h2_ref[...] = (lax.shift_right_logical(bits[:, :half], 16)
                   | (bits[:, half:] & jnp.int32(-65536)))
    hl = (h - hbf).astype(BF16)
    wrh = wrh_ref[...]
    lg = _dot(hb, wrh) + _dot(hl, wrh) + _dot(hb, wrl_ref[...]) + br_ref[...]
    lane = lax.broadcasted_iota(jnp.int32, lg.shape, 1)
    first = lambda hit: jnp.min(jnp.where(hit, lane, LANES), axis=-1, keepdims=True)
    lgg = jnp.where(lane < N_EXPERT_GROUPS, lg, NEG_INF)
    mg = jnp.max(lgg, axis=-1, keepdims=True)
    p_top = 1.0 / jnp.sum(jnp.exp(lgg - mg), axis=-1, keepdims=True)
    lo = N_EXPERT_GROUPS + EXPERTS_PER_GROUP * first(lgg == mg)
    le = jnp.where((lane >= lo) & (lane < lo + EXPERTS_PER_GROUP), lg, NEG_INF)
    v1 = jnp.max(le, axis=-1, keepdims=True)
    i1 = first(le == v1)
    le = jnp.where(lane == i1, NEG_INF, le)
    v2 = jnp.max(le, axis=-1, keepdims=True)
    i2 = first(le == v2)
    e21 = jnp.exp(v2 - v1)
    w1 = p_top / (1.0 + e21)
    rt = jnp.where(lane == ROUTE_E1, (i1 - N_EXPERT_GROUPS).astype(F32), 0.0)
    rt = jnp.where(lane == ROUTE_E2, (i2 - N_EXPERT_GROUPS).astype(F32), rt)
    rt = jnp.where(lane == ROUTE_W1, w1, rt)
    rt_ref[...] = jnp.where(lane == ROUTE_W2, w1 * e21, rt)


def _outproj(x2d, y_hg, y_nsa, w_out, g2, wr_hi, wr_lo, b_r, tm):
    T, D = x2d.shape
    row = lambda n: pl.BlockSpec((tm, n), lambda i: (i, 0))
    full = lambda a: pl.BlockSpec(a.shape, lambda i: (0, 0))
    return pl.pallas_call(
        _outproj_body,
        grid=(T // tm,),
        in_specs=[row(D), row(HG_WIDTH), row(NSA_WIDTH), full(w_out), full(g2),
                  full(wr_hi), full(wr_lo), full(b_r)],
        out_specs=[row(D), row(D // 2), row(LANES)],
        out_shape=[jax.ShapeDtypeStruct((T, D), F32), jax.ShapeDtypeStruct((T, D // 2), jnp.int32),
                   jax.ShapeDtypeStruct((T, LANES), F32)],
        compiler_params=_cparams("arbitrary"),
        name="outproj_router",
    )(x2d, y_hg, y_nsa, w_out, g2, wr_hi, wr_lo, b_r)


ROW_TILE = 512


def _dispatch_body(dest_ref, h_ref, xs0_hbm, xs_hbm, sem):
    del xs0_hbm
    tb = h_ref.shape[0]

    def start(t, carry):
        for k in range(2):
            pltpu.make_async_copy(h_ref.at[pl.ds(t, 1), :],
                                  xs_hbm.at[pl.ds(dest_ref[0, 2 * t + k], 1), :], sem).start()
        return carry

    lax.fori_loop(0, tb, start, 0, unroll=8)
    for _ in range(2):
        pltpu.make_async_copy(h_ref, xs_hbm.at[pl.ds(0, tb), :], sem).wait()


def _dispatch(h2p, dest, n_rows):
    T, W = h2p.shape
    tb = ROW_TILE
    return pl.pallas_call(
        _dispatch_body,
        grid=(T // tb,),
        in_specs=[pl.BlockSpec((None, 1, 2 * tb), lambda i: (i, 0, 0), memory_space=pltpu.SMEM),
                  pl.BlockSpec((tb, W), lambda i: (i, 0)),
                  pl.BlockSpec(memory_space=pl.ANY)],
        out_specs=pl.BlockSpec(memory_space=pl.ANY),
        out_shape=jax.ShapeDtypeStruct((n_rows, W), jnp.int32),
        scratch_shapes=[pltpu.SemaphoreType.DMA(())],
        input_output_aliases={2: 0},
        compiler_params=_cparams("arbitrary"),
        name="moe_dispatch",
    )(dest.reshape(T // tb, 1, 2 * tb), h2p, jnp.zeros((n_rows, W), jnp.int32))


MOE_ALIGN = 128
MOE_TM = 1024
MOE_SUB = 256
MOE_TF = 256
MOE_CAST_PIECES = 4
MOE_WEIGHT_BUFFERS = 3


def _moe_body(ie_ref, ir_ref, in_ref, nb_ref, xs_hbm, wg_hbm, wu_hbm, wd_hbm, y_hbm,
              xbuf, xlo, xhi, acc, wgb, wub, wdb, wgf, wuf, wdf, sem_in, sem_out, sem_w):
    w = pl.program_id(0)
    f = pl.program_id(1)
    n_w = pl.num_programs(0)
    n_f = pl.num_programs(1)
    last_f = n_f - 1
    ring = wgf.shape[0]
    step = w * n_f + f

    def weight_copies(k):
        item = jnp.minimum(k // n_f, n_w - 1)
        wanted = (k < n_w * n_f) & (in_ref[item] > 0)
        e = ie_ref[item]
        cols = pl.ds(pl.multiple_of((k % n_f) * MOE_TF, MOE_TF), MOE_TF)
        s = k % ring
        return wanted, (pltpu.make_async_copy(wg_hbm.at[e, :, cols], wgf.at[s], sem_w.at[0, s]),
                        pltpu.make_async_copy(wu_hbm.at[e, :, cols], wuf.at[s], sem_w.at[1, s]),
                        pltpu.make_async_copy(wd_hbm.at[e, cols, :], wdf.at[s], sem_w.at[2, s]))

    def request_weights(k):
        wanted, copies = weight_copies(k)

        @pl.when(wanted)
        def _():
            for c in copies:
                c.start()

    @pl.when(step == 0)
    def _():
        for k in range(1, ring):
            request_weights(k)

    request_weights(step + ring)
    cast_next, arriving = weight_copies(step + 1)

    @pl.when(cast_next)
    def _():
        for c in arriving:
            c.wait()

    nslot = (step + 1) % ring
    nsub = in_ref[w]
    row0 = pl.multiple_of(ir_ref[w] * MOE_ALIGN, MOE_ALIGN)
    slot = w % 2
    per_mm = MOE_SUB // MOE_ALIGN
    n_full = nsub // per_mm
    n_iter = n_full + nsub % per_mm
    half = xlo.shape[1]
    cur = step % 2

    def cast_piece(p):
        rk = wgf.shape[1] // MOE_CAST_PIECES
        rows = pl.ds(pl.multiple_of(p * rk, rk), rk)
        wgb[1 - cur, rows, :] = wgf[nslot, rows, :].astype(BF16)
        wub[1 - cur, rows, :] = wuf[nslot, rows, :].astype(BF16)
        rf = wdf.shape[1] // MOE_CAST_PIECES
        rows = pl.ds(pl.multiple_of(p * rf, rf), rf)
        wdb[1 - cur, rows, :] = wdf[nslot, rows, :].astype(BF16)

    def x_copy(item, s):
        r = pl.multiple_of(ir_ref[item] * MOE_ALIGN, MOE_ALIGN)
        return pltpu.make_async_copy(xs_hbm.at[pl.ds(r, MOE_TM), :], xbuf.at[s], sem_in.at[s])

    def y_copy(r, dst_row):
        return pltpu.make_async_copy(acc.at[pl.ds(r, MOE_ALIGN), :],
                                     y_hbm.at[pl.ds(dst_row, MOE_ALIGN), :], sem_out)

    def for_row_blocks(fn):
        def body(j, carry):
            fn(pl.multiple_of(j * MOE_SUB, MOE_SUB), MOE_SUB, j)
            return carry

        lax.fori_loop(0, n_full, body, 0)

        @pl.when(nsub % per_mm == 1)
        def _():
            fn(pl.multiple_of(n_full * MOE_SUB, MOE_ALIGN), MOE_ALIGN, n_full)

    @pl.when((f == 0) & (nsub > 0))
    def _():
        @pl.when(in_ref[jnp.maximum(w - 1, 0)] == 0)
        def _():
            x_copy(w, slot).start()

        x_copy(w, slot).wait()

        def unpack(r, size, _):
            bits = xbuf[slot, pl.ds(r, size), :]
            xlo[pl.ds(r, size), :] = pltpu.bitcast(lax.shift_left(bits, 16), F32).astype(BF16)
            xhi[pl.ds(r, size), :] = pltpu.bitcast(bits & jnp.int32(-65536), F32).astype(BF16)

        for_row_blocks(unpack)

    def experts(first, last):
        def block(r, size, it):
            rows = pl.ds(r, size)
            lo = xlo[rows, :]
            hi = xhi[rows, :]
            hg = _dot(lo, wgb[cur, 0:half, :]) + _dot(hi, wgb[cur, half:2 * half, :])
            hu = _dot(lo, wub[cur, 0:half, :]) + _dot(hi, wub[cur, half:2 * half, :])
            y = _dot((jax.nn.silu(hg) * hu).astype(BF16), wdb[cur])
            if first:
                acc[rows, :] = y
            else:
                acc[rows, :] += y
            if last:
                for k in range(size // MOE_ALIGN):
                    y_copy(r + k * MOE_ALIGN, row0 + r + k * MOE_ALIGN).start()

            @pl.when(cast_next & (it < MOE_CAST_PIECES))
            def _():
                cast_piece(it)

        for_row_blocks(block)

        @pl.when(cast_next)
        def _():
            lax.fori_loop(jnp.minimum(n_iter, MOE_CAST_PIECES), MOE_CAST_PIECES,
                          lambda p, c: (cast_piece(p), c)[1], 0)

    @pl.when((nsub == 0) & cast_next)
    def _():
        lax.fori_loop(0, MOE_CAST_PIECES, lambda p, c: (cast_piece(p), c)[1], 0)

    @pl.when((f == 0) & (nsub > 0))
    def _():
        experts(True, False)

    @pl.when((f > 0) & (f < last_f) & (nsub > 0))
    def _():
        experts(False, False)

    @pl.when((f == last_f) & (nsub > 0))
    def _():
        nxt = jnp.minimum(w + 1, n_w - 1)

        @pl.when((w + 1 < n_w) & (in_ref[nxt] > 0))
        def _():
            x_copy(nxt, 1 - slot).start()

        experts(False, True)

        def wait(j, carry):
            y_copy(0, 0).wait()
            return carry

        lax.fori_loop(0, nsub, wait, 0)

    @pl.when((f == last_f) & (w == n_w - 1))
    def _():
        acc[0:MOE_ALIGN, :] = jnp.zeros((MOE_ALIGN, acc.shape[1]), F32)
        used = nb_ref[0]
        total = y_hbm.shape[0] // MOE_ALIGN

        def fill(j, carry):
            y_copy(0, pl.multiple_of(j * MOE_ALIGN, MOE_ALIGN)).start()
            return carry

        def wait(j, carry):
            y_copy(0, 0).wait()
            return carry

        lax.fori_loop(used, total, fill, 0)
        lax.fori_loop(used, total, wait, 0)


def _moe(xs, item_e, item_r, item_n, n_blocks, w_gate, w_up, w_down):
    rows = xs.shape[0]
    D = 2 * xs.shape[1]
    n_items = item_e.shape[0]
    nf = EXPERT_DFF // MOE_TF
    assert nf >= 2 and MOE_SUB == 2 * MOE_ALIGN

    R = MOE_WEIGHT_BUFFERS
    return pl.pallas_call(
        _moe_body,
        grid_spec=pltpu.PrefetchScalarGridSpec(
            num_scalar_prefetch=4,
            grid=(n_items, nf),
            in_specs=[
                pl.BlockSpec(memory_space=pl.ANY),
                pl.BlockSpec(memory_space=pl.ANY),
                pl.BlockSpec(memory_space=pl.ANY),
                pl.BlockSpec(memory_space=pl.ANY),
            ],
            out_specs=pl.BlockSpec(memory_space=pl.ANY),
            scratch_shapes=[pltpu.VMEM((2, MOE_TM, D // 2), jnp.int32),
                            pltpu.VMEM((MOE_TM, D // 2), BF16), pltpu.VMEM((MOE_TM, D // 2), BF16),
                            pltpu.VMEM((MOE_TM, D), F32),
                            pltpu.VMEM((2, D, MOE_TF), BF16), pltpu.VMEM((2, D, MOE_TF), BF16),
                            pltpu.VMEM((2, MOE_TF, D), BF16),
                            pltpu.VMEM((R, D, MOE_TF), F32), pltpu.VMEM((R, D, MOE_TF), F32),
                            pltpu.VMEM((R, MOE_TF, D), F32),
                            pltpu.SemaphoreType.DMA((2,)), pltpu.SemaphoreType.DMA(()),
                            pltpu.SemaphoreType.DMA((3, R))],
        ),
        out_shape=jax.ShapeDtypeStruct((rows, D), F32),
        compiler_params=_cparams("arbitrary", "arbitrary"),
        name="moe_experts",
    )(item_e, item_r, item_n, n_blocks, xs, w_gate, w_up, w_down)


def _final_body(dcur_ref, dnext_ref, x2_ref, rt_ref, g_ref, ys_hbm, o_ref, ybuf, sems):
    i = pl.program_id(0)
    tb = x2_ref.shape[0]
    slot = i % 2

    def gather(d_ref, s):
        def body(t, carry):
            for k in range(2):
                pltpu.make_async_copy(ys_hbm.at[pl.ds(d_ref[0, 2 * t + k], 1), :],
                                      ybuf.at[s, k, pl.ds(t, 1), :], sems.at[s]).start()
            return carry

        lax.fori_loop(0, tb, body, 0, unroll=8)

    @pl.when(i == 0)
    def _():
        gather(dcur_ref, 0)

    @pl.when(i + 1 < pl.num_programs(0))
    def _():
        gather(dnext_ref, 1 - slot)

    for k in range(2):
        pltpu.make_async_copy(ys_hbm.at[pl.ds(0, tb), :], ybuf.at[slot, k], sems.at[slot]).wait()
    rt = rt_ref[...]
    x = (x2_ref[...] + rt[:, ROUTE_W1:ROUTE_W1 + 1] * ybuf[slot, 0]
         + rt[:, ROUTE_W2:ROUTE_W2 + 1] * ybuf[slot, 1])
    o_ref[...] = x * lax.rsqrt(jnp.mean(x * x, axis=-1, keepdims=True) + RMS_EPS) * g_ref[...]


def _final(x2, ys, dest, rt, g):
    T, D = x2.shape
    tb = ROW_TILE
    nb = T // tb
    dest3 = dest.reshape(nb, 1, 2 * tb)
    return pl.pallas_call(
        _final_body,
        grid=(nb,),
        in_specs=[pl.BlockSpec((None, 1, 2 * tb), lambda i: (i, 0, 0), memory_space=pltpu.SMEM),
                  pl.BlockSpec((None, 1, 2 * tb), lambda i: (jnp.minimum(i + 1, nb - 1), 0, 0),
                               memory_space=pltpu.SMEM),
                  pl.BlockSpec((tb, D), lambda i: (i, 0)),
                  pl.BlockSpec((tb, LANES), lambda i: (i, 0)),
                  pl.BlockSpec((1, D), lambda i: (0, 0)),
                  pl.BlockSpec(memory_space=pl.ANY)],
        out_specs=pl.BlockSpec((tb, D), lambda i: (i, 0)),
        out_shape=jax.ShapeDtypeStruct((T, D), F32),
        scratch_shapes=[pltpu.VMEM((2, 2, tb, D), F32), pltpu.SemaphoreType.DMA((2,))],
        compiler_params=_cparams("arbitrary"),
        name="combine_final_norm",
    )(dest3, dest3, x2, rt, g, ys)


def _moe_layout(e_flat):
    A = e_flat.shape[0]
    onehot = (e_flat[:, None] == jnp.arange(N_EXPERTS, dtype=jnp.int32)[None, :]).astype(jnp.int32)
    csum = jnp.cumsum(onehot, axis=0)
    counts = csum[-1]
    nblk = (counts + MOE_ALIGN - 1) // MOE_ALIGN
    blk0 = jnp.cumsum(nblk) - nblk
    dest = jnp.sum(onehot * (blk0[None, :] * MOE_ALIGN + csum - 1), axis=1).astype(jnp.int32)
    per_item = MOE_TM // MOE_ALIGN
    nitem = (nblk + per_item - 1) // per_item
    iend = jnp.cumsum(nitem)
    n_items = (A // MOE_ALIGN + N_EXPERTS + N_EXPERTS * (per_item - 1)) // per_item
    w = jnp.arange(n_items, dtype=jnp.int32)
    total = iend[-1]
    wv = jnp.minimum(w, total - 1)
    item_e = jnp.minimum(jnp.searchsorted(iend, wv, side='right'), N_EXPERTS - 1).astype(jnp.int32)
    sb = wv - (iend - nitem)[item_e]
    item_r = (blk0[item_e] + sb * per_item).astype(jnp.int32)
    item_n = jnp.where(w < total, jnp.minimum(per_item, nblk[item_e] - sb * per_item), 0).astype(jnp.int32)
    n_blocks = jnp.sum(nblk).astype(jnp.int32).reshape(1)
    lead = lambda a, v: jnp.concatenate([jnp.asarray(v, jnp.int32).reshape(1), a])
    return dest, lead(item_e, item_e[0]), lead(item_r, 0), lead(item_n, 0), n_blocks


def kernel(x, norm1_g, w_in, hg_lb_logits, hg_norm_g, cmp_pe_k, cmp_w1_k, cmp_w2_k, cmp_pe_v,
           cmp_w1_v, cmp_w2_v, rel_bias, w_out, norm2_g, w_router_group, b_router_group,
           w_router_expert, b_router_expert, w_expert_gate, w_expert_up, w_expert_down,
           final_norm_g):
    B, S, D = x.shape
    T = B * S
    assert w_in.shape[0] == 1, "single-layer block"
    x2d = x.reshape(T, D)

    lower = jax.nn.softmax(hg_lb_logits.astype(F32), axis=0)[0].reshape(HG_HEADS, 1, HG_DIM)

    w_main = w_in[0, :, :MAIN_COLS].astype(BF16)
    w_gate_cols = jnp.pad(w_in[0, :, MAIN_COLS:].astype(BF16), ((0, 0), (0, LANES - GATE_COLS)))
    proj_hg, proj_nsa, glog = _inproj(x2d, norm1_g[0].reshape(1, D), w_main, w_gate_cols,
                                      tm=1024, tn=512)

    y_hg = _hgrn(proj_hg, lower, hg_norm_g[0].reshape(1, HG_DIM), B, S, tb=1024)
    kcvc = _nsa_compress_all(proj_nsa, cmp_pe_k[0], cmp_w1_k[0], cmp_w2_k[0],
                             cmp_pe_v[0], cmp_w1_v[0], cmp_w2_v[0], B, S)
    y_nsa = _nsa(proj_nsa, glog, kcvc[0], kcvc[1], rel_bias, B, S)

    w_r = jnp.concatenate([w_router_group[0], w_router_expert[0]], axis=1)
    w_r = jnp.pad(w_r, ((0, 0), (0, LANES - w_r.shape[1])))
    wr_hi = w_r.astype(BF16)
    wr_lo = (w_r - wr_hi.astype(F32)).astype(BF16)
    b_r = jnp.concatenate([b_router_group[0], b_router_expert[0]])
    b_r = jnp.pad(b_r, (0, LANES - b_r.shape[0])).reshape(1, LANES)
    x2, h2, rt = _outproj(x2d, y_hg, y_nsa, w_out[0].astype(BF16), norm2_g[0].reshape(1, D),
                          wr_hi, wr_lo, b_r, tm=512)

    e_flat = rt[:, ROUTE_E1:ROUTE_E2 + 1].astype(jnp.int32).reshape(2 * T)
    dest, item_e, item_r, item_n, n_blocks = _moe_layout(e_flat)
    n_rows = (2 * T // MOE_ALIGN + N_EXPERTS) * MOE_ALIGN + MOE_TM
    xs = _dispatch(h2, dest, n_rows)
    ys = _moe(xs, item_e, item_r, item_n, n_blocks,
              w_expert_gate[0], w_expert_up[0], w_expert_down[0])

    out = _final(x2, ys, dest, rt, final_norm_g.reshape(1, D))
    return out.reshape(B, S, D)
```

```python
import functools
import math

import jax
import jax.numpy as jnp
import numpy as np
from jax import lax
from jax.experimental import pallas as pl
from jax.experimental.pallas import tpu as pltpu

F32 = jnp.float32
BF16 = jnp.bfloat16

HG_HEADS = 8
HG_DIM = 128
HG_WIDTH = HG_HEADS * HG_DIM
HG_CHUNK = 64
NSA_HEADS = 16
NSA_GROUPS = 4
NSA_HPG = NSA_HEADS // NSA_GROUPS
NSA_DH = 64
NSA_WIDTH = NSA_HEADS * NSA_DH
NSA_KV = NSA_GROUPS * NSA_DH
CMP_LEN = 32
CMP_STRIDE = 16
CMP_HIDDEN = 256
SLC_LEN = 64
SLC_TOPK = 16
WIN = 512
REL_BUCKETS = 32
REL_MAX_DIST = 128
N_EXPERT_GROUPS = 8
EXPERTS_PER_GROUP = 8
N_EXPERTS = N_EXPERT_GROUPS * EXPERTS_PER_GROUP
EXPERT_DFF = 1024
RMS_EPS = 1e-6
NEG_INF = -1e30
LOG2E = math.log2(math.e)
NSA_Q_SCALE = NSA_DH ** -0.5 * LOG2E
BIG = 1e9

MAIN_COLS = 4 * HG_WIDTH + NSA_WIDTH + 6 * NSA_KV
GATE_COLS = 3 * NSA_HEADS
LANES = 128
VMEM_LIMIT = 56 * 1024 * 1024

INPROJ_TM, INPROJ_TN = 1024, 512
HGRN_TB = 1024
OUTPROJ_TM = 512


def _cparams(*sem):
    return pltpu.CompilerParams(dimension_semantics=sem, vmem_limit_bytes=VMEM_LIMIT)


def _nt(a, b):
    return lax.dot_general(a, b, (((1,), (1,)), ((), ())), preferred_element_type=F32)


def _tn(a, b):
    return lax.dot_general(a, b, (((0,), (0,)), ((), ())), preferred_element_type=F32)


def _dot(a, b):
    return jnp.dot(a, b, preferred_element_type=F32)


def _inproj_body(x_ref, g_ref, w_ref, wg_ref, oh_ref, on_ref, og_ref, h_scr, *, hg_tiles, q_tiles):
    j = pl.program_id(1)

    @pl.when(j == 0)
    def _():
        x = x_ref[...]
        y = x * lax.rsqrt(jnp.mean(x * x, axis=-1, keepdims=True) + RMS_EPS) * g_ref[...]
        h_scr[...] = y.astype(BF16)
        og_ref[...] = _dot(h_scr[...], wg_ref[...])

    @pl.when(j < hg_tiles)
    def _():
        oh_ref[...] = _dot(h_scr[...], w_ref[...])

    @pl.when(j >= hg_tiles)
    def _():
        scale = jnp.where(j < hg_tiles + q_tiles, NSA_Q_SCALE, 1.0)
        on_ref[...] = (_dot(h_scr[...], w_ref[...]) * scale).astype(on_ref.dtype)


def _inproj(x2d, g, w_main, w_gate, tm, tn):
    T, D = x2d.shape
    N = w_main.shape[1]
    n_hg = 4 * HG_WIDTH
    hg_tiles = n_hg // tn
    return pl.pallas_call(
        functools.partial(_inproj_body, hg_tiles=hg_tiles, q_tiles=NSA_WIDTH // tn),
        grid=(T // tm, N // tn),
        in_specs=[
            pl.BlockSpec((tm, D), lambda i, j: (i, 0)),
            pl.BlockSpec((1, D), lambda i, j: (0, 0)),
            pl.BlockSpec((D, tn), lambda i, j: (0, j)),
            pl.BlockSpec((D, LANES), lambda i, j: (0, 0)),
        ],
        out_specs=[
            pl.BlockSpec((tm, tn), lambda i, j: (i, jnp.minimum(j, hg_tiles - 1))),
            pl.BlockSpec((tm, tn), lambda i, j: (i, jnp.maximum(j - hg_tiles, 0))),
            pl.BlockSpec((tm, LANES), lambda i, j: (i, 0)),
        ],
        out_shape=[jax.ShapeDtypeStruct((T, n_hg), F32),
                   jax.ShapeDtypeStruct((T, N - n_hg), BF16),
                   jax.ShapeDtypeStruct((T, LANES), F32)],
        scratch_shapes=[pltpu.VMEM((tm, D), BF16)],
        compiler_params=_cparams("arbitrary", "arbitrary"),
        name="inproj",
    )(x2d, g, w_main, w_gate)


def _hgrn_tables(C):
    t = np.arange(C)
    levels = []
    m = C // 2
    while m >= 1:
        levels.append(m)
        m //= 2
    rows = [np.tril(np.ones((C, C), np.float32)),
            np.triu(np.ones((C, C), np.float32), 1)]
    lev = np.full((C, C), -1, np.int32)
    for li, m in enumerate(levels):
        r = t % (2 * m)
        mid = t - r + m - 1
        M = np.zeros((C, C), np.float32)
        for i in range(C):
            if r[i] >= m:
                M[i, mid[i] + 1:i + 1] = 1.0
            else:
                M[i, i + 1:mid[i] + 1] = 1.0
        rows.append(M)
        same = (t[:, None] // (2 * m)) == (t[None, :] // (2 * m))
        lev[same & (r[:, None] >= m) & (r[None, :] < m)] = li
    lev[t[:, None] == t[None, :]] = len(levels)
    return np.concatenate(rows, 0), lev, len(levels)


HG_HEADS_PER_STEP = 8


def _hgrn_body(q_ref, f_ref, i_ref, g_ref, lb_ref, ng_ref, mst_ref, lev_ref, o_ref, st_ref,
               *, C, n_lev):
    @pl.when(pl.program_id(2) == 0)
    def _():
        st_ref[...] = jnp.zeros_like(st_ref)

    ng = ng_ref[...]
    mst = mst_ref[...]
    lev = lev_ref[...]
    n_chunks = q_ref.shape[0] // C

    def gates(rows, hb):
        cols = slice(hb * HG_DIM, (hb + 1) * HG_DIM)
        lb = lb_ref[hb]
        q = jax.nn.silu(q_ref[rows, cols])
        f = lb + (1.0 - lb) * jax.nn.sigmoid(f_ref[rows, cols])
        lf = jnp.log(f)
        lf0 = lf.astype(BF16)
        r1 = lf - lf0.astype(F32)
        lf1 = r1.astype(BF16)
        lf2 = (r1 - lf1.astype(F32)).astype(BF16)
        return q, 1.0 - f, i_ref[rows, cols].astype(BF16), jnp.concatenate([lf0, lf1, lf2], axis=0)

    def exponents(lf_a, lf_b):
        e = _dot(mst, jnp.concatenate([lf_a, lf_b], axis=1))
        return e[:, 0:HG_DIM], e[:, HG_DIM:2 * HG_DIM]

    def scores(q, k, gdec):
        a = jnp.where(lev == n_lev, _nt(q.astype(BF16), k.astype(BF16)), 0.0)
        for li in range(n_lev):
            g_m = gdec[(2 + li) * C:(3 + li) * C]
            a_m = _nt((q * g_m).astype(BF16), (k * g_m).astype(BF16))
            a = jnp.where(lev == li, a_m, a)
        return a.astype(BF16)

    def finish(rows, hb, q, k, v, gdec, a):
        cols = slice(hb * HG_DIM, (hb + 1) * HG_DIM)
        g_b = gdec[0:C]
        g_r = gdec[C:2 * C]
        st = st_ref[hb]
        o = _nt((q * g_b).astype(BF16), st.astype(BF16)) + _dot(a, v)
        g_last = g_b[C - 1:C, :]
        st_ref[hb] = st * g_last + _tn(v, (k * g_r).astype(BF16))
        o = o * lax.rsqrt(jnp.mean(o * o, axis=-1, keepdims=True) + RMS_EPS) * ng
        o_ref[rows, cols] = (o * jax.nn.silu(g_ref[rows, cols])).astype(o_ref.dtype)

    def chunk(c, carry):
        rows = pl.ds(pl.multiple_of(c * C, C), C)
        hs = range(HG_HEADS_PER_STEP)
        qkve = [gates(rows, hb) for hb in hs]
        e = []
        for hb in range(0, HG_HEADS_PER_STEP, 2):
            e.extend(exponents(qkve[hb][3], qkve[hb + 1][3]))
        gdec = [jnp.exp(x) for x in e]
        a = [scores(qkve[hb][0], qkve[hb][1], gdec[hb]) for hb in hs]
        for hb in hs:
            finish(rows, hb, qkve[hb][0], qkve[hb][1], qkve[hb][2], gdec[hb], a[hb])
        return carry

    lax.fori_loop(0, n_chunks, chunk, 0)


def _hgrn(proj, lb, ng, B, S, tb):
    C = HG_CHUNK
    mst, lev, n_lev = _hgrn_tables(C)
    mst = np.concatenate([mst] * 3, axis=1)
    nb = S // tb
    HB = HG_HEADS_PER_STEP
    hsteps = HG_HEADS // HB

    def col(group):
        return pl.BlockSpec((tb, HB * HG_DIM), lambda b, h, t: (b * nb + t, group * hsteps + h))

    return pl.pallas_call(
        functools.partial(_hgrn_body, C=C, n_lev=n_lev),
        grid=(B, hsteps, nb),
        in_specs=[
            col(0), col(1), col(2), col(3),
            pl.BlockSpec((HB, 1, HG_DIM), lambda b, h, t: (h, 0, 0)),
            pl.BlockSpec((1, HG_DIM), lambda b, h, t: (0, 0)),
            pl.BlockSpec(mst.shape, lambda b, h, t: (0, 0)),
            pl.BlockSpec(lev.shape, lambda b, h, t: (0, 0)),
        ],
        out_specs=pl.BlockSpec((tb, HB * HG_DIM), lambda b, h, t: (b * nb + t, h)),
        out_shape=jax.ShapeDtypeStruct((B * S, HG_WIDTH), BF16),
        scratch_shapes=[pltpu.VMEM((HB, HG_DIM, HG_DIM), F32)],
        compiler_params=_cparams("arbitrary", "arbitrary", "arbitrary"),
        name="hgrn2",
    )(proj, proj, proj, proj, lb, ng, jnp.asarray(mst, BF16), jnp.asarray(lev))


def _compress_body(h_ref, pe_ref, w1_ref, w2_ref, o_ref):
    hv = h_ref[...]
    half = hv.shape[1]
    nh = hv.shape[0]
    u = _dot((hv + pe_ref[0:1, :]).astype(BF16), w1_ref[0:half, :].astype(BF16))
    v = _dot((hv + pe_ref[1:2, :]).astype(BF16), w1_ref[half:2 * half, :].astype(BF16))
    pre = u + pltpu.roll(v, nh - 1, axis=0)
    o_ref[...] = _dot(jax.nn.silu(pre).astype(BF16), w2_ref[...].astype(BF16)).astype(o_ref.dtype)


def _compress(halves, pe, w1, w2):
    _, B, G, NH, HW = halves.shape
    return pl.pallas_call(
        _compress_body,
        grid=(2, B, G),
        in_specs=[
            pl.BlockSpec((None, None, None, NH, HW), lambda s, b, g: (s, b, g, 0, 0)),
            pl.BlockSpec((None, 2, HW), lambda s, b, g: (s, 0, 0)),
            pl.BlockSpec((None, 2 * HW, CMP_HIDDEN), lambda s, b, g: (s, 0, 0)),
            pl.BlockSpec((None, CMP_HIDDEN, NSA_DH), lambda s, b, g: (s, 0, 0)),
        ],
        out_specs=pl.BlockSpec((None, None, None, NH, NSA_DH), lambda s, b, g: (s, b, g, 0, 0)),
        out_shape=jax.ShapeDtypeStruct((2, B, G, NH, NSA_DH), BF16),
        compiler_params=_cparams("arbitrary", "arbitrary", "arbitrary"),
        name="nsa_compress",
    )(halves, pe, w1, w2)


def _nsa_compress_all(proj, pe_k, w1_k, w2_k, pe_v, w1_v, w2_v, B, S):
    NH = S // CMP_STRIDE
    c0 = NSA_WIDTH

    def halves(idx):
        t = proj[:, c0 + idx * NSA_KV:c0 + (idx + 1) * NSA_KV]
        t = t.reshape(B, NH, CMP_STRIDE, NSA_GROUPS, NSA_DH).transpose(0, 3, 1, 2, 4)
        return t.reshape(B, NSA_GROUPS, NH, CMP_STRIDE * NSA_DH)

    hw = CMP_STRIDE * NSA_DH
    return _compress(jnp.stack([halves(0), halves(1)]),
                     jnp.stack([pe_k.reshape(2, hw), pe_v.reshape(2, hw)]),
                     jnp.stack([w1_k, w1_v]), jnp.stack([w2_k, w2_v]))


NSA_TQ = 128


NSA_MASK_ROWS = 64
NSA_WIN_MASK_ROWS = 16
NSA_FAR = 4
NSA_SLC_PAD = (NSA_FAR - 1) * NSA_TQ


NSA_V_ROWS = NSA_DH + 16


def _softmax_cols(s):
    m = jnp.max(s, axis=0, keepdims=True)
    return m, jnp.exp2(s - m).astype(BF16)


def _normalise(acc):
    return acc[0:NSA_DH] * (1.0 / acc[NSA_DH:NSA_DH + 1])


NSA_GROUPS_PER_STEP = 4


def _nsa_body(qt_ref, gl_ref, kc_ref, vct_ref, ks_ref, vst_ref, kw_ref, vwt_ref,
              bc_ref, bw_ref, bn_ref, ovt_ref, o_ref, *, top_k, n_slc):
    TQ = NSA_TQ
    GG = NSA_GROUPS_PER_STEP
    i = pl.program_id(2)
    ovt = ovt_ref[...]
    jb = lax.broadcasted_iota(jnp.int32, (n_slc, TQ), 0)
    qblk = (i * TQ + lax.broadcasted_iota(jnp.int32, (n_slc, TQ), 1)) // SLC_LEN
    forced = (jb == 0) | (jb == qblk) | (jb == qblk - 1)
    future = jb > qblk
    near = pl.ds(pl.multiple_of(NSA_SLC_PAD + (i - 1) * TQ, TQ), 2 * TQ)

    def heads(a):
        return jnp.concatenate([a] * NSA_HPG, axis=1)

    def queries(g):
        qt = qt_ref[g]
        return jnp.concatenate([qt[h * NSA_DH:(h + 1) * NSA_DH, :] for h in range(NSA_HPG)],
                               axis=1)

    def compressed(g, s):
        m = jnp.max(s, axis=0, keepdims=True)
        p = jnp.where(s > 0.5 * NEG_INF, jnp.exp2(s - m), 0.0)
        acc = _dot(vct_ref[g], p.astype(BF16))
        inv_l = 1.0 / jnp.maximum(acc[NSA_DH:NSA_DH + 1], 1e-30)
        o_cmp = acc[0:NSA_DH] * inv_l
        pc = p * inv_l
        ps = pc[:, 0:TQ] + pc[:, TQ:2 * TQ] + pc[:, 2 * TQ:3 * TQ] + pc[:, 3 * TQ:4 * TQ]
        ps_hi = ps.astype(BF16)
        ps_lo = (ps - ps_hi.astype(F32)).astype(BF16)
        imp = _dot(ovt, ps_hi) + _dot(ovt, ps_lo)
        return o_cmp, jnp.where(forced, BIG, jnp.where(future, -BIG, imp))

    def window(g, s):
        _, p = _softmax_cols(s)
        return _normalise(_dot(vwt_ref[g, :, win], p))

    def select(imp, q4):
        slabs = []
        for v in range(n_slc // 8):
            slab = imp[8 * v:8 * v + 8, :]
            jbs = jb[8 * v:8 * v + 8, :]
            cnt = jnp.zeros((8, TQ), F32)
            for jp in range(n_slc):
                row = imp[jp:jp + 1, :]
                if jp < 8 * v:
                    beats = row >= slab
                elif jp >= 8 * v + 8:
                    beats = row > slab
                else:
                    beats = (row > slab) | ((row == slab) & (jbs > jp))
                cnt = cnt + jnp.where(beats, 1.0, 0.0)
            slabs.append(jnp.where(cnt < top_k, 0.0, 1.0))
        if n_slc < NSA_MASK_ROWS:
            slabs.append(jnp.ones((NSA_MASK_ROWS - n_slc, TQ), F32))
        unsel = jnp.concatenate(slabs, axis=0)
        return jnp.concatenate([heads(unsel.astype(BF16)), q4], axis=0)

    def near_tiles(g, s):
        m, p = _softmax_cols(s)
        return m, _dot(vst_ref[g, :, near], p)

    win = pl.ds(pl.multiple_of(i * TQ, TQ), WIN + TQ)
    pad_row = jnp.where(lax.broadcasted_iota(jnp.int32, (NSA_WIN_MASK_ROWS, NSA_HPG * TQ), 0) == 0,
                        1.0, 0.0).astype(BF16)
    gs = range(GG)
    q4 = [queries(g) for g in gs]
    nh = kc_ref.shape[1]
    cmp_rows = pl.ds(pl.multiple_of(nh - (TQ // CMP_STRIDE) * i, TQ // CMP_STRIDE), nh)
    s_cmp = [_dot(kc_ref[g], q4[g]) + bc_ref[g, cmp_rows, :] for g in gs]
    s_win = [_dot(kw_ref[g, win, :], jnp.concatenate([pad_row, q4[g]], axis=0)) + bw_ref[g] for g in gs]
    cmp_out = [compressed(g, s_cmp[g]) for g in gs]
    o_win = [window(g, s_win[g]) for g in gs]
    qa = [select(cmp_out[g][1], q4[g]) for g in gs]
    s_near = [_dot(ks_ref[g, near, :], qa[g]) + bn_ref[g] for g in gs]
    start = tuple(near_tiles(g, s_near[g]) for g in gs)

    n_far = i - 1
    n_chunks = jnp.maximum((n_far + NSA_FAR - 1) // NSA_FAR, 0)
    far0 = NSA_SLC_PAD + TQ * (n_far - NSA_FAR * n_chunks)

    def far_step(c, carry):
        keys = pl.ds(pl.multiple_of(far0 + c * NSA_FAR * TQ, TQ), NSA_FAR * TQ)
        out = []
        logits = [_dot(ks_ref[g, keys, :], qa[g]) for g in gs]
        for g in gs:
            m, acc = carry[g]
            s = logits[g]
            m_new = jnp.maximum(m, jnp.max(s, axis=0, keepdims=True))
            p = jnp.exp2(s - m_new).astype(BF16)
            acc = jnp.exp2(m - m_new) * acc + _dot(vst_ref[g, :, keys], p)
            out.append((m_new, acc))
        return tuple(out)

    far = lax.fori_loop(0, n_chunks, far_step, start)

    for g in gs:
        o_slc = _normalise(far[g][1])
        o_cmp = cmp_out[g][0]
        sg = jax.nn.sigmoid(gl_ref[g])
        outs = []
        for h in range(NSA_HPG):
            lanes = slice(h * TQ, (h + 1) * TQ)
            outs.append(sg[h:h + 1, :] * o_cmp[:, lanes]
                        + sg[NSA_HPG + h:NSA_HPG + h + 1, :] * o_slc[:, lanes]
                        + sg[2 * NSA_HPG + h:2 * NSA_HPG + h + 1, :] * o_win[g][:, lanes])
        width = NSA_HPG * NSA_DH
        o_ref[:, g * width:(g + 1) * width] = jnp.concatenate(outs, axis=0).T.astype(o_ref.dtype)


def _rel_bucket(dist):
    n = jnp.maximum(dist, 0)
    max_exact = REL_BUCKETS // 2
    nf = jnp.maximum(n, 1).astype(F32)
    large = max_exact + (jnp.log(nf / max_exact) / math.log(REL_MAX_DIST / max_exact)
                         * (REL_BUCKETS - max_exact)).astype(jnp.int32)
    large = jnp.minimum(large, REL_BUCKETS - 1)
    return jnp.where(n < max_exact, n, large)


def _bias_table(table, dist, valid):
    onehot = (_rel_bucket(jnp.asarray(dist, jnp.int32))[..., None]
              == jnp.arange(REL_BUCKETS, dtype=jnp.int32)).astype(F32)
    b = jnp.einsum('rmk,ghk->ghrm', onehot, table, precision=lax.Precision.HIGHEST)
    return jnp.where(jnp.asarray(valid), b, NEG_INF)


def _relayout_body(x_ref, ms_ref, qt_ref, ks_ref, vst_ref, kw_ref, vwt_ref, *, lead):
    TQ = NSA_TQ
    real = pl.program_id(1) >= lead
    x = x_ref[...]

    def columns(idx):
        lo = NSA_WIDTH + idx * NSA_KV
        return jnp.where(real, x[:, lo:lo + NSA_KV], 0.0)

    @pl.when(real)
    def _():
        qt = x[:, 0:NSA_WIDTH].astype(F32).T.astype(BF16)
        qt_ref[...] = qt.reshape(qt_ref.shape)

    ones = jnp.where((lax.broadcasted_iota(jnp.int32, (NSA_V_ROWS - NSA_DH, TQ), 0) == 0) & real,
                     1.0, 0.0).astype(BF16)
    pad_col = jnp.where((lax.broadcasted_iota(jnp.int32, (TQ, NSA_WIN_MASK_ROWS), 1) == 0) & ~real,
                        NEG_INF, 0.0).astype(BF16)
    ms = ms_ref[...]
    k_slc, k_win = columns(2), columns(4)
    v_slc = columns(3).astype(F32).T.astype(BF16)
    v_win = columns(5).astype(F32).T.astype(BF16)
    for g in range(NSA_GROUPS):
        d = slice(g * NSA_DH, (g + 1) * NSA_DH)
        ks_ref[g] = jnp.concatenate([ms, k_slc[:, d]], axis=1)
        kw_ref[g] = jnp.concatenate([pad_col, k_win[:, d]], axis=1)
        vst_ref[g] = jnp.concatenate([v_slc[d, :], ones], axis=0)
        vwt_ref[g] = jnp.concatenate([v_win[d, :], ones], axis=0)


def _nsa_relayout(proj, slc_masks, B, S):
    TQ = NSA_TQ
    G, DH, SP = NSA_GROUPS, NSA_DH, NSA_SLC_PAD
    nt = S // TQ
    lead = WIN // TQ
    slc_lead = SP // TQ
    tok = lambda u: jnp.maximum(u - lead, 0)
    slc = lambda u: jnp.maximum(u - (lead - slc_lead), 0)
    return pl.pallas_call(
        functools.partial(_relayout_body, lead=lead),
        grid=(B, lead + nt),
        in_specs=[pl.BlockSpec((TQ, proj.shape[1]), lambda b, u: (b * nt + tok(u), 0)),
                  pl.BlockSpec((TQ, NSA_MASK_ROWS), lambda b, u: (slc(u), 0))],
        out_specs=[
            pl.BlockSpec((None, G, NSA_HPG * DH, TQ), lambda b, u: (b, 0, 0, tok(u))),
            pl.BlockSpec((None, G, TQ, NSA_MASK_ROWS + DH), lambda b, u: (b, 0, slc(u), 0)),
            pl.BlockSpec((None, G, NSA_V_ROWS, TQ), lambda b, u: (b, 0, 0, slc(u))),
            pl.BlockSpec((None, G, TQ, NSA_WIN_MASK_ROWS + DH), lambda b, u: (b, 0, u, 0)),
            pl.BlockSpec((None, G, NSA_V_ROWS, TQ), lambda b, u: (b, 0, 0, u)),
        ],
        out_shape=[
            jax.ShapeDtypeStruct((B, G, NSA_HPG * DH, S), BF16),
            jax.ShapeDtypeStruct((B, G, S + SP, NSA_MASK_ROWS + DH), BF16),
            jax.ShapeDtypeStruct((B, G, NSA_V_ROWS, S + SP), BF16),
            jax.ShapeDtypeStruct((B, G, S + WIN, NSA_WIN_MASK_ROWS + DH), BF16),
            jax.ShapeDtypeStruct((B, G, NSA_V_ROWS, S + WIN), BF16),
        ],
        compiler_params=_cparams("arbitrary", "arbitrary"),
        name="nsa_relayout",
    )(proj, slc_masks)


def _nsa(proj, glog, kc, vc, rel_bias, B, S):
    TQ = NSA_TQ
    G, HPG, DH = NSA_GROUPS, NSA_HPG, NSA_DH
    NH = S // CMP_STRIDE
    n_cmp = (S - CMP_LEN) // CMP_STRIDE + 1
    n_slc = S // SLC_LEN
    top_k = min(SLC_TOPK, n_slc)
    nt = S // TQ
    c0 = NSA_WIDTH

    MR = NSA_MASK_ROWS
    assert n_slc % 8 == 0 and (n_slc < MR or (n_slc == MR and top_k < n_slc))

    SP = NSA_SLC_PAD
    pos = np.arange(SP + S) - SP
    cols = np.zeros((SP + S, MR), np.float32)
    cols[:, :n_slc] = ((pos // SLC_LEN)[:, None] == np.arange(n_slc)[None, :]) & (pos >= 0)[:, None]
    if n_slc < MR:
        cols[:, n_slc] = pos < 0
    else:
        cols[pos < 0, :] = 1.0
    qt, ks, vst, kw, vwt = _nsa_relayout(proj, jnp.asarray(cols * NEG_INF, BF16), B, S)
    kcb = kc
    ones = jnp.zeros((NSA_V_ROWS - DH, NH), BF16).at[0].set(1.0)
    vct = jnp.concatenate([vc.transpose(0, 1, 3, 2), jnp.broadcast_to(ones, (B, G) + ones.shape)],
                          axis=2)

    table = rel_bias.T.reshape(G, HPG, REL_BUCKETS) * LOG2E
    far = table[:, :, REL_BUCKETS - 1]
    r = np.arange(TQ)[:, None]

    def keys_major(b):
        return b.transpose(0, 3, 1, 2).reshape(G, b.shape[3], HPG * TQ)

    d_c = r - (CMP_LEN - 1) - CMP_STRIDE * (np.arange(2 * NH)[None, :] - NH)
    bias_cmp = keys_major(_bias_table(table, d_c, d_c >= 0))
    d_w = r + WIN - np.arange(WIN + TQ)[None, :]
    bias_win = keys_major(_bias_table(table, d_w, (d_w >= 0) & (d_w < WIN)))
    d_n = r + TQ - np.arange(2 * TQ)[None, :]
    bias_near = keys_major(_bias_table(table - far[:, :, None], d_n, d_n >= 0))

    ci = np.arange(NH)[None, :] * CMP_STRIDE
    sj = np.arange(n_slc)[:, None] * SLC_LEN
    overlap_t = ((ci <= sj + SLC_LEN - 1) & (ci + CMP_LEN - 1 >= sj) & (np.arange(NH)[None, :] < n_cmp))

    gate_rows = 16
    glog_g = glog[:, :GATE_COLS].reshape(B * S, 3, G, HPG).transpose(2, 1, 3, 0).reshape(G, 3 * HPG, B * S)
    glog_g = jnp.pad(glog_g, ((0, 0), (0, gate_rows - 3 * HPG), (0, 0)))

    GG = NSA_GROUPS_PER_STEP
    whole = lambda shape: pl.BlockSpec((None, GG) + shape, lambda b, g, i: (b, g, 0, 0))
    tile = lambda k: pl.BlockSpec((GG, k, HPG * TQ), lambda b, g, i: (g, 0, 0))
    return pl.pallas_call(
        functools.partial(_nsa_body, top_k=top_k, n_slc=n_slc),
        grid=(B, G // GG, nt),
        in_specs=[
            pl.BlockSpec((None, GG, HPG * DH, TQ), lambda b, g, i: (b, g, 0, i)),
            pl.BlockSpec((GG, gate_rows, TQ), lambda b, g, i: (g, 0, b * nt + i)),
            whole((NH, DH)), whole((NSA_V_ROWS, NH)),
            whole((S + SP, MR + DH)), whole((NSA_V_ROWS, S + SP)),
            whole((S + WIN, NSA_WIN_MASK_ROWS + DH)), whole((NSA_V_ROWS, S + WIN)),
            tile(2 * NH), tile(WIN + TQ), tile(2 * TQ),
            pl.BlockSpec((n_slc, NH), lambda b, g, i: (0, 0)),
        ],
        out_specs=pl.BlockSpec((TQ, GG * HPG * DH), lambda b, g, i: (b * nt + i, g)),
        out_shape=jax.ShapeDtypeStruct((B * S, NSA_WIDTH), BF16),
        compiler_params=_cparams("arbitrary", "arbitrary", "arbitrary"),
        name="nsa_attention",
    )(qt, glog_g, kcb, vct, ks, vst, kw, vwt, bias_cmp, bias_win, bias_near,
      jnp.asarray(overlap_t, BF16))


ROUTE_E1, ROUTE_E2, ROUTE_W1, ROUTE_W2 = 0, 1, 2, 3


def _outproj_body(x_ref, yh_ref, yn_ref, wo_ref, g2_ref, wrh_ref, wrl_ref, br_ref,
                  x2_ref, h2_ref, rt_ref):
    x2 = (x_ref[...] + _dot(yh_ref[...], wo_ref[0:HG_WIDTH, :])
          + _dot(yn_ref[...], wo_ref[HG_WIDTH:HG_WIDTH + NSA_WIDTH, :]))
    x2_ref[...] = x2
    h = x2 * lax.rsqrt(jnp.mean(x2 * x2, axis=-1, keepdims=True) + RMS_EPS) * g2_ref[...]
    hb = h.astype(BF16)
    hbf = hb.astype(F32)
    half = h.shape[1] // 2
    bits = pltpu.bitcast(hbf, jnp.int32)
    h2_ref[...] = (lax.shift_right_logical(bits[:, :half], 16)
                   | (bits[:, half:] & jnp.int32(-65536)))
    hl = (h - hbf).astype(BF16)
    wrh = wrh_ref[...]
    lg = _dot(hb, wrh) + _dot(hl, wrh) + _dot(hb, wrl_ref[...]) + br_ref[...]
    lane = lax.broadcasted_iota(jnp.int32, lg.shape, 1)
    first = lambda hit: jnp.min(jnp.where(hit, lane, LANES), axis=-1, keepdims=True)
    lgg = jnp.where(lane < N_EXPERT_GROUPS, lg, NEG_INF)
    mg = jnp.max(lgg, axis=-1, keepdims=True)
    p_top = 1.0 / jnp.sum(jnp.exp(lgg - mg), axis=-1, keepdims=True)
    lo = N_EXPERT_GROUPS + EXPERTS_PER_GROUP * first(lgg == mg)
    le = jnp.where((lane >= lo) & (lane < lo + EXPERTS_PER_GROUP), lg, NEG_INF)
    v1 = jnp.max(le, axis=-1, keepdims=True)
    i1 = first(le == v1)
    le = jnp.where(lane == i1, NEG_INF, le)
    v2 = jnp.max(le, axis=-1, keepdims=True)
    i2 = first(le == v2)
    e21 = jnp.exp(v2 - v1)
    w1 = p_top / (1.0 + e21)
    rt = jnp.where(lane == ROUTE_E1, (i1 - N_EXPERT_GROUPS).astype(F32), 0.0)
    rt = jnp.where(lane == ROUTE_E2, (i2 - N_EXPERT_GROUPS).astype(F32), rt)
    rt = jnp.where(lane == ROUTE_W1, w1, rt)
    rt_ref[...] = jnp.where(lane == ROUTE_W2, w1 * e21, rt)


def _outproj(x2d, y_hg, y_nsa, w_out, g2, wr_hi, wr_lo, b_r, tm):
    T, D = x2d.shape
    row = lambda n: pl.BlockSpec((tm, n), lambda i: (i, 0))
    full = lambda a: pl.BlockSpec(a.shape, lambda i: (0, 0))
    return pl.pallas_call(
        _outproj_body,
        grid=(T // tm,),
        in_specs=[row(D), row(HG_WIDTH), row(NSA_WIDTH), full(w_out), full(g2),
                  full(wr_hi), full(wr_lo), full(b_r)],
        out_specs=[row(D), row(D // 2), row(LANES)],
        out_shape=[jax.ShapeDtypeStruct((T, D), F32), jax.ShapeDtypeStruct((T, D // 2), jnp.int32),
                   jax.ShapeDtypeStruct((T, LANES), F32)],
        compiler_params=_cparams("arbitrary"),
        name="outproj_router",
    )(x2d, y_hg, y_nsa, w_out, g2, wr_hi, wr_lo, b_r)


ROW_TILE = 512


def _dispatch_body(dest_ref, h_ref, xs0_hbm, xs_hbm, sem):
    del xs0_hbm
    tb = h_ref.shape[0]

    def start(t, carry):
        for k in range(2):
            pltpu.make_async_copy(h_ref.at[pl.ds(t, 1), :],
                                  xs_hbm.at[pl.ds(dest_ref[0, 2 * t + k], 1), :], sem).start()
        return carry

    lax.fori_loop(0, tb, start, 0, unroll=8)
    for _ in range(2):
        pltpu.make_async_copy(h_ref, xs_hbm.at[pl.ds(0, tb), :], sem).wait()


def _dispatch(h2p, dest, n_rows):
    T, W = h2p.shape
    tb = ROW_TILE
    return pl.pallas_call(
        _dispatch_body,
        grid=(T // tb,),
        in_specs=[pl.BlockSpec((None, 1, 2 * tb), lambda i: (i, 0, 0), memory_space=pltpu.SMEM),
                  pl.BlockSpec((tb, W), lambda i: (i, 0)),
                  pl.BlockSpec(memory_space=pl.ANY)],
        out_specs=pl.BlockSpec(memory_space=pl.ANY),
        out_shape=jax.ShapeDtypeStruct((n_rows, W), jnp.int32),
        scratch_shapes=[pltpu.SemaphoreType.DMA(())],
        input_output_aliases={2: 0},
        compiler_params=_cparams("arbitrary"),
        name="moe_dispatch",
    )(dest.reshape(T // tb, 1, 2 * tb), h2p, jnp.zeros((n_rows, W), jnp.int32))


MOE_ALIGN = 128
MOE_TM = 1024
MOE_SUB = 256
MOE_TF = 256
MOE_CAST_PIECES = 4
MOE_WEIGHT_BUFFERS = 3


def _moe_body(ie_ref, ir_ref, in_ref, nb_ref, xs_hbm, wg_hbm, wu_hbm, wd_hbm, y_hbm,
              xbuf, xlo, xhi, acc, wgb, wub, wdb, wgf, wuf, wdf, sem_in, sem_out, sem_w):
    w = pl.program_id(0)
    f = pl.program_id(1)
    n_w = pl.num_programs(0)
    n_f = pl.num_programs(1)
    last_f = n_f - 1
    ring = wgf.shape[0]
    step = w * n_f + f

    def weight_copies(k):
        item = jnp.minimum(k // n_f, n_w - 1)
        wanted = (k < n_w * n_f) & (in_ref[item] > 0)
        e = ie_ref[item]
        cols = pl.ds(pl.multiple_of((k % n_f) * MOE_TF, MOE_TF), MOE_TF)
        s = k % ring
        return wanted, (pltpu.make_async_copy(wg_hbm.at[e, :, cols], wgf.at[s], sem_w.at[0, s]),
                        pltpu.make_async_copy(wu_hbm.at[e, :, cols], wuf.at[s], sem_w.at[1, s]),
                        pltpu.make_async_copy(wd_hbm.at[e, cols, :], wdf.at[s], sem_w.at[2, s]))

    def request_weights(k):
        wanted, copies = weight_copies(k)

        @pl.when(wanted)
        def _():
            for c in copies:
                c.start()

    @pl.when(step == 0)
    def _():
        for k in range(1, ring):
            request_weights(k)

    request_weights(step + ring)
    cast_next, arriving = weight_copies(step + 1)

    @pl.when(cast_next)
    def _():
        for c in arriving:
            c.wait()

    nslot = (step + 1) % ring
    nsub = in_ref[w]
    row0 = pl.multiple_of(ir_ref[w] * MOE_ALIGN, MOE_ALIGN)
    slot = w % 2
    per_mm = MOE_SUB // MOE_ALIGN
    n_full = nsub // per_mm
    n_iter = n_full + nsub % per_mm
    half = xlo.shape[1]
    cur = step % 2

    def cast_piece(p):
        rk = wgf.shape[1] // MOE_CAST_PIECES
        rows = pl.ds(pl.multiple_of(p * rk, rk), rk)
        wgb[1 - cur, rows, :] = wgf[nslot, rows, :].astype(BF16)
        wub[1 - cur, rows, :] = wuf[nslot, rows, :].astype(BF16)
        rf = wdf.shape[1] // MOE_CAST_PIECES
        rows = pl.ds(pl.multiple_of(p * rf, rf), rf)
        wdb[1 - cur, rows, :] = wdf[nslot, rows, :].astype(BF16)

    def x_copy(item, s):
        r = pl.multiple_of(ir_ref[item] * MOE_ALIGN, MOE_ALIGN)
        return pltpu.make_async_copy(xs_hbm.at[pl.ds(r, MOE_TM), :], xbuf.at[s], sem_in.at[s])

    def y_copy(r, dst_row):
        return pltpu.make_async_copy(acc.at[pl.ds(r, MOE_ALIGN), :],
                                     y_hbm.at[pl.ds(dst_row, MOE_ALIGN), :], sem_out)

    def for_row_blocks(fn):
        def body(j, carry):
            fn(pl.multiple_of(j * MOE_SUB, MOE_SUB), MOE_SUB, j)
            return carry

        lax.fori_loop(0, n_full, body, 0)

        @pl.when(nsub % per_mm == 1)
        def _():
            fn(pl.multiple_of(n_full * MOE_SUB, MOE_ALIGN), MOE_ALIGN, n_full)

    @pl.when((f == 0) & (nsub > 0))
    def _():
        @pl.when(in_ref[jnp.maximum(w - 1, 0)] == 0)
        def _():
            x_copy(w, slot).start()

        x_copy(w, slot).wait()

        def unpack(r, size, _):
            bits = xbuf[slot, pl.ds(r, size), :]
            xlo[pl.ds(r, size), :] = pltpu.bitcast(lax.shift_left(bits, 16), F32).astype(BF16)
            xhi[pl.ds(r, size), :] = pltpu.bitcast(bits & jnp.int32(-65536), F32).astype(BF16)

        for_row_blocks(unpack)

    def experts(first, last):
        def block(r, size, it):
            rows = pl.ds(r, size)
            lo = xlo[rows, :]
            hi = xhi[rows, :]
            hg = _dot(lo, wgb[cur, 0:half, :]) + _dot(hi, wgb[cur, half:2 * half, :])
            hu = _dot(lo, wub[cur, 0:half, :]) + _dot(hi, wub[cur, half:2 * half, :])
            y = _dot((jax.nn.silu(hg) * hu).astype(BF16), wdb[cur])
            if first:
                acc[rows, :] = y
            else:
                acc[rows, :] += y
            if last:
                for k in range(size // MOE_ALIGN):
                    y_copy(r + k * MOE_ALIGN, row0 + r + k * MOE_ALIGN).start()

            @pl.when(cast_next & (it < MOE_CAST_PIECES))
            def _():
                cast_piece(it)

        for_row_blocks(block)

        @pl.when(cast_next)
        def _():
            lax.fori_loop(jnp.minimum(n_iter, MOE_CAST_PIECES), MOE_CAST_PIECES,
                          lambda p, c: (cast_piece(p), c)[1], 0)

    @pl.when((nsub == 0) & cast_next)
    def _():
        lax.fori_loop(0, MOE_CAST_PIECES, lambda p, c: (cast_piece(p), c)[1], 0)

    @pl.when((f == 0) & (nsub > 0))
    def _():
        experts(True, False)

    @pl.when((f > 0) & (f < last_f) & (nsub > 0))
    def _():
        experts(False, False)

    @pl.when((f == last_f) & (nsub > 0))
    def _():
        nxt = jnp.minimum(w + 1, n_w - 1)

        @pl.when((w + 1 < n_w) & (in_ref[nxt] > 0))
        def _():
            x_copy(nxt, 1 - slot).start()

        experts(False, True)

        def wait(j, carry):
            y_copy(0, 0).wait()
            return carry

        lax.fori_loop(0, nsub, wait, 0)

    @pl.when((f == last_f) & (w == n_w - 1))
    def _():
        acc[0:MOE_ALIGN, :] = jnp.zeros((MOE_ALIGN, acc.shape[1]), F32)
        used = nb_ref[0]
        total = y_hbm.shape[0] // MOE_ALIGN

        def fill(j, carry):
            y_copy(0, pl.multiple_of(j * MOE_ALIGN, MOE_ALIGN)).start()
            return carry

        def wait(j, carry):
            y_copy(0, 0).wait()
            return carry

        lax.fori_loop(used, total, fill, 0)
        lax.fori_loop(used, total, wait, 0)


def _moe(xs, item_e, item_r, item_n, n_blocks, w_gate, w_up, w_down):
    rows = xs.shape[0]
    D = 2 * xs.shape[1]
    n_items = item_e.shape[0]
    nf = EXPERT_DFF // MOE_TF
    assert nf >= 2 and MOE_SUB == 2 * MOE_ALIGN

    R = MOE_WEIGHT_BUFFERS
    return pl.pallas_call(
        _moe_body,
        grid_spec=pltpu.PrefetchScalarGridSpec(
            num_scalar_prefetch=4,
            grid=(n_items, nf),
            in_specs=[
                pl.BlockSpec(memory_space=pl.ANY),
                pl.BlockSpec(memory_space=pl.ANY),
                pl.BlockSpec(memory_space=pl.ANY),
                pl.BlockSpec(memory_space=pl.ANY),
            ],
            out_specs=pl.BlockSpec(memory_space=pl.ANY),
            scratch_shapes=[pltpu.VMEM((2, MOE_TM, D // 2), jnp.int32),
                            pltpu.VMEM((MOE_TM, D // 2), BF16), pltpu.VMEM((MOE_TM, D // 2), BF16),
                            pltpu.VMEM((MOE_TM, D), F32),
                            pltpu.VMEM((2, D, MOE_TF), BF16), pltpu.VMEM((2, D, MOE_TF), BF16),
                            pltpu.VMEM((2, MOE_TF, D), BF16),
                            pltpu.VMEM((R, D, MOE_TF), F32), pltpu.VMEM((R, D, MOE_TF), F32),
                            pltpu.VMEM((R, MOE_TF, D), F32),
                            pltpu.SemaphoreType.DMA((2,)), pltpu.SemaphoreType.DMA(()),
                            pltpu.SemaphoreType.DMA((3, R))],
        ),
        out_shape=jax.ShapeDtypeStruct((rows, D), F32),
        compiler_params=_cparams("arbitrary", "arbitrary"),
        name="moe_experts",
    )(item_e, item_r, item_n, n_blocks, xs, w_gate, w_up, w_down)


def _final_body(dcur_ref, dnext_ref, x2_ref, rt_ref, g_ref, ys_hbm, o_ref, ybuf, sems):
    i = pl.program_id(0)
    tb = x2_ref.shape[0]
    slot = i % 2

    def gather(d_ref, s):
        def body(t, carry):
            for k in range(2):
                pltpu.make_async_copy(ys_hbm.at[pl.ds(d_ref[0, 2 * t + k], 1), :],
                                      ybuf.at[s, k, pl.ds(t, 1), :], sems.at[s]).start()
            return carry

        lax.fori_loop(0, tb, body, 0, unroll=8)

    @pl.when(i == 0)
    def _():
        gather(dcur_ref, 0)

    @pl.when(i + 1 < pl.num_programs(0))
    def _():
        gather(dnext_ref, 1 - slot)

    for k in range(2):
        pltpu.make_async_copy(ys_hbm.at[pl.ds(0, tb), :], ybuf.at[slot, k], sems.at[slot]).wait()
    rt = rt_ref[...]
    x = (x2_ref[...] + rt[:, ROUTE_W1:ROUTE_W1 + 1] * ybuf[slot, 0]
         + rt[:, ROUTE_W2:ROUTE_W2 + 1] * ybuf[slot, 1])
    o_ref[...] = x * lax.rsqrt(jnp.mean(x * x, axis=-1, keepdims=True) + RMS_EPS) * g_ref[...]


def _final(x2, ys, dest, rt, g):
    T, D = x2.shape
    tb = ROW_TILE
    nb = T // tb
    dest3 = dest.reshape(nb, 1, 2 * tb)
    return pl.pallas_call(
        _final_body,
        grid=(nb,),
        in_specs=[pl.BlockSpec((None, 1, 2 * tb), lambda i: (i, 0, 0), memory_space=pltpu.SMEM),
                  pl.BlockSpec((None, 1, 2 * tb), lambda i: (jnp.minimum(i + 1, nb - 1), 0, 0),
                               memory_space=pltpu.SMEM),
                  pl.BlockSpec((tb, D), lambda i: (i, 0)),
                  pl.BlockSpec((tb, LANES), lambda i: (i, 0)),
                  pl.BlockSpec((1, D), lambda i: (0, 0)),
                  pl.BlockSpec(memory_space=pl.ANY)],
        out_specs=pl.BlockSpec((tb, D), lambda i: (i, 0)),
        out_shape=jax.ShapeDtypeStruct((T, D), F32),
        scratch_shapes=[pltpu.VMEM((2, 2, tb, D), F32), pltpu.SemaphoreType.DMA((2,))],
        compiler_params=_cparams("arbitrary"),
        name="combine_final_norm",
    )(dest3, dest3, x2, rt, g, ys)


def _moe_layout(e_flat):
    A = e_flat.shape[0]
    onehot = (e_flat[:, None] == jnp.arange(N_EXPERTS, dtype=jnp.int32)[None, :]).astype(jnp.int32)
    csum = jnp.cumsum(onehot, axis=0)
    counts = csum[-1]
    nblk = (counts + MOE_ALIGN - 1) // MOE_ALIGN
    blk0 = jnp.cumsum(nblk) - nblk
    dest = jnp.sum(onehot * (blk0[None, :] * MOE_ALIGN + csum - 1), axis=1).astype(jnp.int32)
    per_item = MOE_TM // MOE_ALIGN
    nitem = (nblk + per_item - 1) // per_item
    iend = jnp.cumsum(nitem)
    n_items = (A // MOE_ALIGN + N_EXPERTS + N_EXPERTS * (per_item - 1)) // per_item
    w = jnp.arange(n_items, dtype=jnp.int32)
    total = iend[-1]
    wv = jnp.minimum(w, total - 1)
    item_e = jnp.minimum(jnp.searchsorted(iend, wv, side='right'), N_EXPERTS - 1).astype(jnp.int32)
    sb = wv - (iend - nitem)[item_e]
    item_r = (blk0[item_e] + sb * per_item).astype(jnp.int32)
    item_n = jnp.where(w < total, jnp.minimum(per_item, nblk[item_e] - sb * per_item), 0).astype(jnp.int32)
    n_blocks = jnp.sum(nblk).astype(jnp.int32).reshape(1)
    lead = lambda a, v: jnp.concatenate([jnp.asarray(v, jnp.int32).reshape(1), a])
    return dest, lead(item_e, item_e[0]), lead(item_r, 0), lead(item_n, 0), n_blocks


def kernel(x, norm1_g, w_in, hg_lb_logits, hg_norm_g, cmp_pe_k, cmp_w1_k, cmp_w2_k, cmp_pe_v,
           cmp_w1_v, cmp_w2_v, rel_bias, w_out, norm2_g, w_router_group, b_router_group,
           w_router_expert, b_router_expert, w_expert_gate, w_expert_up, w_expert_down,
           final_norm_g):
    B, S, D = x.shape
    T = B * S
    assert w_in.shape[0] == 1, "single-layer block"
    x2d = x.reshape(T, D)

    lower = jax.nn.softmax(hg_lb_logits.astype(F32), axis=0)[0].reshape(HG_HEADS, 1, HG_DIM)

    w_main = w_in[0, :, :MAIN_COLS].astype(BF16)
    w_gate_cols = jnp.pad(w_in[0, :, MAIN_COLS:].astype(BF16), ((0, 0), (0, LANES - GATE_COLS)))
    proj_hg, proj_nsa, glog = _inproj(x2d, norm1_g[0].reshape(1, D), w_main, w_gate_cols,
                                      tm=INPROJ_TM, tn=INPROJ_TN)

    y_hg = _hgrn(proj_hg, lower, hg_norm_g[0].reshape(1, HG_DIM), B, S, tb=HGRN_TB)
    kcvc = _nsa_compress_all(proj_nsa, cmp_pe_k[0], cmp_w1_k[0], cmp_w2_k[0],
                             cmp_pe_v[0], cmp_w1_v[0], cmp_w2_v[0], B, S)
    y_nsa = _nsa(proj_nsa, glog, kcvc[0], kcvc[1], rel_bias, B, S)

    w_r = jnp.concatenate([w_router_group[0], w_router_expert[0]], axis=1)
    w_r = jnp.pad(w_r, ((0, 0), (0, LANES - w_r.shape[1])))
    wr_hi = w_r.astype(BF16)
    wr_lo = (w_r - wr_hi.astype(F32)).astype(BF16)
    b_r = jnp.concatenate([b_router_group[0], b_router_expert[0]])
    b_r = jnp.pad(b_r, (0, LANES - b_r.shape[0])).reshape(1, LANES)
    x2, h2, rt = _outproj(x2d, y_hg, y_nsa, w_out[0].astype(BF16), norm2_g[0].reshape(1, D),
                          wr_hi, wr_lo, b_r, tm=OUTPROJ_TM)

    e_flat = rt[:, ROUTE_E1:ROUTE_E2 + 1].astype(jnp.int32).reshape(2 * T)
    dest, item_e, item_r, item_n, n_blocks = _moe_layout(e_flat)
    n_rows = (2 * T // MOE_ALIGN + N_EXPERTS) * MOE_ALIGN + MOE_TM
    xs = _dispatch(h2, dest, n_rows)
    ys = _moe(xs, item_e, item_r, item_n, n_blocks,
              w_expert_gate[0], w_expert_up[0], w_expert_down[0])

    out = _final(x2, ys, dest, rt, final_norm_g.reshape(1, D))
    return out.reshape(B, S, D)
```

```python
import functools
import math

import jax
import jax.numpy as jnp
import numpy as np
from jax import lax
from jax.experimental import pallas as pl
from jax.experimental.pallas import tpu as pltpu

F32 = jnp.float32
BF16 = jnp.bfloat16

HG_HEADS = 8
HG_DIM = 128
HG_WIDTH = HG_HEADS * HG_DIM
HG_CHUNK = 64
NSA_HEADS = 16
NSA_GROUPS = 4
NSA_HPG = NSA_HEADS // NSA_GROUPS
NSA_DH = 64
NSA_WIDTH = NSA_HEADS * NSA_DH
NSA_KV = NSA_GROUPS * NSA_DH
CMP_LEN = 32
CMP_STRIDE = 16
CMP_HIDDEN = 256
SLC_LEN = 64
SLC_TOPK = 16
WIN = 512
REL_BUCKETS = 32
REL_MAX_DIST = 128
N_EXPERT_GROUPS = 8
EXPERTS_PER_GROUP = 8
N_EXPERTS = N_EXPERT_GROUPS * EXPERTS_PER_GROUP
EXPERT_DFF = 1024
RMS_EPS = 1e-6
NEG_INF = -1e30
LOG2E = math.log2(math.e)
NSA_Q_SCALE = NSA_DH ** -0.5 * LOG2E
BIG = 1e9

MAIN_COLS = 4 * HG_WIDTH + NSA_WIDTH + 6 * NSA_KV
GATE_COLS = 3 * NSA_HEADS
LANES = 128
VMEM_LIMIT = 56 * 1024 * 1024

INPROJ_TM, INPROJ_TN = 1024, 512
HGRN_TB = 1024
OUTPROJ_TM = 512


def _cparams(*sem):
    return pltpu.CompilerParams(dimension_semantics=sem, vmem_limit_bytes=VMEM_LIMIT)


def _nt(a, b):
    return lax.dot_general(a, b, (((1,), (1,)), ((), ())), preferred_element_type=F32)


def _tn(a, b):
    return lax.dot_general(a, b, (((0,), (0,)), ((), ())), preferred_element_type=F32)


def _dot(a, b):
    return jnp.dot(a, b, preferred_element_type=F32)


def _pack_bf16_pairs(x):
    half = x.shape[1] // 2
    bits = pltpu.bitcast(x.astype(BF16).astype(F32), jnp.int32)
    return lax.shift_right_logical(bits[:, :half], 16) | (bits[:, half:] & jnp.int32(-65536))


def _unpack_bf16_pairs(bits):
    return (pltpu.bitcast(lax.shift_left(bits, 16), F32),
            pltpu.bitcast(bits & jnp.int32(-65536), F32))


def _inproj_body(x_ref, g_ref, w_ref, wg_ref, oh_ref, on_ref, og_ref, h_scr, *, hg_tiles, q_tiles):
    j = pl.program_id(1)

    @pl.when(j == 0)
    def _():
        x = x_ref[...]
        y = x * lax.rsqrt(jnp.mean(x * x, axis=-1, keepdims=True) + RMS_EPS) * g_ref[...]
        h_scr[...] = y.astype(BF16)
        og_ref[...] = _dot(h_scr[...], wg_ref[...])

    @pl.when(j < hg_tiles)
    def _():
        oh_ref[...] = _dot(h_scr[...], w_ref[...])

    @pl.when(j >= hg_tiles)
    def _():
        scale = jnp.where(j < hg_tiles + q_tiles, NSA_Q_SCALE, 1.0)
        on_ref[...] = (_dot(h_scr[...], w_ref[...]) * scale).astype(on_ref.dtype)


def _inproj(x2d, g, w_main, w_gate, tm, tn):
    T, D = x2d.shape
    N = w_main.shape[1]
    n_hg = 4 * HG_WIDTH
    hg_tiles = n_hg // tn
    return pl.pallas_call(
        functools.partial(_inproj_body, hg_tiles=hg_tiles, q_tiles=NSA_WIDTH // tn),
        grid=(T // tm, N // tn),
        in_specs=[
            pl.BlockSpec((tm, D), lambda i, j: (i, 0)),
            pl.BlockSpec((1, D), lambda i, j: (0, 0)),
            pl.BlockSpec((D, tn), lambda i, j: (0, j)),
            pl.BlockSpec((D, LANES), lambda i, j: (0, 0)),
        ],
        out_specs=[
            pl.BlockSpec((tm, tn), lambda i, j: (i, jnp.minimum(j, hg_tiles - 1))),
            pl.BlockSpec((tm, tn), lambda i, j: (i, jnp.maximum(j - hg_tiles, 0))),
            pl.BlockSpec((tm, LANES), lambda i, j: (i, 0)),
        ],
        out_shape=[jax.ShapeDtypeStruct((T, n_hg), F32),
                   jax.ShapeDtypeStruct((T, N - n_hg), BF16),
                   jax.ShapeDtypeStruct((T, LANES), F32)],
        scratch_shapes=[pltpu.VMEM((tm, D), BF16)],
        compiler_params=_cparams("arbitrary", "arbitrary"),
        name="inproj",
    )(x2d, g, w_main, w_gate)


def _hgrn_tables(C):
    t = np.arange(C)
    levels = []
    m = C // 2
    while m >= 1:
        levels.append(m)
        m //= 2
    rows = [np.tril(np.ones((C, C), np.float32)),
            np.triu(np.ones((C, C), np.float32), 1)]
    lev = np.full((C, C), -1, np.int32)
    for li, m in enumerate(levels):
        r = t % (2 * m)
        mid = t - r + m - 1
        M = np.zeros((C, C), np.float32)
        for i in range(C):
            if r[i] >= m:
                M[i, mid[i] + 1:i + 1] = 1.0
            else:
                M[i, i + 1:mid[i] + 1] = 1.0
        rows.append(M)
        same = (t[:, None] // (2 * m)) == (t[None, :] // (2 * m))
        lev[same & (r[:, None] >= m) & (r[None, :] < m)] = li
    lev[t[:, None] == t[None, :]] = len(levels)
    return np.concatenate(rows, 0), lev, len(levels)


HG_HEADS_PER_STEP = 8


def _hgrn_body(q_ref, f_ref, i_ref, g_ref, lb_ref, ng_ref, mst_ref, lev_ref, o_ref, st_ref,
               *, C, n_lev):
    @pl.when(pl.program_id(2) == 0)
    def _():
        st_ref[...] = jnp.zeros_like(st_ref)

    ng = ng_ref[...]
    mst = mst_ref[...]
    lev = lev_ref[...]
    n_chunks = q_ref.shape[0] // C

    def gates(rows, hb):
        cols = slice(hb * HG_DIM, (hb + 1) * HG_DIM)
        lb = lb_ref[hb]
        q = jax.nn.silu(q_ref[rows, cols])
        f = lb + (1.0 - lb) * jax.nn.sigmoid(f_ref[rows, cols])
        lf = jnp.log(f)
        lf0 = lf.astype(BF16)
        r1 = lf - lf0.astype(F32)
        lf1 = r1.astype(BF16)
        lf2 = (r1 - lf1.astype(F32)).astype(BF16)
        return q, 1.0 - f, i_ref[rows, cols].astype(BF16), jnp.concatenate([lf0, lf1, lf2], axis=0)

    def exponents(lf_a, lf_b):
        e = _dot(mst, jnp.concatenate([lf_a, lf_b], axis=1))
        return e[:, 0:HG_DIM], e[:, HG_DIM:2 * HG_DIM]

    def scores(q, k, gdec):
        a = jnp.where(lev == n_lev, _nt(q.astype(BF16), k.astype(BF16)), 0.0)
        for li in range(n_lev):
            g_m = gdec[(2 + li) * C:(3 + li) * C]
            a_m = _nt((q * g_m).astype(BF16), (k * g_m).astype(BF16))
            a = jnp.where(lev == li, a_m, a)
        return a.astype(BF16)

    def finish(rows, hb, q, k, v, gdec, a):
        cols = slice(hb * HG_DIM, (hb + 1) * HG_DIM)
        g_b = gdec[0:C]
        g_r = gdec[C:2 * C]
        st = st_ref[hb]
        o = _nt((q * g_b).astype(BF16), st.astype(BF16)) + _dot(a, v)
        g_last = g_b[C - 1:C, :]
        st_ref[hb] = st * g_last + _tn(v, (k * g_r).astype(BF16))
        o = o * lax.rsqrt(jnp.mean(o * o, axis=-1, keepdims=True) + RMS_EPS) * ng
        o_ref[rows, cols] = (o * jax.nn.silu(g_ref[rows, cols])).astype(o_ref.dtype)

    def chunk(c, carry):
        rows = pl.ds(pl.multiple_of(c * C, C), C)
        hs = range(HG_HEADS_PER_STEP)
        qkve = [gates(rows, hb) for hb in hs]
        e = []
        for hb in range(0, HG_HEADS_PER_STEP, 2):
            e.extend(exponents(qkve[hb][3], qkve[hb + 1][3]))
        gdec = [jnp.exp(x) for x in e]
        a = [scores(qkve[hb][0], qkve[hb][1], gdec[hb]) for hb in hs]
        for hb in hs:
            finish(rows, hb, qkve[hb][0], qkve[hb][1], qkve[hb][2], gdec[hb], a[hb])
        return carry

    lax.fori_loop(0, n_chunks, chunk, 0)


def _hgrn(proj, lb, ng, B, S, tb):
    C = HG_CHUNK
    mst, lev, n_lev = _hgrn_tables(C)
    mst = np.concatenate([mst] * 3, axis=1)
    nb = S // tb
    HB = HG_HEADS_PER_STEP
    hsteps = HG_HEADS // HB

    def col(group):
        return pl.BlockSpec((tb, HB * HG_DIM), lambda b, h, t: (b * nb + t, group * hsteps + h))

    return pl.pallas_call(
        functools.partial(_hgrn_body, C=C, n_lev=n_lev),
        grid=(B, hsteps, nb),
        in_specs=[
            col(0), col(1), col(2), col(3),
            pl.BlockSpec((HB, 1, HG_DIM), lambda b, h, t: (h, 0, 0)),
            pl.BlockSpec((1, HG_DIM), lambda b, h, t: (0, 0)),
            pl.BlockSpec(mst.shape, lambda b, h, t: (0, 0)),
            pl.BlockSpec(lev.shape, lambda b, h, t: (0, 0)),
        ],
        out_specs=pl.BlockSpec((tb, HB * HG_DIM), lambda b, h, t: (b * nb + t, h)),
        out_shape=jax.ShapeDtypeStruct((B * S, HG_WIDTH), BF16),
        scratch_shapes=[pltpu.VMEM((HB, HG_DIM, HG_DIM), F32)],
        compiler_params=_cparams("arbitrary", "arbitrary", "arbitrary"),
        name="hgrn2",
    )(proj, proj, proj, proj, lb, ng, jnp.asarray(mst, BF16), jnp.asarray(lev))


def _compress_body(h_ref, pe_ref, w1_ref, w2_ref, o_ref):
    hv = h_ref[...]
    half = hv.shape[1]
    nh = hv.shape[0]
    u = _dot((hv + pe_ref[0:1, :]).astype(BF16), w1_ref[0:half, :].astype(BF16))
    v = _dot((hv + pe_ref[1:2, :]).astype(BF16), w1_ref[half:2 * half, :].astype(BF16))
    pre = u + pltpu.roll(v, nh - 1, axis=0)
    o_ref[...] = _dot(jax.nn.silu(pre).astype(BF16), w2_ref[...].astype(BF16)).astype(o_ref.dtype)


def _compress(halves, pe, w1, w2):
    _, B, G, NH, HW = halves.shape
    return pl.pallas_call(
        _compress_body,
        grid=(2, B, G),
        in_specs=[
            pl.BlockSpec((None, None, None, NH, HW), lambda s, b, g: (s, b, g, 0, 0)),
            pl.BlockSpec((None, 2, HW), lambda s, b, g: (s, 0, 0)),
            pl.BlockSpec((None, 2 * HW, CMP_HIDDEN), lambda s, b, g: (s, 0, 0)),
            pl.BlockSpec((None, CMP_HIDDEN, NSA_DH), lambda s, b, g: (s, 0, 0)),
        ],
        out_specs=pl.BlockSpec((None, None, None, NH, NSA_DH), lambda s, b, g: (s, b, g, 0, 0)),
        out_shape=jax.ShapeDtypeStruct((2, B, G, NH, NSA_DH), BF16),
        compiler_params=_cparams("arbitrary", "arbitrary", "arbitrary"),
        name="nsa_compress",
    )(halves, pe, w1, w2)


def _nsa_compress_all(proj, pe_k, w1_k, w2_k, pe_v, w1_v, w2_v, B, S):
    NH = S // CMP_STRIDE
    c0 = NSA_WIDTH

    def halves(idx):
        t = proj[:, c0 + idx * NSA_KV:c0 + (idx + 1) * NSA_KV]
        t = t.reshape(B, NH, CMP_STRIDE, NSA_GROUPS, NSA_DH).transpose(0, 3, 1, 2, 4)
        return t.reshape(B, NSA_GROUPS, NH, CMP_STRIDE * NSA_DH)

    hw = CMP_STRIDE * NSA_DH
    return _compress(jnp.stack([halves(0), halves(1)]),
                     jnp.stack([pe_k.reshape(2, hw), pe_v.reshape(2, hw)]),
                     jnp.stack([w1_k, w1_v]), jnp.stack([w2_k, w2_v]))


NSA_TQ = 128


NSA_MASK_ROWS = 64
NSA_WIN_MASK_ROWS = 16
NSA_FAR = 4
NSA_SLC_PAD = (NSA_FAR - 1) * NSA_TQ


NSA_V_ROWS = NSA_DH + 16


def _softmax_cols(s):
    m = jnp.max(s, axis=0, keepdims=True)
    return m, jnp.exp2(s - m).astype(BF16)


def _normalise(acc):
    return acc[0:NSA_DH] * (1.0 / acc[NSA_DH:NSA_DH + 1])


NSA_GROUPS_PER_STEP = 4


def _nsa_body(qt_ref, gl_ref, kc_ref, vct_ref, ks_ref, vst_ref, kw_ref, vwt_ref,
              bc_ref, bw_ref, bn_ref, ovt_ref, o_ref, *, top_k, n_slc):
    TQ = NSA_TQ
    GG = NSA_GROUPS_PER_STEP
    i = pl.program_id(2)
    ovt = ovt_ref[...]
    jb = lax.broadcasted_iota(jnp.int32, (n_slc, TQ), 0)
    qblk = (i * TQ + lax.broadcasted_iota(jnp.int32, (n_slc, TQ), 1)) // SLC_LEN
    forced = (jb == 0) | (jb == qblk) | (jb == qblk - 1)
    future = jb > qblk
    near = pl.ds(pl.multiple_of(NSA_SLC_PAD + (i - 1) * TQ, TQ), 2 * TQ)

    def heads(a):
        return jnp.concatenate([a] * NSA_HPG, axis=1)

    def queries(g):
        qt = qt_ref[g]
        return jnp.concatenate([qt[h * NSA_DH:(h + 1) * NSA_DH, :] for h in range(NSA_HPG)],
                               axis=1)

    def compressed(g, s):
        m = jnp.max(s, axis=0, keepdims=True)
        p = jnp.where(s > 0.5 * NEG_INF, jnp.exp2(s - m), 0.0)
        acc = _dot(vct_ref[g], p.astype(BF16))
        inv_l = 1.0 / jnp.maximum(acc[NSA_DH:NSA_DH + 1], 1e-30)
        o_cmp = acc[0:NSA_DH] * inv_l
        pc = p * inv_l
        ps = pc[:, 0:TQ] + pc[:, TQ:2 * TQ] + pc[:, 2 * TQ:3 * TQ] + pc[:, 3 * TQ:4 * TQ]
        ps_hi = ps.astype(BF16)
        ps_lo = (ps - ps_hi.astype(F32)).astype(BF16)
        imp = _dot(ovt, ps_hi) + _dot(ovt, ps_lo)
        return o_cmp, jnp.where(forced, BIG, jnp.where(future, -BIG, imp))

    def window(g, s):
        _, p = _softmax_cols(s)
        return _normalise(_dot(vwt_ref[g, :, win], p))

    def select(imp, q4):
        slabs = []
        for v in range(n_slc // 8):
            slab = imp[8 * v:8 * v + 8, :]
            jbs = jb[8 * v:8 * v + 8, :]
            cnt = jnp.zeros((8, TQ), F32)
            for jp in range(n_slc):
                row = imp[jp:jp + 1, :]
                if jp < 8 * v:
                    beats = row >= slab
                elif jp >= 8 * v + 8:
                    beats = row > slab
                else:
                    beats = (row > slab) | ((row == slab) & (jbs > jp))
                cnt = cnt + jnp.where(beats, 1.0, 0.0)
            slabs.append(jnp.where(cnt < top_k, 0.0, 1.0))
        if n_slc < NSA_MASK_ROWS:
            slabs.append(jnp.ones((NSA_MASK_ROWS - n_slc, TQ), F32))
        unsel = jnp.concatenate(slabs, axis=0)
        return jnp.concatenate([heads(unsel.astype(BF16)), q4], axis=0)

    def near_tiles(g, s):
        m, p = _softmax_cols(s)
        return m, _dot(vst_ref[g, :, near], p)

    win = pl.ds(pl.multiple_of(i * TQ, TQ), WIN + TQ)
    pad_row = jnp.where(lax.broadcasted_iota(jnp.int32, (NSA_WIN_MASK_ROWS, NSA_HPG * TQ), 0) == 0,
                        1.0, 0.0).astype(BF16)
    gs = range(GG)
    q4 = [queries(g) for g in gs]
    nh = kc_ref.shape[1]
    cmp_rows = pl.ds(pl.multiple_of(nh - (TQ // CMP_STRIDE) * i, TQ // CMP_STRIDE), nh)
    s_cmp = [_dot(kc_ref[g], q4[g]) + bc_ref[g, cmp_rows, :] for g in gs]
    s_win = [_dot(kw_ref[g, win, :], jnp.concatenate([pad_row, q4[g]], axis=0)) + bw_ref[g] for g in gs]
    cmp_out = [compressed(g, s_cmp[g]) for g in gs]
    o_win = [window(g, s_win[g]) for g in gs]
    qa = [select(cmp_out[g][1], q4[g]) for g in gs]
    s_near = [_dot(ks_ref[g, near, :], qa[g]) + bn_ref[g] for g in gs]
    start = tuple(near_tiles(g, s_near[g]) for g in gs)

    n_far = i - 1
    n_chunks = jnp.maximum((n_far + NSA_FAR - 1) // NSA_FAR, 0)
    far0 = NSA_SLC_PAD + TQ * (n_far - NSA_FAR * n_chunks)

    def far_step(c, carry):
        keys = pl.ds(pl.multiple_of(far0 + c * NSA_FAR * TQ, TQ), NSA_FAR * TQ)
        out = []
        logits = [_dot(ks_ref[g, keys, :], qa[g]) for g in gs]
        for g in gs:
            m, acc = carry[g]
            s = logits[g]
            m_new = jnp.maximum(m, jnp.max(s, axis=0, keepdims=True))
            p = jnp.exp2(s - m_new).astype(BF16)
            acc = jnp.exp2(m - m_new) * acc + _dot(vst_ref[g, :, keys], p)
            out.append((m_new, acc))
        return tuple(out)

    far = lax.fori_loop(0, n_chunks, far_step, start)

    for g in gs:
        o_slc = _normalise(far[g][1])
        o_cmp = cmp_out[g][0]
        sg = jax.nn.sigmoid(gl_ref[g])
        outs = []
        for h in range(NSA_HPG):
            lanes = slice(h * TQ, (h + 1) * TQ)
            outs.append(sg[h:h + 1, :] * o_cmp[:, lanes]
                        + sg[NSA_HPG + h:NSA_HPG + h + 1, :] * o_slc[:, lanes]
                        + sg[2 * NSA_HPG + h:2 * NSA_HPG + h + 1, :] * o_win[g][:, lanes])
        width = NSA_HPG * NSA_DH
        o_ref[:, g * width:(g + 1) * width] = jnp.concatenate(outs, axis=0).T.astype(o_ref.dtype)


def _rel_bucket(dist):
    n = jnp.maximum(dist, 0)
    max_exact = REL_BUCKETS // 2
    nf = jnp.maximum(n, 1).astype(F32)
    large = max_exact + (jnp.log(nf / max_exact) / math.log(REL_MAX_DIST / max_exact)
                         * (REL_BUCKETS - max_exact)).astype(jnp.int32)
    large = jnp.minimum(large, REL_BUCKETS - 1)
    return jnp.where(n < max_exact, n, large)


def _bias_table(table, dist, valid):
    onehot = (_rel_bucket(jnp.asarray(dist, jnp.int32))[..., None]
              == jnp.arange(REL_BUCKETS, dtype=jnp.int32)).astype(F32)
    b = jnp.einsum('rmk,ghk->ghrm', onehot, table, precision=lax.Precision.HIGHEST)
    return jnp.where(jnp.asarray(valid), b, NEG_INF)


def _relayout_body(x_ref, ms_ref, qt_ref, ks_ref, vst_ref, kw_ref, vwt_ref, *, lead):
    TQ = NSA_TQ
    real = pl.program_id(1) >= lead
    x = x_ref[...]

    def columns(idx):
        lo = NSA_WIDTH + idx * NSA_KV
        return jnp.where(real, x[:, lo:lo + NSA_KV], 0.0)

    @pl.when(real)
    def _():
        qt = x[:, 0:NSA_WIDTH].astype(F32).T.astype(BF16)
        qt_ref[...] = qt.reshape(qt_ref.shape)

    ones = jnp.where((lax.broadcasted_iota(jnp.int32, (NSA_V_ROWS - NSA_DH, TQ), 0) == 0) & real,
                     1.0, 0.0).astype(BF16)
    pad_col = jnp.where((lax.broadcasted_iota(jnp.int32, (TQ, NSA_WIN_MASK_ROWS), 1) == 0) & ~real,
                        NEG_INF, 0.0).astype(BF16)
    ms = ms_ref[...]
    k_slc, k_win = columns(2), columns(4)
    v_slc = columns(3).astype(F32).T.astype(BF16)
    v_win = columns(5).astype(F32).T.astype(BF16)
    for g in range(NSA_GROUPS):
        d = slice(g * NSA_DH, (g + 1) * NSA_DH)
        ks_ref[g] = jnp.concatenate([ms, k_slc[:, d]], axis=1)
        kw_ref[g] = jnp.concatenate([pad_col, k_win[:, d]], axis=1)
        vst_ref[g] = jnp.concatenate([v_slc[d, :], ones], axis=0)
        vwt_ref[g] = jnp.concatenate([v_win[d, :], ones], axis=0)


def _nsa_relayout(proj, slc_masks, B, S):
    TQ = NSA_TQ
    G, DH, SP = NSA_GROUPS, NSA_DH, NSA_SLC_PAD
    nt = S // TQ
    lead = WIN // TQ
    slc_lead = SP // TQ
    tok = lambda u: jnp.maximum(u - lead, 0)
    slc = lambda u: jnp.maximum(u - (lead - slc_lead), 0)
    return pl.pallas_call(
        functools.partial(_relayout_body, lead=lead),
        grid=(B, lead + nt),
        in_specs=[pl.BlockSpec((TQ, proj.shape[1]), lambda b, u: (b * nt + tok(u), 0)),
                  pl.BlockSpec((TQ, NSA_MASK_ROWS), lambda b, u: (slc(u), 0))],
        out_specs=[
            pl.BlockSpec((None, G, NSA_HPG * DH, TQ), lambda b, u: (b, 0, 0, tok(u))),
            pl.BlockSpec((None, G, TQ, NSA_MASK_ROWS + DH), lambda b, u: (b, 0, slc(u), 0)),
            pl.BlockSpec((None, G, NSA_V_ROWS, TQ), lambda b, u: (b, 0, 0, slc(u))),
            pl.BlockSpec((None, G, TQ, NSA_WIN_MASK_ROWS + DH), lambda b, u: (b, 0, u, 0)),
            pl.BlockSpec((None, G, NSA_V_ROWS, TQ), lambda b, u: (b, 0, 0, u)),
        ],
        out_shape=[
            jax.ShapeDtypeStruct((B, G, NSA_HPG * DH, S), BF16),
            jax.ShapeDtypeStruct((B, G, S + SP, NSA_MASK_ROWS + DH), BF16),
            jax.ShapeDtypeStruct((B, G, NSA_V_ROWS, S + SP), BF16),
            jax.ShapeDtypeStruct((B, G, S + WIN, NSA_WIN_MASK_ROWS + DH), BF16),
            jax.ShapeDtypeStruct((B, G, NSA_V_ROWS, S + WIN), BF16),
        ],
        compiler_params=_cparams("arbitrary", "arbitrary"),
        name="nsa_relayout",
    )(proj, slc_masks)


def _nsa(proj, glog, kc, vc, rel_bias, B, S):
    TQ = NSA_TQ
    G, HPG, DH = NSA_GROUPS, NSA_HPG, NSA_DH
    NH = S // CMP_STRIDE
    n_cmp = (S - CMP_LEN) // CMP_STRIDE + 1
    n_slc = S // SLC_LEN
    top_k = min(SLC_TOPK, n_slc)
    nt = S // TQ
    c0 = NSA_WIDTH

    MR = NSA_MASK_ROWS
    assert n_slc % 8 == 0 and (n_slc < MR or (n_slc == MR and top_k < n_slc))

    SP = NSA_SLC_PAD
    pos = np.arange(SP + S) - SP
    cols = np.zeros((SP + S, MR), np.float32)
    cols[:, :n_slc] = ((pos // SLC_LEN)[:, None] == np.arange(n_slc)[None, :]) & (pos >= 0)[:, None]
    if n_slc < MR:
        cols[:, n_slc] = pos < 0
    else:
        cols[pos < 0, :] = 1.0
    qt, ks, vst, kw, vwt = _nsa_relayout(proj, jnp.asarray(cols * NEG_INF, BF16), B, S)
    kcb = kc
    ones = jnp.zeros((NSA_V_ROWS - DH, NH), BF16).at[0].set(1.0)
    vct = jnp.concatenate([vc.transpose(0, 1, 3, 2), jnp.broadcast_to(ones, (B, G) + ones.shape)],
                          axis=2)

    table = rel_bias.T.reshape(G, HPG, REL_BUCKETS) * LOG2E
    far = table[:, :, REL_BUCKETS - 1]
    r = np.arange(TQ)[:, None]

    def keys_major(b):
        return b.transpose(0, 3, 1, 2).reshape(G, b.shape[3], HPG * TQ)

    d_c = r - (CMP_LEN - 1) - CMP_STRIDE * (np.arange(2 * NH)[None, :] - NH)
    bias_cmp = keys_major(_bias_table(table, d_c, d_c >= 0))
    d_w = r + WIN - np.arange(WIN + TQ)[None, :]
    bias_win = keys_major(_bias_table(table, d_w, (d_w >= 0) & (d_w < WIN)))
    d_n = r + TQ - np.arange(2 * TQ)[None, :]
    bias_near = keys_major(_bias_table(table - far[:, :, None], d_n, d_n >= 0))

    ci = np.arange(NH)[None, :] * CMP_STRIDE
    sj = np.arange(n_slc)[:, None] * SLC_LEN
    overlap_t = ((ci <= sj + SLC_LEN - 1) & (ci + CMP_LEN - 1 >= sj) & (np.arange(NH)[None, :] < n_cmp))

    gate_rows = 16
    glog_g = glog[:, :GATE_COLS].reshape(B * S, 3, G, HPG).transpose(2, 1, 3, 0).reshape(G, 3 * HPG, B * S)
    glog_g = jnp.pad(glog_g, ((0, 0), (0, gate_rows - 3 * HPG), (0, 0)))

    GG = NSA_GROUPS_PER_STEP
    whole = lambda shape: pl.BlockSpec((None, GG) + shape, lambda b, g, i: (b, g, 0, 0))
    tile = lambda k: pl.BlockSpec((GG, k, HPG * TQ), lambda b, g, i: (g, 0, 0))
    return pl.pallas_call(
        functools.partial(_nsa_body, top_k=top_k, n_slc=n_slc),
        grid=(B, G // GG, nt),
        in_specs=[
            pl.BlockSpec((None, GG, HPG * DH, TQ), lambda b, g, i: (b, g, 0, i)),
            pl.BlockSpec((GG, gate_rows, TQ), lambda b, g, i: (g, 0, b * nt + i)),
            whole((NH, DH)), whole((NSA_V_ROWS, NH)),
            whole((S + SP, MR + DH)), whole((NSA_V_ROWS, S + SP)),
            whole((S + WIN, NSA_WIN_MASK_ROWS + DH)), whole((NSA_V_ROWS, S + WIN)),
            tile(2 * NH), tile(WIN + TQ), tile(2 * TQ),
            pl.BlockSpec((n_slc, NH), lambda b, g, i: (0, 0)),
        ],
        out_specs=pl.BlockSpec((TQ, GG * HPG * DH), lambda b, g, i: (b * nt + i, g)),
        out_shape=jax.ShapeDtypeStruct((B * S, NSA_WIDTH), BF16),
        compiler_params=_cparams("arbitrary", "arbitrary", "arbitrary"),
        name="nsa_attention",
    )(qt, glog_g, kcb, vct, ks, vst, kw, vwt, bias_cmp, bias_win, bias_near,
      jnp.asarray(overlap_t, BF16))


ROUTE_E1, ROUTE_E2, ROUTE_W1, ROUTE_W2 = 0, 1, 2, 3


def _outproj_body(x_ref, yh_ref, yn_ref, wo_ref, g2_ref, wrh_ref, wrl_ref, br_ref,
                  x2_ref, h2_ref, rt_ref):
    x2 = (x_ref[...] + _dot(yh_ref[...], wo_ref[0:HG_WIDTH, :])
          + _dot(yn_ref[...], wo_ref[HG_WIDTH:HG_WIDTH + NSA_WIDTH, :]))
    x2_ref[...] = x2
    h = x2 * lax.rsqrt(jnp.mean(x2 * x2, axis=-1, keepdims=True) + RMS_EPS) * g2_ref[...]
    hb = h.astype(BF16)
    h2_ref[...] = _pack_bf16_pairs(h)
    hl = (h - hb.astype(F32)).astype(BF16)
    wrh = wrh_ref[...]
    lg = _dot(hb, wrh) + _dot(hl, wrh) + _dot(hb, wrl_ref[...]) + br_ref[...]
    lane = lax.broadcasted_iota(jnp.int32, lg.shape, 1)
    first = lambda hit: jnp.min(jnp.where(hit, lane, LANES), axis=-1, keepdims=True)
    lgg = jnp.where(lane < N_EXPERT_GROUPS, lg, NEG_INF)
    mg = jnp.max(lgg, axis=-1, keepdims=True)
    p_top = 1.0 / jnp.sum(jnp.exp(lgg - mg), axis=-1, keepdims=True)
    lo = N_EXPERT_GROUPS + EXPERTS_PER_GROUP * first(lgg == mg)
    le = jnp.where((lane >= lo) & (lane < lo + EXPERTS_PER_GROUP), lg, NEG_INF)
    v1 = jnp.max(le, axis=-1, keepdims=True)
    i1 = first(le == v1)
    le = jnp.where(lane == i1, NEG_INF, le)
    v2 = jnp.max(le, axis=-1, keepdims=True)
    i2 = first(le == v2)
    e21 = jnp.exp(v2 - v1)
    w1 = p_top / (1.0 + e21)
    rt = jnp.where(lane == ROUTE_E1, (i1 - N_EXPERT_GROUPS).astype(F32), 0.0)
    rt = jnp.where(lane == ROUTE_E2, (i2 - N_EXPERT_GROUPS).astype(F32), rt)
    rt = jnp.where(lane == ROUTE_W1, w1, rt)
    rt_ref[...] = jnp.where(lane == ROUTE_W2, w1 * e21, rt)


def _outproj(x2d, y_hg, y_nsa, w_out, g2, wr_hi, wr_lo, b_r, tm):
    T, D = x2d.shape
    row = lambda n: pl.BlockSpec((tm, n), lambda i: (i, 0))
    full = lambda a: pl.BlockSpec(a.shape, lambda i: (0, 0))
    return pl.pallas_call(
        _outproj_body,
        grid=(T // tm,),
        in_specs=[row(D), row(HG_WIDTH), row(NSA_WIDTH), full(w_out), full(g2),
                  full(wr_hi), full(wr_lo), full(b_r)],
        out_specs=[row(D), row(D // 2), row(LANES)],
        out_shape=[jax.ShapeDtypeStruct((T, D), F32), jax.ShapeDtypeStruct((T, D // 2), jnp.int32),
                   jax.ShapeDtypeStruct((T, LANES), F32)],
        compiler_params=_cparams("arbitrary"),
        name="outproj_router",
    )(x2d, y_hg, y_nsa, w_out, g2, wr_hi, wr_lo, b_r)


ROW_TILE = 512


def _dispatch_body(dest_ref, h_ref, xs0_hbm, xs_hbm, sem):
    del xs0_hbm
    tb = h_ref.shape[0]

    def start(t, carry):
        for k in range(2):
            pltpu.make_async_copy(h_ref.at[pl.ds(t, 1), :],
                                  xs_hbm.at[pl.ds(dest_ref[0, 2 * t + k], 1), :], sem).start()
        return carry

    lax.fori_loop(0, tb, start, 0, unroll=8)
    for _ in range(2):
        pltpu.make_async_copy(h_ref, xs_hbm.at[pl.ds(0, tb), :], sem).wait()


def _dispatch(h2p, dest, n_rows):
    T, W = h2p.shape
    tb = ROW_TILE
    return pl.pallas_call(
        _dispatch_body,
        grid=(T // tb,),
        in_specs=[pl.BlockSpec((None, 1, 2 * tb), lambda i: (i, 0, 0), memory_space=pltpu.SMEM),
                  pl.BlockSpec((tb, W), lambda i: (i, 0)),
                  pl.BlockSpec(memory_space=pl.ANY)],
        out_specs=pl.BlockSpec(memory_space=pl.ANY),
        out_shape=jax.ShapeDtypeStruct((n_rows, W), jnp.int32),
        scratch_shapes=[pltpu.SemaphoreType.DMA(())],
        input_output_aliases={2: 0},
        compiler_params=_cparams("arbitrary"),
        name="moe_dispatch",
    )(dest.reshape(T // tb, 1, 2 * tb), h2p, jnp.zeros((n_rows, W), jnp.int32))


MOE_ALIGN = 128
MOE_TM = 1024
MOE_SUB = 256
MOE_TF = 256
MOE_CAST_PIECES = 4
MOE_WEIGHT_BUFFERS = 3


def _moe_body(ie_ref, ir_ref, in_ref, nb_ref, xs_hbm, wg_hbm, wu_hbm, wd_hbm, y_hbm,
              xbuf, xlo, xhi, acc, ypk, wgb, wub, wdb, wgf, wuf, wdf, sem_in, sem_out, sem_w):
    w = pl.program_id(0)
    f = pl.program_id(1)
    n_w = pl.num_programs(0)
    n_f = pl.num_programs(1)
    last_f = n_f - 1
    ring = wgf.shape[0]
    step = w * n_f + f

    def weight_copies(k):
        item = jnp.minimum(k // n_f, n_w - 1)
        wanted = (k < n_w * n_f) & (in_ref[item] > 0)
        e = ie_ref[item]
        cols = pl.ds(pl.multiple_of((k % n_f) * MOE_TF, MOE_TF), MOE_TF)
        s = k % ring
        return wanted, (pltpu.make_async_copy(wg_hbm.at[e, :, cols], wgf.at[s], sem_w.at[0, s]),
                        pltpu.make_async_copy(wu_hbm.at[e, :, cols], wuf.at[s], sem_w.at[1, s]),
                        pltpu.make_async_copy(wd_hbm.at[e, cols, :], wdf.at[s], sem_w.at[2, s]))

    def request_weights(k):
        wanted, copies = weight_copies(k)

        @pl.when(wanted)
        def _():
            for c in copies:
                c.start()

    @pl.when(step == 0)
    def _():
        for k in range(1, ring):
            request_weights(k)

    request_weights(step + ring)
    cast_next, arriving = weight_copies(step + 1)

    @pl.when(cast_next)
    def _():
        for c in arriving:
            c.wait()

    nslot = (step + 1) % ring
    nsub = in_ref[w]
    row0 = pl.multiple_of(ir_ref[w] * MOE_ALIGN, MOE_ALIGN)
    slot = w % 2
    per_mm = MOE_SUB // MOE_ALIGN
    n_full = nsub // per_mm
    n_iter = n_full + nsub % per_mm
    half = xlo.shape[1]
    cur = step % 2

    def cast_piece(p):
        rk = wgf.shape[1] // MOE_CAST_PIECES
        rows = pl.ds(pl.multiple_of(p * rk, rk), rk)
        wgb[1 - cur, rows, :] = wgf[nslot, rows, :].astype(BF16)
        wub[1 - cur, rows, :] = wuf[nslot, rows, :].astype(BF16)
        rf = wdf.shape[1] // MOE_CAST_PIECES
        rows = pl.ds(pl.multiple_of(p * rf, rf), rf)
        wdb[1 - cur, rows, :] = wdf[nslot, rows, :].astype(BF16)

    def x_copy(item, s):
        r = pl.multiple_of(ir_ref[item] * MOE_ALIGN, MOE_ALIGN)
        return pltpu.make_async_copy(xs_hbm.at[pl.ds(r, MOE_TM), :], xbuf.at[s], sem_in.at[s])

    def y_copy(r, dst_row):
        return pltpu.make_async_copy(ypk.at[pl.ds(r, MOE_ALIGN), :],
                                     y_hbm.at[pl.ds(dst_row, MOE_ALIGN), :], sem_out)

    def for_row_blocks(fn):
        def body(j, carry):
            fn(pl.multiple_of(j * MOE_SUB, MOE_SUB), MOE_SUB, j)
            return carry

        lax.fori_loop(0, n_full, body, 0)

        @pl.when(nsub % per_mm == 1)
        def _():
            fn(pl.multiple_of(n_full * MOE_SUB, MOE_ALIGN), MOE_ALIGN, n_full)

    @pl.when((f == 0) & (nsub > 0))
    def _():
        @pl.when(in_ref[jnp.maximum(w - 1, 0)] == 0)
        def _():
            x_copy(w, slot).start()

        x_copy(w, slot).wait()

        def unpack(r, size, _):
            lo, hi = _unpack_bf16_pairs(xbuf[slot, pl.ds(r, size), :])
            xlo[pl.ds(r, size), :] = lo.astype(BF16)
            xhi[pl.ds(r, size), :] = hi.astype(BF16)

        for_row_blocks(unpack)

    def experts(first, last):
        def block(r, size, it):
            rows = pl.ds(r, size)
            lo = xlo[rows, :]
            hi = xhi[rows, :]
            hg = _dot(lo, wgb[cur, 0:half, :]) + _dot(hi, wgb[cur, half:2 * half, :])
            hu = _dot(lo, wub[cur, 0:half, :]) + _dot(hi, wub[cur, half:2 * half, :])
            y = _dot((jax.nn.silu(hg) * hu).astype(BF16), wdb[cur])
            if first:
                acc[rows, :] = y
            elif not last:
                acc[rows, :] += y
            else:
                ypk[rows, :] = _pack_bf16_pairs(acc[rows, :] + y)
                for k in range(size // MOE_ALIGN):
                    y_copy(r + k * MOE_ALIGN, row0 + r + k * MOE_ALIGN).start()

            @pl.when(cast_next & (it < MOE_CAST_PIECES))
            def _():
                cast_piece(it)

        for_row_blocks(block)

        @pl.when(cast_next)
        def _():
            lax.fori_loop(jnp.minimum(n_iter, MOE_CAST_PIECES), MOE_CAST_PIECES,
                          lambda p, c: (cast_piece(p), c)[1], 0)

    @pl.when((nsub == 0) & cast_next)
    def _():
        lax.fori_loop(0, MOE_CAST_PIECES, lambda p, c: (cast_piece(p), c)[1], 0)

    @pl.when((f == 0) & (nsub > 0))
    def _():
        experts(True, False)

    @pl.when((f > 0) & (f < last_f) & (nsub > 0))
    def _():
        experts(False, False)

    @pl.when((f == last_f) & (nsub > 0))
    def _():
        nxt = jnp.minimum(w + 1, n_w - 1)

        @pl.when((w + 1 < n_w) & (in_ref[nxt] > 0))
        def _():
            x_copy(nxt, 1 - slot).start()

        experts(False, True)

        def wait(j, carry):
            y_copy(0, 0).wait()
            return carry

        lax.fori_loop(0, nsub, wait, 0)

    @pl.when((f == last_f) & (w == n_w - 1))
    def _():
        ypk[0:MOE_ALIGN, :] = jnp.zeros((MOE_ALIGN, ypk.shape[1]), jnp.int32)
        used = nb_ref[0]
        total = y_hbm.shape[0] // MOE_ALIGN

        def fill(j, carry):
            y_copy(0, pl.multiple_of(j * MOE_ALIGN, MOE_ALIGN)).start()
            return carry

        def wait(j, carry):
            y_copy(0, 0).wait()
            return carry

        lax.fori_loop(used, total, fill, 0)
        lax.fori_loop(used, total, wait, 0)


def _moe(xs, item_e, item_r, item_n, n_blocks, w_gate, w_up, w_down):
    rows = xs.shape[0]
    D = 2 * xs.shape[1]
    n_items = item_e.shape[0]
    nf = EXPERT_DFF // MOE_TF
    assert nf >= 2 and MOE_SUB == 2 * MOE_ALIGN

    R = MOE_WEIGHT_BUFFERS
    return pl.pallas_call(
        _moe_body,
        grid_spec=pltpu.PrefetchScalarGridSpec(
            num_scalar_prefetch=4,
            grid=(n_items, nf),
            in_specs=[
                pl.BlockSpec(memory_space=pl.ANY),
                pl.BlockSpec(memory_space=pl.ANY),
                pl.BlockSpec(memory_space=pl.ANY),
                pl.BlockSpec(memory_space=pl.ANY),
            ],
            out_specs=pl.BlockSpec(memory_space=pl.ANY),
            scratch_shapes=[pltpu.VMEM((2, MOE_TM, D // 2), jnp.int32),
                            pltpu.VMEM((MOE_TM, D // 2), BF16), pltpu.VMEM((MOE_TM, D // 2), BF16),
                            pltpu.VMEM((MOE_TM, D), F32), pltpu.VMEM((MOE_TM, D // 2), jnp.int32),
                            pltpu.VMEM((2, D, MOE_TF), BF16), pltpu.VMEM((2, D, MOE_TF), BF16),
                            pltpu.VMEM((2, MOE_TF, D), BF16),
                            pltpu.VMEM((R, D, MOE_TF), F32), pltpu.VMEM((R, D, MOE_TF), F32),
                            pltpu.VMEM((R, MOE_TF, D), F32),
                            pltpu.SemaphoreType.DMA((2,)), pltpu.SemaphoreType.DMA(()),
                            pltpu.SemaphoreType.DMA((3, R))],
        ),
        out_shape=jax.ShapeDtypeStruct((rows, D // 2), jnp.int32),
        compiler_params=_cparams("arbitrary", "arbitrary"),
        name="moe_experts",
    )(item_e, item_r, item_n, n_blocks, xs, w_gate, w_up, w_down)


def _final_body(dcur_ref, dnext_ref, x2_ref, rt_ref, g_ref, ys_hbm, o_ref, ybuf, sems):
    i = pl.program_id(0)
    tb = x2_ref.shape[0]
    slot = i % 2

    def gather(d_ref, s):
        def body(t, carry):
            for k in range(2):
                pltpu.make_async_copy(ys_hbm.at[pl.ds(d_ref[0, 2 * t + k], 1), :],
                                      ybuf.at[s, k, pl.ds(t, 1), :], sems.at[s]).start()
            return carry

        lax.fori_loop(0, tb, body, 0, unroll=8)

    @pl.when(i == 0)
    def _():
        gather(dcur_ref, 0)

    @pl.when(i + 1 < pl.num_programs(0))
    def _():
        gather(dnext_ref, 1 - slot)

    for k in range(2):
        pltpu.make_async_copy(ys_hbm.at[pl.ds(0, tb), :], ybuf.at[slot, k], sems.at[slot]).wait()
    rt = rt_ref[...]
    lo1, hi1 = _unpack_bf16_pairs(ybuf[slot, 0])
    lo2, hi2 = _unpack_bf16_pairs(ybuf[slot, 1])
    w1 = rt[:, ROUTE_W1:ROUTE_W1 + 1]
    w2 = rt[:, ROUTE_W2:ROUTE_W2 + 1]
    x = x2_ref[...] + jnp.concatenate([w1 * lo1 + w2 * lo2, w1 * hi1 + w2 * hi2], axis=1)
    o_ref[...] = x * lax.rsqrt(jnp.mean(x * x, axis=-1, keepdims=True) + RMS_EPS) * g_ref[...]


def _final(x2, ys, dest, rt, g):
    T, D = x2.shape
    tb = ROW_TILE
    nb = T // tb
    dest3 = dest.reshape(nb, 1, 2 * tb)
    return pl.pallas_call(
        _final_body,
        grid=(nb,),
        in_specs=[pl.BlockSpec((None, 1, 2 * tb), lambda i: (i, 0, 0), memory_space=pltpu.SMEM),
                  pl.BlockSpec((None, 1, 2 * tb), lambda i: (jnp.minimum(i + 1, nb - 1), 0, 0),
                               memory_space=pltpu.SMEM),
                  pl.BlockSpec((tb, D), lambda i: (i, 0)),
                  pl.BlockSpec((tb, LANES), lambda i: (i, 0)),
                  pl.BlockSpec((1, D), lambda i: (0, 0)),
                  pl.BlockSpec(memory_space=pl.ANY)],
        out_specs=pl.BlockSpec((tb, D), lambda i: (i, 0)),
        out_shape=jax.ShapeDtypeStruct((T, D), F32),
        scratch_shapes=[pltpu.VMEM((2, 2, tb, D // 2), jnp.int32), pltpu.SemaphoreType.DMA((2,))],
        compiler_params=_cparams("arbitrary"),
        name="combine_final_norm",
    )(dest3, dest3, x2, rt, g, ys)


def _moe_layout(e_flat):
    A = e_flat.shape[0]
    onehot = (e_flat[:, None] == jnp.arange(N_EXPERTS, dtype=jnp.int32)[None, :]).astype(jnp.int32)
    csum = jnp.cumsum(onehot, axis=0)
    counts = csum[-1]
    nblk = (counts + MOE_ALIGN - 1) // MOE_ALIGN
    blk0 = jnp.cumsum(nblk) - nblk
    dest = jnp.sum(onehot * (blk0[None, :] * MOE_ALIGN + csum - 1), axis=1).astype(jnp.int32)
    per_item = MOE_TM // MOE_ALIGN
    nitem = (nblk + per_item - 1) // per_item
    iend = jnp.cumsum(nitem)
    n_items = (A // MOE_ALIGN + N_EXPERTS + N_EXPERTS * (per_item - 1)) // per_item
    w = jnp.arange(n_items, dtype=jnp.int32)
    total = iend[-1]
    wv = jnp.minimum(w, total - 1)
    item_e = jnp.minimum(jnp.searchsorted(iend, wv, side='right'), N_EXPERTS - 1).astype(jnp.int32)
    sb = wv - (iend - nitem)[item_e]
    item_r = (blk0[item_e] + sb * per_item).astype(jnp.int32)
    item_n = jnp.where(w < total, jnp.minimum(per_item, nblk[item_e] - sb * per_item), 0).astype(jnp.int32)
    n_blocks = jnp.sum(nblk).astype(jnp.int32).reshape(1)
    lead = lambda a, v: jnp.concatenate([jnp.asarray(v, jnp.int32).reshape(1), a])
    return dest, lead(item_e, item_e[0]), lead(item_r, 0), lead(item_n, 0), n_blocks


def kernel(x, norm1_g, w_in, hg_lb_logits, hg_norm_g, cmp_pe_k, cmp_w1_k, cmp_w2_k, cmp_pe_v,
           cmp_w1_v, cmp_w2_v, rel_bias, w_out, norm2_g, w_router_group, b_router_group,
           w_router_expert, b_router_expert, w_expert_gate, w_expert_up, w_expert_down,
           final_norm_g):
    B, S, D = x.shape
    T = B * S
    assert w_in.shape[0] == 1, "single-layer block"
    x2d = x.reshape(T, D)

    lower = jax.nn.softmax(hg_lb_logits.astype(F32), axis=0)[0].reshape(HG_HEADS, 1, HG_DIM)

    w_main = w_in[0, :, :MAIN_COLS].astype(BF16)
    w_gate_cols = jnp.pad(w_in[0, :, MAIN_COLS:].astype(BF16), ((0, 0), (0, LANES - GATE_COLS)))
    proj_hg, proj_nsa, glog = _inproj(x2d, norm1_g[0].reshape(1, D), w_main, w_gate_cols,
                                      tm=INPROJ_TM, tn=INPROJ_TN)

    y_hg = _hgrn(proj_hg, lower, hg_norm_g[0].reshape(1, HG_DIM), B, S, tb=HGRN_TB)
    kcvc = _nsa_compress_all(proj_nsa, cmp_pe_k[0], cmp_w1_k[0], cmp_w2_k[0],
                             cmp_pe_v[0], cmp_w1_v[0], cmp_w2_v[0], B, S)
    y_nsa = _nsa(proj_nsa, glog, kcvc[0], kcvc[1], rel_bias, B, S)

    w_r = jnp.concatenate([w_router_group[0], w_router_expert[0]], axis=1)
    w_r = jnp.pad(w_r, ((0, 0), (0, LANES - w_r.shape[1])))
    wr_hi = w_r.astype(BF16)
    wr_lo = (w_r - wr_hi.astype(F32)).astype(BF16)
    b_r = jnp.concatenate([b_router_group[0], b_router_expert[0]])
    b_r = jnp.pad(b_r, (0, LANES - b_r.shape[0])).reshape(1, LANES)
    x2, h2, rt = _outproj(x2d, y_hg, y_nsa, w_out[0].astype(BF16), norm2_g[0].reshape(1, D),
                          wr_hi, wr_lo, b_r, tm=OUTPROJ_TM)

    e_flat = rt[:, ROUTE_E1:ROUTE_E2 + 1].astype(jnp.int32).reshape(2 * T)
    dest, item_e, item_r, item_n, n_blocks = _moe_layout(e_flat)
    n_rows = (2 * T // MOE_ALIGN + N_EXPERTS) * MOE_ALIGN + MOE_TM
    xs = _dispatch(h2, dest, n_rows)
    ys = _moe(xs, item_e, item_r, item_n, n_blocks,
              w_expert_gate[0], w_expert_up[0], w_expert_down[0])

    out = _final(x2, ys, dest, rt, final_norm_g.reshape(1, D))
    return out.reshape(B, S, D)
```

```python
import functools
import math

import jax
import jax.numpy as jnp
import numpy as np
from jax import lax
from jax.experimental import pallas as pl
from jax.experimental.pallas import tpu as pltpu

F32 = jnp.float32
BF16 = jnp.bfloat16

HG_HEADS = 8
HG_DIM = 128
HG_WIDTH = HG_HEADS * HG_DIM
HG_CHUNK = 64
NSA_HEADS = 16
NSA_GROUPS = 4
NSA_HPG = NSA_HEADS // NSA_GROUPS
NSA_DH = 64
NSA_WIDTH = NSA_HEADS * NSA_DH
NSA_KV = NSA_GROUPS * NSA_DH
CMP_LEN = 32
CMP_STRIDE = 16
CMP_HIDDEN = 256
SLC_LEN = 64
SLC_TOPK = 16
WIN = 512
REL_BUCKETS = 32
REL_MAX_DIST = 128
N_EXPERT_GROUPS = 8
EXPERTS_PER_GROUP = 8
N_EXPERTS = N_EXPERT_GROUPS * EXPERTS_PER_GROUP
EXPERT_DFF = 1024
RMS_EPS = 1e-6
NEG_INF = -1e30
LOG2E = math.log2(math.e)
NSA_Q_SCALE = NSA_DH ** -0.5 * LOG2E
BIG = 1e9

MAIN_COLS = 4 * HG_WIDTH + NSA_WIDTH + 6 * NSA_KV
GATE_COLS = 3 * NSA_HEADS
LANES = 128
VMEM_LIMIT = 56 * 1024 * 1024

INPROJ_TM, INPROJ_TN = 1024, 512
HGRN_TB = 1024
OUTPROJ_TM = 512


def _cparams(*sem):
    return pltpu.CompilerParams(dimension_semantics=sem, vmem_limit_bytes=VMEM_LIMIT)


def _nt(a, b):
    return lax.dot_general(a, b, (((1,), (1,)), ((), ())), preferred_element_type=F32)


def _tn(a, b):
    return lax.dot_general(a, b, (((0,), (0,)), ((), ())), preferred_element_type=F32)


def _dot(a, b):
    return jnp.dot(a, b, preferred_element_type=F32)


def _pack_bf16_pairs(x):
    half = x.shape[1] // 2
    bits = pltpu.bitcast(x.astype(BF16).astype(F32), jnp.int32)
    return lax.shift_right_logical(bits[:, :half], 16) | (bits[:, half:] & jnp.int32(-65536))


def _unpack_bf16_pairs(bits):
    return (pltpu.bitcast(lax.shift_left(bits, 16), F32),
            pltpu.bitcast(bits & jnp.int32(-65536), F32))


def _inproj_body(x_ref, g_ref, w_ref, wg_ref, oh_ref, on_ref, og_ref, h_scr, *, hg_tiles, q_tiles):
    j = pl.program_id(1)

    @pl.when(j == 0)
    def _():
        x = x_ref[...]
        y = x * lax.rsqrt(jnp.mean(x * x, axis=-1, keepdims=True) + RMS_EPS) * g_ref[...]
        h_scr[...] = y.astype(BF16)
        og_ref[...] = _dot(h_scr[...], wg_ref[...])

    @pl.when(j < hg_tiles)
    def _():
        oh_ref[...] = _dot(h_scr[...], w_ref[...])

    @pl.when(j >= hg_tiles)
    def _():
        scale = jnp.where(j < hg_tiles + q_tiles, NSA_Q_SCALE, 1.0)
        on_ref[...] = (_dot(h_scr[...], w_ref[...]) * scale).astype(on_ref.dtype)


def _inproj(x2d, g, w_main, w_gate, tm, tn):
    T, D = x2d.shape
    N = w_main.shape[1]
    n_hg = 4 * HG_WIDTH
    hg_tiles = n_hg // tn
    return pl.pallas_call(
        functools.partial(_inproj_body, hg_tiles=hg_tiles, q_tiles=NSA_WIDTH // tn),
        grid=(T // tm, N // tn),
        in_specs=[
            pl.BlockSpec((tm, D), lambda i, j: (i, 0)),
            pl.BlockSpec((1, D), lambda i, j: (0, 0)),
            pl.BlockSpec((D, tn), lambda i, j: (0, j)),
            pl.BlockSpec((D, LANES), lambda i, j: (0, 0)),
        ],
        out_specs=[
            pl.BlockSpec((tm, tn), lambda i, j: (i, jnp.minimum(j, hg_tiles - 1))),
            pl.BlockSpec((tm, tn), lambda i, j: (i, jnp.maximum(j - hg_tiles, 0))),
            pl.BlockSpec((tm, LANES), lambda i, j: (i, 0)),
        ],
        out_shape=[jax.ShapeDtypeStruct((T, n_hg), F32),
                   jax.ShapeDtypeStruct((T, N - n_hg), BF16),
                   jax.ShapeDtypeStruct((T, LANES), F32)],
        scratch_shapes=[pltpu.VMEM((tm, D), BF16)],
        compiler_params=_cparams("arbitrary", "arbitrary"),
        name="inproj",
    )(x2d, g, w_main, w_gate)


def _hgrn_tables(C):
    t = np.arange(C)
    levels = []
    m = C // 2
    while m >= 1:
        levels.append(m)
        m //= 2
    rows = [np.tril(np.ones((C, C), np.float32)),
            np.triu(np.ones((C, C), np.float32), 1)]
    lev = np.full((C, C), -1, np.int32)
    for li, m in enumerate(levels):
        r = t % (2 * m)
        mid = t - r + m - 1
        M = np.zeros((C, C), np.float32)
        for i in range(C):
            if r[i] >= m:
                M[i, mid[i] + 1:i + 1] = 1.0
            else:
                M[i, i + 1:mid[i] + 1] = 1.0
        rows.append(M)
        same = (t[:, None] // (2 * m)) == (t[None, :] // (2 * m))
        lev[same & (r[:, None] >= m) & (r[None, :] < m)] = li
    lev[t[:, None] == t[None, :]] = len(levels)
    return np.concatenate(rows, 0), lev, len(levels)


HG_HEADS_PER_STEP = 8


def _hgrn_body(q_ref, f_ref, i_ref, g_ref, lb_ref, ng_ref, mst_ref, lev_ref, o_ref, st_ref,
               *, C, n_lev):
    @pl.when(pl.program_id(2) == 0)
    def _():
        st_ref[...] = jnp.zeros_like(st_ref)

    ng = ng_ref[...]
    mst = mst_ref[...]
    lev = lev_ref[...]
    n_chunks = q_ref.shape[0] // C

    def gates(rows, hb):
        cols = slice(hb * HG_DIM, (hb + 1) * HG_DIM)
        lb = lb_ref[hb]
        q = jax.nn.silu(q_ref[rows, cols])
        f = lb + (1.0 - lb) * jax.nn.sigmoid(f_ref[rows, cols])
        lf = jnp.log(f)
        lf0 = lf.astype(BF16)
        r1 = lf - lf0.astype(F32)
        lf1 = r1.astype(BF16)
        lf2 = (r1 - lf1.astype(F32)).astype(BF16)
        return q, 1.0 - f, i_ref[rows, cols].astype(BF16), jnp.concatenate([lf0, lf1, lf2], axis=0)

    def exponents(lf_a, lf_b):
        e = _dot(mst, jnp.concatenate([lf_a, lf_b], axis=1))
        return e[:, 0:HG_DIM], e[:, HG_DIM:2 * HG_DIM]

    def scores(q, k, gdec):
        a = jnp.where(lev == n_lev, _nt(q.astype(BF16), k.astype(BF16)), 0.0)
        for li in range(n_lev):
            g_m = gdec[(2 + li) * C:(3 + li) * C]
            a_m = _nt((q * g_m).astype(BF16), (k * g_m).astype(BF16))
            a = jnp.where(lev == li, a_m, a)
        return a.astype(BF16)

    def finish(rows, hb, q, k, v, gdec, a):
        cols = slice(hb * HG_DIM, (hb + 1) * HG_DIM)
        g_b = gdec[0:C]
        g_r = gdec[C:2 * C]
        st = st_ref[hb]
        o = _nt((q * g_b).astype(BF16), st.astype(BF16)) + _dot(a, v)
        g_last = g_b[C - 1:C, :]
        st_ref[hb] = st * g_last + _tn(v, (k * g_r).astype(BF16))
        o = o * lax.rsqrt(jnp.mean(o * o, axis=-1, keepdims=True) + RMS_EPS) * ng
        o_ref[rows, cols] = (o * jax.nn.silu(g_ref[rows, cols])).astype(o_ref.dtype)

    def chunk(c, carry):
        rows = pl.ds(pl.multiple_of(c * C, C), C)
        hs = range(HG_HEADS_PER_STEP)
        qkve = [gates(rows, hb) for hb in hs]
        e = []
        for hb in range(0, HG_HEADS_PER_STEP, 2):
            e.extend(exponents(qkve[hb][3], qkve[hb + 1][3]))
        gdec = [jnp.exp(x) for x in e]
        a = [scores(qkve[hb][0], qkve[hb][1], gdec[hb]) for hb in hs]
        for hb in hs:
            finish(rows, hb, qkve[hb][0], qkve[hb][1], qkve[hb][2], gdec[hb], a[hb])
        return carry

    lax.fori_loop(0, n_chunks, chunk, 0)


def _hgrn(proj, lb, ng, B, S, tb):
    C = HG_CHUNK
    mst, lev, n_lev = _hgrn_tables(C)
    mst = np.concatenate([mst] * 3, axis=1)
    nb = S // tb
    HB = HG_HEADS_PER_STEP
    hsteps = HG_HEADS // HB

    def col(group):
        return pl.BlockSpec((tb, HB * HG_DIM), lambda b, h, t: (b * nb + t, group * hsteps + h))

    return pl.pallas_call(
        functools.partial(_hgrn_body, C=C, n_lev=n_lev),
        grid=(B, hsteps, nb),
        in_specs=[
            col(0), col(1), col(2), col(3),
            pl.BlockSpec((HB, 1, HG_DIM), lambda b, h, t: (h, 0, 0)),
            pl.BlockSpec((1, HG_DIM), lambda b, h, t: (0, 0)),
            pl.BlockSpec(mst.shape, lambda b, h, t: (0, 0)),
            pl.BlockSpec(lev.shape, lambda b, h, t: (0, 0)),
        ],
        out_specs=pl.BlockSpec((tb, HB * HG_DIM), lambda b, h, t: (b * nb + t, h)),
        out_shape=jax.ShapeDtypeStruct((B * S, HG_WIDTH), BF16),
        scratch_shapes=[pltpu.VMEM((HB, HG_DIM, HG_DIM), F32)],
        compiler_params=_cparams("arbitrary", "arbitrary", "arbitrary"),
        name="hgrn2",
    )(proj, proj, proj, proj, lb, ng, jnp.asarray(mst, BF16), jnp.asarray(lev))


def _compress_body(h_ref, pe_ref, w1_ref, w2_ref, o_ref):
    hv = h_ref[...]
    half = hv.shape[1]
    nh = hv.shape[0]
    u = _dot((hv + pe_ref[0:1, :]).astype(BF16), w1_ref[0:half, :].astype(BF16))
    v = _dot((hv + pe_ref[1:2, :]).astype(BF16), w1_ref[half:2 * half, :].astype(BF16))
    pre = u + pltpu.roll(v, nh - 1, axis=0)
    o_ref[...] = _dot(jax.nn.silu(pre).astype(BF16), w2_ref[...].astype(BF16)).astype(o_ref.dtype)


def _compress(halves, pe, w1, w2):
    _, B, G, NH, HW = halves.shape
    return pl.pallas_call(
        _compress_body,
        grid=(2, B, G),
        in_specs=[
            pl.BlockSpec((None, None, None, NH, HW), lambda s, b, g: (s, b, g, 0, 0)),
            pl.BlockSpec((None, 2, HW), lambda s, b, g: (s, 0, 0)),
            pl.BlockSpec((None, 2 * HW, CMP_HIDDEN), lambda s, b, g: (s, 0, 0)),
            pl.BlockSpec((None, CMP_HIDDEN, NSA_DH), lambda s, b, g: (s, 0, 0)),
        ],
        out_specs=pl.BlockSpec((None, None, None, NH, NSA_DH), lambda s, b, g: (s, b, g, 0, 0)),
        out_shape=jax.ShapeDtypeStruct((2, B, G, NH, NSA_DH), BF16),
        compiler_params=_cparams("arbitrary", "arbitrary", "arbitrary"),
        name="nsa_compress",
    )(halves, pe, w1, w2)


def _nsa_compress_all(proj, pe_k, w1_k, w2_k, pe_v, w1_v, w2_v, B, S):
    NH = S // CMP_STRIDE
    c0 = NSA_WIDTH

    def halves(idx):
        t = proj[:, c0 + idx * NSA_KV:c0 + (idx + 1) * NSA_KV]
        t = t.reshape(B, NH, CMP_STRIDE, NSA_GROUPS, NSA_DH).transpose(0, 3, 1, 2, 4)
        return t.reshape(B, NSA_GROUPS, NH, CMP_STRIDE * NSA_DH)

    hw = CMP_STRIDE * NSA_DH
    return _compress(jnp.stack([halves(0), halves(1)]),
                     jnp.stack([pe_k.reshape(2, hw), pe_v.reshape(2, hw)]),
                     jnp.stack([w1_k, w1_v]), jnp.stack([w2_k, w2_v]))


NSA_TQ = 128


NSA_MASK_ROWS = 64
NSA_WIN_MASK_ROWS = 16
NSA_FAR = 4
NSA_SLC_PAD = (NSA_FAR - 1) * NSA_TQ


NSA_V_ROWS = NSA_DH + 16


def _softmax_cols(s):
    m = jnp.max(s, axis=0, keepdims=True)
    return m, jnp.exp2(s - m).astype(BF16)


def _normalise(acc):
    return acc[0:NSA_DH] * (1.0 / acc[NSA_DH:NSA_DH + 1])


NSA_GROUPS_PER_STEP = 4


def _nsa_body(qt_ref, gl_ref, kc_ref, vct_ref, ks_ref, vst_ref, kw_ref, vwt_ref,
              bc_ref, bw_ref, bn_ref, ovt_ref, o_ref, *, top_k, n_slc):
    TQ = NSA_TQ
    GG = NSA_GROUPS_PER_STEP
    i = pl.program_id(2)
    ovt = ovt_ref[...]
    jb = lax.broadcasted_iota(jnp.int32, (n_slc, TQ), 0)
    qblk = (i * TQ + lax.broadcasted_iota(jnp.int32, (n_slc, TQ), 1)) // SLC_LEN
    forced = (jb == 0) | (jb == qblk) | (jb == qblk - 1)
    future = jb > qblk
    near = pl.ds(pl.multiple_of(NSA_SLC_PAD + (i - 1) * TQ, TQ), 2 * TQ)

    def heads(a):
        return jnp.concatenate([a] * NSA_HPG, axis=1)

    def queries(g):
        qt = qt_ref[g]
        return jnp.concatenate([qt[h * NSA_DH:(h + 1) * NSA_DH, :] for h in range(NSA_HPG)],
                               axis=1)

    def compressed(g, s):
        m = jnp.maximum(jnp.max(s, axis=0, keepdims=True), 0.1 * NEG_INF)
        p = jnp.exp2(s - m)
        acc = _dot(vct_ref[g], p.astype(BF16))
        inv_l = 1.0 / jnp.maximum(acc[NSA_DH:NSA_DH + 1], 1e-30)
        o_cmp = acc[0:NSA_DH] * inv_l
        pc = p * inv_l
        ps = pc[:, 0:TQ] + pc[:, TQ:2 * TQ] + pc[:, 2 * TQ:3 * TQ] + pc[:, 3 * TQ:4 * TQ]
        ps_hi = ps.astype(BF16)
        ps_lo = (ps - ps_hi.astype(F32)).astype(BF16)
        imp = _dot(ovt, ps_hi) + _dot(ovt, ps_lo)
        return o_cmp, jnp.where(forced, BIG, jnp.where(future, -BIG, imp))

    def window(g, s):
        _, p = _softmax_cols(s)
        return _normalise(_dot(vwt_ref[g, :, win], p))

    def select(imp, q4):
        slabs = []
        for v in range(n_slc // 8):
            slab = imp[8 * v:8 * v + 8, :]
            jbs = jb[8 * v:8 * v + 8, :]
            cnt = jnp.zeros((8, TQ), F32)
            for jp in range(n_slc):
                row = imp[jp:jp + 1, :]
                if jp < 8 * v:
                    beats = row >= slab
                elif jp >= 8 * v + 8:
                    beats = row > slab
                else:
                    beats = (row > slab) | ((row == slab) & (jbs > jp))
                cnt = cnt + jnp.where(beats, 1.0, 0.0)
            slabs.append(jnp.where(cnt < top_k, 0.0, 1.0))
        if n_slc < NSA_MASK_ROWS:
            slabs.append(jnp.ones((NSA_MASK_ROWS - n_slc, TQ), F32))
        unsel = jnp.concatenate(slabs, axis=0)
        return jnp.concatenate([heads(unsel.astype(BF16)), q4], axis=0)

    def near_tiles(g, s):
        m, p = _softmax_cols(s)
        return m, _dot(vst_ref[g, :, near], p)

    win = pl.ds(pl.multiple_of(i * TQ, TQ), WIN + TQ)
    pad_row = jnp.where(lax.broadcasted_iota(jnp.int32, (NSA_WIN_MASK_ROWS, NSA_HPG * TQ), 0) == 0,
                        1.0, 0.0).astype(BF16)
    gs = range(GG)
    q4 = [queries(g) for g in gs]
    nh = kc_ref.shape[1]
    cmp_rows = pl.ds(pl.multiple_of(nh - (TQ // CMP_STRIDE) * i, TQ // CMP_STRIDE), nh)
    s_cmp = [_dot(kc_ref[g], q4[g]) + bc_ref[g, cmp_rows, :] for g in gs]
    s_win = [_dot(kw_ref[g, win, :], jnp.concatenate([pad_row, q4[g]], axis=0)) + bw_ref[g] for g in gs]
    cmp_out = [compressed(g, s_cmp[g]) for g in gs]
    o_win = [window(g, s_win[g]) for g in gs]
    qa = [select(cmp_out[g][1], q4[g]) for g in gs]
    s_near = [_dot(ks_ref[g, near, :], qa[g]) + bn_ref[g] for g in gs]
    start = tuple(near_tiles(g, s_near[g]) for g in gs)

    n_far = i - 1
    n_chunks = jnp.maximum((n_far + NSA_FAR - 1) // NSA_FAR, 0)
    far0 = NSA_SLC_PAD + TQ * (n_far - NSA_FAR * n_chunks)

    def far_step(c, carry):
        keys = pl.ds(pl.multiple_of(far0 + c * NSA_FAR * TQ, TQ), NSA_FAR * TQ)
        out = []
        logits = [_dot(ks_ref[g, keys, :], qa[g]) for g in gs]
        for g in gs:
            m, acc = carry[g]
            s = logits[g]
            m_new = jnp.maximum(m, jnp.max(s, axis=0, keepdims=True))
            p = jnp.exp2(s - m_new).astype(BF16)
            acc = jnp.exp2(m - m_new) * acc + _dot(vst_ref[g, :, keys], p)
            out.append((m_new, acc))
        return tuple(out)

    far = lax.fori_loop(0, n_chunks, far_step, start)

    for g in gs:
        o_slc = _normalise(far[g][1])
        o_cmp = cmp_out[g][0]
        sg = jax.nn.sigmoid(gl_ref[g])
        outs = []
        for h in range(NSA_HPG):
            lanes = slice(h * TQ, (h + 1) * TQ)
            outs.append(sg[h:h + 1, :] * o_cmp[:, lanes]
                        + sg[NSA_HPG + h:NSA_HPG + h + 1, :] * o_slc[:, lanes]
                        + sg[2 * NSA_HPG + h:2 * NSA_HPG + h + 1, :] * o_win[g][:, lanes])
        width = NSA_HPG * NSA_DH
        o_ref[:, g * width:(g + 1) * width] = jnp.concatenate(outs, axis=0).T.astype(o_ref.dtype)


def _rel_bucket(dist):
    n = jnp.maximum(dist, 0)
    max_exact = REL_BUCKETS // 2
    nf = jnp.maximum(n, 1).astype(F32)
    large = max_exact + (jnp.log(nf / max_exact) / math.log(REL_MAX_DIST / max_exact)
                         * (REL_BUCKETS - max_exact)).astype(jnp.int32)
    large = jnp.minimum(large, REL_BUCKETS - 1)
    return jnp.where(n < max_exact, n, large)


def _bias_table(table, dist, valid):
    onehot = (_rel_bucket(jnp.asarray(dist, jnp.int32))[..., None]
              == jnp.arange(REL_BUCKETS, dtype=jnp.int32)).astype(F32)
    b = jnp.einsum('rmk,ghk->ghrm', onehot, table, precision=lax.Precision.HIGHEST)
    return jnp.where(jnp.asarray(valid), b, NEG_INF)


def _relayout_body(x_ref, ms_ref, qt_ref, ks_ref, vst_ref, kw_ref, vwt_ref, *, lead):
    TQ = NSA_TQ
    real = pl.program_id(1) >= lead
    x = x_ref[...]

    def columns(idx):
        lo = NSA_WIDTH + idx * NSA_KV
        return jnp.where(real, x[:, lo:lo + NSA_KV], 0.0)

    @pl.when(real)
    def _():
        qt = x[:, 0:NSA_WIDTH].astype(F32).T.astype(BF16)
        qt_ref[...] = qt.reshape(qt_ref.shape)

    ones = jnp.where((lax.broadcasted_iota(jnp.int32, (NSA_V_ROWS - NSA_DH, TQ), 0) == 0) & real,
                     1.0, 0.0).astype(BF16)
    pad_col = jnp.where((lax.broadcasted_iota(jnp.int32, (TQ, NSA_WIN_MASK_ROWS), 1) == 0) & ~real,
                        NEG_INF, 0.0).astype(BF16)
    ms = ms_ref[...]
    k_slc, k_win = columns(2), columns(4)
    v_slc = columns(3).astype(F32).T.astype(BF16)
    v_win = columns(5).astype(F32).T.astype(BF16)
    for g in range(NSA_GROUPS):
        d = slice(g * NSA_DH, (g + 1) * NSA_DH)
        ks_ref[g] = jnp.concatenate([ms, k_slc[:, d]], axis=1)
        kw_ref[g] = jnp.concatenate([pad_col, k_win[:, d]], axis=1)
        vst_ref[g] = jnp.concatenate([v_slc[d, :], ones], axis=0)
        vwt_ref[g] = jnp.concatenate([v_win[d, :], ones], axis=0)


def _nsa_relayout(proj, slc_masks, B, S):
    TQ = NSA_TQ
    G, DH, SP = NSA_GROUPS, NSA_DH, NSA_SLC_PAD
    nt = S // TQ
    lead = WIN // TQ
    slc_lead = SP // TQ
    tok = lambda u: jnp.maximum(u - lead, 0)
    slc = lambda u: jnp.maximum(u - (lead - slc_lead), 0)
    return pl.pallas_call(
        functools.partial(_relayout_body, lead=lead),
        grid=(B, lead + nt),
        in_specs=[pl.BlockSpec((TQ, proj.shape[1]), lambda b, u: (b * nt + tok(u), 0)),
                  pl.BlockSpec((TQ, NSA_MASK_ROWS), lambda b, u: (slc(u), 0))],
        out_specs=[
            pl.BlockSpec((None, G, NSA_HPG * DH, TQ), lambda b, u: (b, 0, 0, tok(u))),
            pl.BlockSpec((None, G, TQ, NSA_MASK_ROWS + DH), lambda b, u: (b, 0, slc(u), 0)),
            pl.BlockSpec((None, G, NSA_V_ROWS, TQ), lambda b, u: (b, 0, 0, slc(u))),
            pl.BlockSpec((None, G, TQ, NSA_WIN_MASK_ROWS + DH), lambda b, u: (b, 0, u, 0)),
            pl.BlockSpec((None, G, NSA_V_ROWS, TQ), lambda b, u: (b, 0, 0, u)),
        ],
        out_shape=[
            jax.ShapeDtypeStruct((B, G, NSA_HPG * DH, S), BF16),
            jax.ShapeDtypeStruct((B, G, S + SP, NSA_MASK_ROWS + DH), BF16),
            jax.ShapeDtypeStruct((B, G, NSA_V_ROWS, S + SP), BF16),
            jax.ShapeDtypeStruct((B, G, S + WIN, NSA_WIN_MASK_ROWS + DH), BF16),
            jax.ShapeDtypeStruct((B, G, NSA_V_ROWS, S + WIN), BF16),
        ],
        compiler_params=_cparams("arbitrary", "arbitrary"),
        name="nsa_relayout",
    )(proj, slc_masks)


def _nsa(proj, glog, kc, vc, rel_bias, B, S):
    TQ = NSA_TQ
    G, HPG, DH = NSA_GROUPS, NSA_HPG, NSA_DH
    NH = S // CMP_STRIDE
    n_cmp = (S - CMP_LEN) // CMP_STRIDE + 1
    n_slc = S // SLC_LEN
    top_k = min(SLC_TOPK, n_slc)
    nt = S // TQ
    c0 = NSA_WIDTH

    MR = NSA_MASK_ROWS
    assert n_slc % 8 == 0 and (n_slc < MR or (n_slc == MR and top_k < n_slc))

    SP = NSA_SLC_PAD
    pos = np.arange(SP + S) - SP
    cols = np.zeros((SP + S, MR), np.float32)
    cols[:, :n_slc] = ((pos // SLC_LEN)[:, None] == np.arange(n_slc)[None, :]) & (pos >= 0)[:, None]
    if n_slc < MR:
        cols[:, n_slc] = pos < 0
    else:
        cols[pos < 0, :] = 1.0
    qt, ks, vst, kw, vwt = _nsa_relayout(proj, jnp.asarray(cols * NEG_INF, BF16), B, S)
    kcb = kc
    ones = jnp.zeros((NSA_V_ROWS - DH, NH), BF16).at[0].set(1.0)
    vct = jnp.concatenate([vc.transpose(0, 1, 3, 2), jnp.broadcast_to(ones, (B, G) + ones.shape)],
                          axis=2)

    table = rel_bias.T.reshape(G, HPG, REL_BUCKETS) * LOG2E
    far = table[:, :, REL_BUCKETS - 1]
    r = np.arange(TQ)[:, None]

    def keys_major(b):
        return b.transpose(0, 3, 1, 2).reshape(G, b.shape[3], HPG * TQ)

    d_c = r - (CMP_LEN - 1) - CMP_STRIDE * (np.arange(2 * NH)[None, :] - NH)
    bias_cmp = keys_major(_bias_table(table, d_c, d_c >= 0))
    d_w = r + WIN - np.arange(WIN + TQ)[None, :]
    bias_win = keys_major(_bias_table(table, d_w, (d_w >= 0) & (d_w < WIN)))
    d_n = r + TQ - np.arange(2 * TQ)[None, :]
    bias_near = keys_major(_bias_table(table - far[:, :, None], d_n, d_n >= 0))

    ci = np.arange(NH)[None, :] * CMP_STRIDE
    sj = np.arange(n_slc)[:, None] * SLC_LEN
    overlap_t = ((ci <= sj + SLC_LEN - 1) & (ci + CMP_LEN - 1 >= sj) & (np.arange(NH)[None, :] < n_cmp))

    gate_rows = 16
    glog_g = glog[:, :GATE_COLS].reshape(B * S, 3, G, HPG).transpose(2, 1, 3, 0).reshape(G, 3 * HPG, B * S)
    glog_g = jnp.pad(glog_g, ((0, 0), (0, gate_rows - 3 * HPG), (0, 0)))

    GG = NSA_GROUPS_PER_STEP
    whole = lambda shape: pl.BlockSpec((None, GG) + shape, lambda b, g, i: (b, g, 0, 0))
    tile = lambda k: pl.BlockSpec((GG, k, HPG * TQ), lambda b, g, i: (g, 0, 0))
    return pl.pallas_call(
        functools.partial(_nsa_body, top_k=top_k, n_slc=n_slc),
        grid=(B, G // GG, nt),
        in_specs=[
            pl.BlockSpec((None, GG, HPG * DH, TQ), lambda b, g, i: (b, g, 0, i)),
            pl.BlockSpec((GG, gate_rows, TQ), lambda b, g, i: (g, 0, b * nt + i)),
            whole((NH, DH)), whole((NSA_V_ROWS, NH)),
            whole((S + SP, MR + DH)), whole((NSA_V_ROWS, S + SP)),
            whole((S + WIN, NSA_WIN_MASK_ROWS + DH)), whole((NSA_V_ROWS, S + WIN)),
            tile(2 * NH), tile(WIN + TQ), tile(2 * TQ),
            pl.BlockSpec((n_slc, NH), lambda b, g, i: (0, 0)),
        ],
        out_specs=pl.BlockSpec((TQ, GG * HPG * DH), lambda b, g, i: (b * nt + i, g)),
        out_shape=jax.ShapeDtypeStruct((B * S, NSA_WIDTH), BF16),
        compiler_params=_cparams("arbitrary", "arbitrary", "arbitrary"),
        name="nsa_attention",
    )(qt, glog_g, kcb, vct, ks, vst, kw, vwt, bias_cmp, bias_win, bias_near,
      jnp.asarray(overlap_t, BF16))


ROUTE_E1, ROUTE_E2, ROUTE_W1, ROUTE_W2 = 0, 1, 2, 3


def _outproj_body(x_ref, yh_ref, yn_ref, wo_ref, g2_ref, wrh_ref, wrl_ref, br_ref,
                  x2_ref, h2_ref, rt_ref):
    x2 = (x_ref[...] + _dot(yh_ref[...], wo_ref[0:HG_WIDTH, :])
          + _dot(yn_ref[...], wo_ref[HG_WIDTH:HG_WIDTH + NSA_WIDTH, :]))
    x2_ref[...] = x2
    h = x2 * lax.rsqrt(jnp.mean(x2 * x2, axis=-1, keepdims=True) + RMS_EPS) * g2_ref[...]
    hb = h.astype(BF16)
    h2_ref[...] = _pack_bf16_pairs(h)
    hl = (h - hb.astype(F32)).astype(BF16)
    wrh = wrh_ref[...]
    lg = _dot(hb, wrh) + _dot(hl, wrh) + _dot(hb, wrl_ref[...]) + br_ref[...]
    lane = lax.broadcasted_iota(jnp.int32, lg.shape, 1)
    first = lambda hit: jnp.min(jnp.where(hit, lane, LANES), axis=-1, keepdims=True)
    lgg = jnp.where(lane < N_EXPERT_GROUPS, lg, NEG_INF)
    mg = jnp.max(lgg, axis=-1, keepdims=True)
    p_top = 1.0 / jnp.sum(jnp.exp(lgg - mg), axis=-1, keepdims=True)
    lo = N_EXPERT_GROUPS + EXPERTS_PER_GROUP * first(lgg == mg)
    le = jnp.where((lane >= lo) & (lane < lo + EXPERTS_PER_GROUP), lg, NEG_INF)
    v1 = jnp.max(le, axis=-1, keepdims=True)
    i1 = first(le == v1)
    le = jnp.where(lane == i1, NEG_INF, le)
    v2 = jnp.max(le, axis=-1, keepdims=True)
    i2 = first(le == v2)
    e21 = jnp.exp(v2 - v1)
    w1 = p_top / (1.0 + e21)
    rt = jnp.where(lane == ROUTE_E1, (i1 - N_EXPERT_GROUPS).astype(F32), 0.0)
    rt = jnp.where(lane == ROUTE_E2, (i2 - N_EXPERT_GROUPS).astype(F32), rt)
    rt = jnp.where(lane == ROUTE_W1, w1, rt)
    rt_ref[...] = jnp.where(lane == ROUTE_W2, w1 * e21, rt)


def _outproj(x2d, y_hg, y_nsa, w_out, g2, wr_hi, wr_lo, b_r, tm):
    T, D = x2d.shape
    row = lambda n: pl.BlockSpec((tm, n), lambda i: (i, 0))
    full = lambda a: pl.BlockSpec(a.shape, lambda i: (0, 0))
    return pl.pallas_call(
        _outproj_body,
        grid=(T // tm,),
        in_specs=[row(D), row(HG_WIDTH), row(NSA_WIDTH), full(w_out), full(g2),
                  full(wr_hi), full(wr_lo), full(b_r)],
        out_specs=[row(D), row(D // 2), row(LANES)],
        out_shape=[jax.ShapeDtypeStruct((T, D), F32), jax.ShapeDtypeStruct((T, D // 2), jnp.int32),
                   jax.ShapeDtypeStruct((T, LANES), F32)],
        compiler_params=_cparams("arbitrary"),
        name="outproj_router",
    )(x2d, y_hg, y_nsa, w_out, g2, wr_hi, wr_lo, b_r)


ROW_TILE = 512


def _dispatch_body(dest_ref, h_ref, xs0_hbm, xs_hbm, sem):
    del xs0_hbm
    tb = h_ref.shape[0]

    def start(t, carry):
        for k in range(2):
            pltpu.make_async_copy(h_ref.at[pl.ds(t, 1), :],
                                  xs_hbm.at[pl.ds(dest_ref[0, 2 * t + k], 1), :], sem).start()
        return carry

    lax.fori_loop(0, tb, start, 0, unroll=16)
    for _ in range(2):
        pltpu.make_async_copy(h_ref, xs_hbm.at[pl.ds(0, tb), :], sem).wait()


def _dispatch(h2p, dest, n_rows):
    T, W = h2p.shape
    tb = ROW_TILE
    return pl.pallas_call(
        _dispatch_body,
        grid=(T // tb,),
        in_specs=[pl.BlockSpec((None, 1, 2 * tb), lambda i: (i, 0, 0), memory_space=pltpu.SMEM),
                  pl.BlockSpec((tb, W), lambda i: (i, 0)),
                  pl.BlockSpec(memory_space=pl.ANY)],
        out_specs=pl.BlockSpec(memory_space=pl.ANY),
        out_shape=jax.ShapeDtypeStruct((n_rows, W), jnp.int32),
        scratch_shapes=[pltpu.SemaphoreType.DMA(())],
        input_output_aliases={2: 0},
        compiler_params=_cparams("arbitrary"),
        name="moe_dispatch",
    )(dest.reshape(T // tb, 1, 2 * tb), h2p, jnp.zeros((n_rows, W), jnp.int32))


MOE_ALIGN = 128
MOE_TM = 1024
MOE_SUB = 256
MOE_TF = 256
MOE_CAST_PIECES = 4
MOE_WEIGHT_BUFFERS = 3


def _moe_body(ie_ref, ir_ref, in_ref, nb_ref, xs_hbm, wg_hbm, wu_hbm, wd_hbm, y_hbm,
              xbuf, xlo, xhi, acc, ypk, wgb, wub, wdb, wgf, wuf, wdf, sem_in, sem_out, sem_w):
    w = pl.program_id(0)
    f = pl.program_id(1)
    n_w = pl.num_programs(0)
    n_f = pl.num_programs(1)
    last_f = n_f - 1
    ring = wgf.shape[0]
    step = w * n_f + f

    def weight_copies(k):
        item = jnp.minimum(k // n_f, n_w - 1)
        wanted = (k < n_w * n_f) & (in_ref[item] > 0)
        e = ie_ref[item]
        cols = pl.ds(pl.multiple_of((k % n_f) * MOE_TF, MOE_TF), MOE_TF)
        s = k % ring
        return wanted, (pltpu.make_async_copy(wg_hbm.at[e, :, cols], wgf.at[s], sem_w.at[0, s]),
                        pltpu.make_async_copy(wu_hbm.at[e, :, cols], wuf.at[s], sem_w.at[1, s]),
                        pltpu.make_async_copy(wd_hbm.at[e, cols, :], wdf.at[s], sem_w.at[2, s]))

    def request_weights(k):
        wanted, copies = weight_copies(k)

        @pl.when(wanted)
        def _():
            for c in copies:
                c.start()

    @pl.when(step == 0)
    def _():
        for k in range(1, ring):
            request_weights(k)

    request_weights(step + ring)
    cast_next, arriving = weight_copies(step + 1)

    @pl.when(cast_next)
    def _():
        for c in arriving:
            c.wait()

    nslot = (step + 1) % ring
    nsub = in_ref[w]
    row0 = pl.multiple_of(ir_ref[w] * MOE_ALIGN, MOE_ALIGN)
    slot = w % 2
    per_mm = MOE_SUB // MOE_ALIGN
    n_full = nsub // per_mm
    n_iter = n_full + nsub % per_mm
    half = xlo.shape[1]
    cur = step % 2

    def cast_piece(p):
        rk = wgf.shape[1] // MOE_CAST_PIECES
        rows = pl.ds(pl.multiple_of(p * rk, rk), rk)
        wgb[1 - cur, rows, :] = wgf[nslot, rows, :].astype(BF16)
        wub[1 - cur, rows, :] = wuf[nslot, rows, :].astype(BF16)
        rf = wdf.shape[1] // MOE_CAST_PIECES
        rows = pl.ds(pl.multiple_of(p * rf, rf), rf)
        wdb[1 - cur, rows, :] = wdf[nslot, rows, :].astype(BF16)

    def x_copy(item, s):
        r = pl.multiple_of(ir_ref[item] * MOE_ALIGN, MOE_ALIGN)
        return pltpu.make_async_copy(xs_hbm.at[pl.ds(r, MOE_TM), :], xbuf.at[s], sem_in.at[s])

    def y_copy(r, dst_row):
        return pltpu.make_async_copy(ypk.at[pl.ds(r, MOE_ALIGN), :],
                                     y_hbm.at[pl.ds(dst_row, MOE_ALIGN), :], sem_out)

    def for_row_blocks(fn):
        def body(j, carry):
            fn(pl.multiple_of(j * MOE_SUB, MOE_SUB), MOE_SUB, j)
            return carry

        lax.fori_loop(0, n_full, body, 0)

        @pl.when(nsub % per_mm == 1)
        def _():
            fn(pl.multiple_of(n_full * MOE_SUB, MOE_ALIGN), MOE_ALIGN, n_full)

    @pl.when((f == 0) & (nsub > 0))
    def _():
        @pl.when(in_ref[jnp.maximum(w - 1, 0)] == 0)
        def _():
            x_copy(w, slot).start()

        x_copy(w, slot).wait()

        def unpack(r, size, _):
            lo, hi = _unpack_bf16_pairs(xbuf[slot, pl.ds(r, size), :])
            xlo[pl.ds(r, size), :] = lo.astype(BF16)
            xhi[pl.ds(r, size), :] = hi.astype(BF16)

        for_row_blocks(unpack)

    def experts(first, last):
        def block(r, size, it):
            rows = pl.ds(r, size)
            lo = xlo[rows, :]
            hi = xhi[rows, :]
            hg = _dot(lo, wgb[cur, 0:half, :]) + _dot(hi, wgb[cur, half:2 * half, :])
            hu = _dot(lo, wub[cur, 0:half, :]) + _dot(hi, wub[cur, half:2 * half, :])
            y = _dot((jax.nn.silu(hg) * hu).astype(BF16), wdb[cur])
            if first:
                acc[rows, :] = y
            elif not last:
                acc[rows, :] += y
            else:
                ypk[rows, :] = _pack_bf16_pairs(acc[rows, :] + y)
                for k in range(size // MOE_ALIGN):
                    y_copy(r + k * MOE_ALIGN, row0 + r + k * MOE_ALIGN).start()

            @pl.when(cast_next & (it < MOE_CAST_PIECES))
            def _():
                cast_piece(it)

        for_row_blocks(block)

        @pl.when(cast_next)
        def _():
            lax.fori_loop(jnp.minimum(n_iter, MOE_CAST_PIECES), MOE_CAST_PIECES,
                          lambda p, c: (cast_piece(p), c)[1], 0)

    @pl.when((nsub == 0) & cast_next)
    def _():
        lax.fori_loop(0, MOE_CAST_PIECES, lambda p, c: (cast_piece(p), c)[1], 0)

    @pl.when((f == 0) & (nsub > 0))
    def _():
        experts(True, False)

    @pl.when((f > 0) & (f < last_f) & (nsub > 0))
    def _():
        experts(False, False)

    @pl.when((f == last_f) & (nsub > 0))
    def _():
        nxt = jnp.minimum(w + 1, n_w - 1)

        @pl.when((w + 1 < n_w) & (in_ref[nxt] > 0))
        def _():
            x_copy(nxt, 1 - slot).start()

        experts(False, True)

        def wait(j, carry):
            y_copy(0, 0).wait()
            return carry

        lax.fori_loop(0, nsub, wait, 0)

    @pl.when((f == last_f) & (w == n_w - 1))
    def _():
        ypk[0:MOE_ALIGN, :] = jnp.zeros((MOE_ALIGN, ypk.shape[1]), jnp.int32)
        used = nb_ref[0]
        total = y_hbm.shape[0] // MOE_ALIGN

        def fill(j, carry):
            y_copy(0, pl.multiple_of(j * MOE_ALIGN, MOE_ALIGN)).start()
            return carry

        def wait(j, carry):
            y_copy(0, 0).wait()
            return carry

        lax.fori_loop(used, total, fill, 0)
        lax.fori_loop(used, total, wait, 0)


def _moe(xs, item_e, item_r, item_n, n_blocks, w_gate, w_up, w_down):
    rows = xs.shape[0]
    D = 2 * xs.shape[1]
    n_items = item_e.shape[0]
    nf = EXPERT_DFF // MOE_TF
    assert nf >= 2 and MOE_SUB == 2 * MOE_ALIGN

    R = MOE_WEIGHT_BUFFERS
    return pl.pallas_call(
        _moe_body,
        grid_spec=pltpu.PrefetchScalarGridSpec(
            num_scalar_prefetch=4,
            grid=(n_items, nf),
            in_specs=[
                pl.BlockSpec(memory_space=pl.ANY),
                pl.BlockSpec(memory_space=pl.ANY),
                pl.BlockSpec(memory_space=pl.ANY),
                pl.BlockSpec(memory_space=pl.ANY),
            ],
            out_specs=pl.BlockSpec(memory_space=pl.ANY),
            scratch_shapes=[pltpu.VMEM((2, MOE_TM, D // 2), jnp.int32),
                            pltpu.VMEM((MOE_TM, D // 2), BF16), pltpu.VMEM((MOE_TM, D // 2), BF16),
                            pltpu.VMEM((MOE_TM, D), F32), pltpu.VMEM((MOE_TM, D // 2), jnp.int32),
                            pltpu.VMEM((2, D, MOE_TF), BF16), pltpu.VMEM((2, D, MOE_TF), BF16),
                            pltpu.VMEM((2, MOE_TF, D), BF16),
                            pltpu.VMEM((R, D, MOE_TF), F32), pltpu.VMEM((R, D, MOE_TF), F32),
                            pltpu.VMEM((R, MOE_TF, D), F32),
                            pltpu.SemaphoreType.DMA((2,)), pltpu.SemaphoreType.DMA(()),
                            pltpu.SemaphoreType.DMA((3, R))],
        ),
        out_shape=jax.ShapeDtypeStruct((rows, D // 2), jnp.int32),
        compiler_params=_cparams("arbitrary", "arbitrary"),
        name="moe_experts",
    )(item_e, item_r, item_n, n_blocks, xs, w_gate, w_up, w_down)


def _final_body(dcur_ref, dnext_ref, x2_ref, rt_ref, g_ref, ys_hbm, o_ref, ybuf, sems):
    i = pl.program_id(0)
    tb = x2_ref.shape[0]
    slot = i % 2

    def gather(d_ref, s):
        def body(t, carry):
            for k in range(2):
                pltpu.make_async_copy(ys_hbm.at[pl.ds(d_ref[0, 2 * t + k], 1), :],
                                      ybuf.at[s, k, pl.ds(t, 1), :], sems.at[s]).start()
            return carry

        lax.fori_loop(0, tb, body, 0, unroll=16)

    @pl.when(i == 0)
    def _():
        gather(dcur_ref, 0)

    @pl.when(i + 1 < pl.num_programs(0))
    def _():
        gather(dnext_ref, 1 - slot)

    for k in range(2):
        pltpu.make_async_copy(ys_hbm.at[pl.ds(0, tb), :], ybuf.at[slot, k], sems.at[slot]).wait()
    rt = rt_ref[...]
    lo1, hi1 = _unpack_bf16_pairs(ybuf[slot, 0])
    lo2, hi2 = _unpack_bf16_pairs(ybuf[slot, 1])
    w1 = rt[:, ROUTE_W1:ROUTE_W1 + 1]
    w2 = rt[:, ROUTE_W2:ROUTE_W2 + 1]
    x = x2_ref[...] + jnp.concatenate([w1 * lo1 + w2 * lo2, w1 * hi1 + w2 * hi2], axis=1)
    o_ref[...] = x * lax.rsqrt(jnp.mean(x * x, axis=-1, keepdims=True) + RMS_EPS) * g_ref[...]


def _final(x2, ys, dest, rt, g):
    T, D = x2.shape
    tb = ROW_TILE
    nb = T // tb
    dest3 = dest.reshape(nb, 1, 2 * tb)
    return pl.pallas_call(
        _final_body,
        grid=(nb,),
        in_specs=[pl.BlockSpec((None, 1, 2 * tb), lambda i: (i, 0, 0), memory_space=pltpu.SMEM),
                  pl.BlockSpec((None, 1, 2 * tb), lambda i: (jnp.minimum(i + 1, nb - 1), 0, 0),
                               memory_space=pltpu.SMEM),
                  pl.BlockSpec((tb, D), lambda i: (i, 0)),
                  pl.BlockSpec((tb, LANES), lambda i: (i, 0)),
                  pl.BlockSpec((1, D), lambda i: (0, 0)),
                  pl.BlockSpec(memory_space=pl.ANY)],
        out_specs=pl.BlockSpec((tb, D), lambda i: (i, 0)),
        out_shape=jax.ShapeDtypeStruct((T, D), F32),
        scratch_shapes=[pltpu.VMEM((2, 2, tb, D // 2), jnp.int32), pltpu.SemaphoreType.DMA((2,))],
        compiler_params=_cparams("arbitrary"),
        name="combine_final_norm",
    )(dest3, dest3, x2, rt, g, ys)


def _moe_layout(e_flat):
    A = e_flat.shape[0]
    onehot = (e_flat[:, None] == jnp.arange(N_EXPERTS, dtype=jnp.int32)[None, :]).astype(jnp.int32)
    csum = jnp.cumsum(onehot, axis=0)
    counts = csum[-1]
    nblk = (counts + MOE_ALIGN - 1) // MOE_ALIGN
    blk0 = jnp.cumsum(nblk) - nblk
    dest = jnp.sum(onehot * (blk0[None, :] * MOE_ALIGN + csum - 1), axis=1).astype(jnp.int32)
    per_item = MOE_TM // MOE_ALIGN
    nitem = (nblk + per_item - 1) // per_item
    iend = jnp.cumsum(nitem)
    n_items = (A // MOE_ALIGN + N_EXPERTS + N_EXPERTS * (per_item - 1)) // per_item
    w = jnp.arange(n_items, dtype=jnp.int32)
    total = iend[-1]
    wv = jnp.minimum(w, total - 1)
    item_e = jnp.minimum(jnp.searchsorted(iend, wv, side='right'), N_EXPERTS - 1).astype(jnp.int32)
    sb = wv - (iend - nitem)[item_e]
    item_r = (blk0[item_e] + sb * per_item).astype(jnp.int32)
    item_n = jnp.where(w < total, jnp.minimum(per_item, nblk[item_e] - sb * per_item), 0).astype(jnp.int32)
    n_blocks = jnp.sum(nblk).astype(jnp.int32).reshape(1)
    lead = lambda a, v: jnp.concatenate([jnp.asarray(v, jnp.int32).reshape(1), a])
    return dest, lead(item_e, item_e[0]), lead(item_r, 0), lead(item_n, 0), n_blocks


def kernel(x, norm1_g, w_in, hg_lb_logits, hg_norm_g, cmp_pe_k, cmp_w1_k, cmp_w2_k, cmp_pe_v,
           cmp_w1_v, cmp_w2_v, rel_bias, w_out, norm2_g, w_router_group, b_router_group,
           w_router_expert, b_router_expert, w_expert_gate, w_expert_up, w_expert_down,
           final_norm_g):
    B, S, D = x.shape
    T = B * S
    assert w_in.shape[0] == 1, "single-layer block"
    x2d = x.reshape(T, D)

    lower = jax.nn.softmax(hg_lb_logits.astype(F32), axis=0)[0].reshape(HG_HEADS, 1, HG_DIM)

    w_main = w_in[0, :, :MAIN_COLS].astype(BF16)
    w_gate_cols = jnp.pad(w_in[0, :, MAIN_COLS:].astype(BF16), ((0, 0), (0, LANES - GATE_COLS)))
    proj_hg, proj_nsa, glog = _inproj(x2d, norm1_g[0].reshape(1, D), w_main, w_gate_cols,
                                      tm=INPROJ_TM, tn=INPROJ_TN)

    y_hg = _hgrn(proj_hg, lower, hg_norm_g[0].reshape(1, HG_DIM), B, S, tb=HGRN_TB)
    kcvc = _nsa_compress_all(proj_nsa, cmp_pe_k[0], cmp_w1_k[0], cmp_w2_k[0],
                             cmp_pe_v[0], cmp_w1_v[0], cmp_w2_v[0], B, S)
    y_nsa = _nsa(proj_nsa, glog, kcvc[0], kcvc[1], rel_bias, B, S)

    w_r = jnp.concatenate([w_router_group[0], w_router_expert[0]], axis=1)
    w_r = jnp.pad(w_r, ((0, 0), (0, LANES - w_r.shape[1])))
    wr_hi = w_r.astype(BF16)
    wr_lo = (w_r - wr_hi.astype(F32)).astype(BF16)
    b_r = jnp.concatenate([b_router_group[0], b_router_expert[0]])
    b_r = jnp.pad(b_r, (0, LANES - b_r.shape[0])).reshape(1, LANES)
    x2, h2, rt = _outproj(x2d, y_hg, y_nsa, w_out[0].astype(BF16), norm2_g[0].reshape(1, D),
                          wr_hi, wr_lo, b_r, tm=OUTPROJ_TM)

    e_flat = rt[:, ROUTE_E1:ROUTE_E2 + 1].astype(jnp.int32).reshape(2 * T)
    dest, item_e, item_r, item_n, n_blocks = _moe_layout(e_flat)
    n_rows = (2 * T // MOE_ALIGN + N_EXPERTS) * MOE_ALIGN + MOE_TM
    xs = _dispatch(h2, dest, n_rows)
    ys = _moe(xs, item_e, item_r, item_n, n_blocks,
              w_expert_gate[0], w_expert_up[0], w_expert_down[0])

    out = _final(x2, ys, dest, rt, final_norm_g.reshape(1, D))
    return out.reshape(B, S, D)
```

```python
import functools
import math

import jax
import jax.numpy as jnp
import numpy as np
from jax import lax
from jax.experimental import pallas as pl
from jax.experimental.pallas import tpu as pltpu

F32 = jnp.float32
BF16 = jnp.bfloat16

HG_HEADS = 8
HG_DIM = 128
HG_WIDTH = HG_HEADS * HG_DIM
HG_CHUNK = 64
NSA_HEADS = 16
NSA_GROUPS = 4
NSA_HPG = NSA_HEADS // NSA_GROUPS
NSA_DH = 64
NSA_WIDTH = NSA_HEADS * NSA_DH
NSA_KV = NSA_GROUPS * NSA_DH
CMP_LEN = 32
CMP_STRIDE = 16
CMP_HIDDEN = 256
SLC_LEN = 64
SLC_TOPK = 16
WIN = 512
REL_BUCKETS = 32
REL_MAX_DIST = 128
N_EXPERT_GROUPS = 8
EXPERTS_PER_GROUP = 8
N_EXPERTS = N_EXPERT_GROUPS * EXPERTS_PER_GROUP
EXPERT_DFF = 1024
RMS_EPS = 1e-6
NEG_INF = -1e30
LOG2E = math.log2(math.e)
NSA_Q_SCALE = NSA_DH ** -0.5 * LOG2E
BIG = 1e9

MAIN_COLS = 4 * HG_WIDTH + NSA_WIDTH + 6 * NSA_KV
GATE_COLS = 3 * NSA_HEADS
LANES = 128
VMEM_LIMIT = 56 * 1024 * 1024

INPROJ_TM, INPROJ_TN = 1024, 512
HGRN_TB = 1024
OUTPROJ_TM = 512


def _cparams(*sem):
    return pltpu.CompilerParams(dimension_semantics=sem, vmem_limit_bytes=VMEM_LIMIT)


def _nt(a, b):
    return lax.dot_general(a, b, (((1,), (1,)), ((), ())), preferred_element_type=F32)


def _tn(a, b):
    return lax.dot_general(a, b, (((0,), (0,)), ((), ())), preferred_element_type=F32)


def _dot(a, b):
    return jnp.dot(a, b, preferred_element_type=F32)


def _pack_bf16_pairs(x):
    half = x.shape[1] // 2
    bits = pltpu.bitcast(x.astype(BF16).astype(F32), jnp.int32)
    return lax.shift_right_logical(bits[:, :half], 16) | (bits[:, half:] & jnp.int32(-65536))


def _unpack_bf16_pairs(bits):
    return (pltpu.bitcast(lax.shift_left(bits, 16), F32),
            pltpu.bitcast(bits & jnp.int32(-65536), F32))


def _inproj_body(x_ref, g_ref, w_ref, wg_ref, oh_ref, on_ref, og_ref, h_scr, *, hg_tiles, q_tiles):
    j = pl.program_id(1)

    @pl.when(j == 0)
    def _():
        x = x_ref[...]
        y = x * lax.rsqrt(jnp.mean(x * x, axis=-1, keepdims=True) + RMS_EPS) * g_ref[...]
        h_scr[...] = y.astype(BF16)
        og_ref[...] = _dot(h_scr[...], wg_ref[...])

    @pl.when(j < hg_tiles)
    def _():
        oh_ref[...] = _dot(h_scr[...], w_ref[...])

    @pl.when(j >= hg_tiles)
    def _():
        scale = jnp.where(j < hg_tiles + q_tiles, NSA_Q_SCALE, 1.0)
        on_ref[...] = (_dot(h_scr[...], w_ref[...]) * scale).astype(on_ref.dtype)


def _inproj(x2d, g, w_main, w_gate, tm, tn):
    T, D = x2d.shape
    N = w_main.shape[1]
    n_hg = 4 * HG_WIDTH
    hg_tiles = n_hg // tn
    return pl.pallas_call(
        functools.partial(_inproj_body, hg_tiles=hg_tiles, q_tiles=NSA_WIDTH // tn),
        grid=(T // tm, N // tn),
        in_specs=[
            pl.BlockSpec((tm, D), lambda i, j: (i, 0)),
            pl.BlockSpec((1, D), lambda i, j: (0, 0)),
            pl.BlockSpec((D, tn), lambda i, j: (0, j)),
            pl.BlockSpec((D, LANES), lambda i, j: (0, 0)),
        ],
        out_specs=[
            pl.BlockSpec((tm, tn), lambda i, j: (i, jnp.minimum(j, hg_tiles - 1))),
            pl.BlockSpec((tm, tn), lambda i, j: (i, jnp.maximum(j - hg_tiles, 0))),
            pl.BlockSpec((tm, LANES), lambda i, j: (i, 0)),
        ],
        out_shape=[jax.ShapeDtypeStruct((T, n_hg), F32),
                   jax.ShapeDtypeStruct((T, N - n_hg), BF16),
                   jax.ShapeDtypeStruct((T, LANES), F32)],
        scratch_shapes=[pltpu.VMEM((tm, D), BF16)],
        compiler_params=_cparams("arbitrary", "arbitrary"),
        name="inproj",
    )(x2d, g, w_main, w_gate)


def _hgrn_tables(C):
    t = np.arange(C)
    levels = []
    m = C // 2
    while m >= 1:
        levels.append(m)
        m //= 2
    rows = [np.tril(np.ones((C, C), np.float32)),
            np.triu(np.ones((C, C), np.float32), 1)]
    lev = np.full((C, C), -1, np.int32)
    for li, m in enumerate(levels):
        r = t % (2 * m)
        mid = t - r + m - 1
        M = np.zeros((C, C), np.float32)
        for i in range(C):
            if r[i] >= m:
                M[i, mid[i] + 1:i + 1] = 1.0
            else:
                M[i, i + 1:mid[i] + 1] = 1.0
        rows.append(M)
        same = (t[:, None] // (2 * m)) == (t[None, :] // (2 * m))
        lev[same & (r[:, None] >= m) & (r[None, :] < m)] = li
    lev[t[:, None] == t[None, :]] = len(levels)
    return np.concatenate(rows, 0), lev, len(levels)


HG_HEADS_PER_STEP = 8


def _hgrn_body(q_ref, f_ref, i_ref, g_ref, lb_ref, ng_ref, mst_ref, lev_ref, o_ref, st_ref,
               *, C, n_lev):
    @pl.when(pl.program_id(2) == 0)
    def _():
        st_ref[...] = jnp.zeros_like(st_ref)

    ng = ng_ref[...]
    mst = mst_ref[...]
    lev = lev_ref[...]
    n_chunks = q_ref.shape[0] // C

    def gates(rows, hb):
        cols = slice(hb * HG_DIM, (hb + 1) * HG_DIM)
        lb = lb_ref[hb]
        q = jax.nn.silu(q_ref[rows, cols])
        f = lb + (1.0 - lb) * jax.nn.sigmoid(f_ref[rows, cols])
        lf = jnp.log(f)
        lf0 = lf.astype(BF16)
        r1 = lf - lf0.astype(F32)
        lf1 = r1.astype(BF16)
        lf2 = (r1 - lf1.astype(F32)).astype(BF16)
        return q, 1.0 - f, i_ref[rows, cols].astype(BF16), jnp.concatenate([lf0, lf1, lf2], axis=0)

    def exponents(lf_a, lf_b):
        e = _dot(mst, jnp.concatenate([lf_a, lf_b], axis=1))
        return e[:, 0:HG_DIM], e[:, HG_DIM:2 * HG_DIM]

    def scores(q, k, gdec):
        a = jnp.where(lev == n_lev, _nt(q.astype(BF16), k.astype(BF16)), 0.0)
        for li in range(n_lev):
            g_m = gdec[(2 + li) * C:(3 + li) * C]
            a_m = _nt((q * g_m).astype(BF16), (k * g_m).astype(BF16))
            a = jnp.where(lev == li, a_m, a)
        return a.astype(BF16)

    def finish(rows, hb, q, k, v, gdec, a):
        cols = slice(hb * HG_DIM, (hb + 1) * HG_DIM)
        g_b = gdec[0:C]
        g_r = gdec[C:2 * C]
        st = st_ref[hb]
        o = _nt((q * g_b).astype(BF16), st.astype(BF16)) + _dot(a, v)
        g_last = g_b[C - 1:C, :]
        st_ref[hb] = st * g_last + _tn(v, (k * g_r).astype(BF16))
        o = o * lax.rsqrt(jnp.mean(o * o, axis=-1, keepdims=True) + RMS_EPS) * ng
        o_ref[rows, cols] = (o * jax.nn.silu(g_ref[rows, cols])).astype(o_ref.dtype)

    def chunk(c, carry):
        rows = pl.ds(pl.multiple_of(c * C, C), C)
        hs = range(HG_HEADS_PER_STEP)
        qkve = [gates(rows, hb) for hb in hs]
        e = []
        for hb in range(0, HG_HEADS_PER_STEP, 2):
            e.extend(exponents(qkve[hb][3], qkve[hb + 1][3]))
        gdec = [jnp.exp(x) for x in e]
        a = [scores(qkve[hb][0], qkve[hb][1], gdec[hb]) for hb in hs]
        for hb in hs:
            finish(rows, hb, qkve[hb][0], qkve[hb][1], qkve[hb][2], gdec[hb], a[hb])
        return carry

    lax.fori_loop(0, n_chunks, chunk, 0)


def _hgrn(proj, lb, ng, B, S, tb):
    C = HG_CHUNK
    mst, lev, n_lev = _hgrn_tables(C)
    mst = np.concatenate([mst] * 3, axis=1)
    nb = S // tb
    HB = HG_HEADS_PER_STEP
    hsteps = HG_HEADS // HB

    def col(group):
        return pl.BlockSpec((tb, HB * HG_DIM), lambda b, h, t: (b * nb + t, group * hsteps + h))

    return pl.pallas_call(
        functools.partial(_hgrn_body, C=C, n_lev=n_lev),
        grid=(B, hsteps, nb),
        in_specs=[
            col(0), col(1), col(2), col(3),
            pl.BlockSpec((HB, 1, HG_DIM), lambda b, h, t: (h, 0, 0)),
            pl.BlockSpec((1, HG_DIM), lambda b, h, t: (0, 0)),
            pl.BlockSpec(mst.shape, lambda b, h, t: (0, 0)),
            pl.BlockSpec(lev.shape, lambda b, h, t: (0, 0)),
        ],
        out_specs=pl.BlockSpec((tb, HB * HG_DIM), lambda b, h, t: (b * nb + t, h)),
        out_shape=jax.ShapeDtypeStruct((B * S, HG_WIDTH), BF16),
        scratch_shapes=[pltpu.VMEM((HB, HG_DIM, HG_DIM), F32)],
        compiler_params=_cparams("arbitrary", "arbitrary", "arbitrary"),
        name="hgrn2",
    )(proj, proj, proj, proj, lb, ng, jnp.asarray(mst, BF16), jnp.asarray(lev))


def _compress_body(h_ref, pe_ref, w1_ref, w2_ref, o_ref):
    hv = h_ref[...]
    half = hv.shape[1]
    nh = hv.shape[0]
    u = _dot((hv + pe_ref[0:1, :]).astype(BF16), w1_ref[0:half, :].astype(BF16))
    v = _dot((hv + pe_ref[1:2, :]).astype(BF16), w1_ref[half:2 * half, :].astype(BF16))
    pre = u + pltpu.roll(v, nh - 1, axis=0)
    o_ref[...] = _dot(jax.nn.silu(pre).astype(BF16), w2_ref[...].astype(BF16)).astype(o_ref.dtype)


def _compress(halves, pe, w1, w2):
    _, B, G, NH, HW = halves.shape
    return pl.pallas_call(
        _compress_body,
        grid=(2, B, G),
        in_specs=[
            pl.BlockSpec((None, None, None, NH, HW), lambda s, b, g: (s, b, g, 0, 0)),
            pl.BlockSpec((None, 2, HW), lambda s, b, g: (s, 0, 0)),
            pl.BlockSpec((None, 2 * HW, CMP_HIDDEN), lambda s, b, g: (s, 0, 0)),
            pl.BlockSpec((None, CMP_HIDDEN, NSA_DH), lambda s, b, g: (s, 0, 0)),
        ],
        out_specs=pl.BlockSpec((None, None, None, NH, NSA_DH), lambda s, b, g: (s, b, g, 0, 0)),
        out_shape=jax.ShapeDtypeStruct((2, B, G, NH, NSA_DH), BF16),
        compiler_params=_cparams("arbitrary", "arbitrary", "arbitrary"),
        name="nsa_compress",
    )(halves, pe, w1, w2)


def _nsa_compress_all(proj, pe_k, w1_k, w2_k, pe_v, w1_v, w2_v, B, S):
    NH = S // CMP_STRIDE
    c0 = NSA_WIDTH

    def halves(idx):
        t = proj[:, c0 + idx * NSA_KV:c0 + (idx + 1) * NSA_KV]
        t = t.reshape(B, NH, CMP_STRIDE, NSA_GROUPS, NSA_DH).transpose(0, 3, 1, 2, 4)
        return t.reshape(B, NSA_GROUPS, NH, CMP_STRIDE * NSA_DH)

    hw = CMP_STRIDE * NSA_DH
    return _compress(jnp.stack([halves(0), halves(1)]),
                     jnp.stack([pe_k.reshape(2, hw), pe_v.reshape(2, hw)]),
                     jnp.stack([w1_k, w1_v]), jnp.stack([w2_k, w2_v]))


NSA_TQ = 128


NSA_MASK_ROWS = 64
NSA_WIN_MASK_ROWS = 16
NSA_FAR = 4
NSA_SLC_PAD = (NSA_FAR - 1) * NSA_TQ


NSA_V_ROWS = NSA_DH + 16


def _softmax_cols(s):
    m = jnp.max(s, axis=0, keepdims=True)
    return m, jnp.exp2(s - m).astype(BF16)


def _normalise(acc):
    return acc[0:NSA_DH] * (1.0 / acc[NSA_DH:NSA_DH + 1])


NSA_GROUPS_PER_STEP = 4


def _nsa_body(qt_ref, gl_ref, kc_ref, vct_ref, ks_ref, vst_ref, kw_ref, vwt_ref,
              bc_ref, bw_ref, bn_ref, ovt_ref, o_ref, *, top_k, n_slc):
    TQ = NSA_TQ
    GG = NSA_GROUPS_PER_STEP
    i = pl.program_id(2)
    ovt = ovt_ref[...]
    jb = lax.broadcasted_iota(jnp.int32, (n_slc, TQ), 0)
    qblk = (i * TQ + lax.broadcasted_iota(jnp.int32, (n_slc, TQ), 1)) // SLC_LEN
    forced = (jb == 0) | (jb == qblk) | (jb == qblk - 1)
    future = jb > qblk
    near = pl.ds(pl.multiple_of(NSA_SLC_PAD + (i - 1) * TQ, TQ), 2 * TQ)

    def heads(a):
        return jnp.concatenate([a] * NSA_HPG, axis=1)

    def queries(g):
        qt = qt_ref[g]
        return jnp.concatenate([qt[h * NSA_DH:(h + 1) * NSA_DH, :] for h in range(NSA_HPG)],
                               axis=1)

    def compressed(g, s):
        m = jnp.maximum(jnp.max(s, axis=0, keepdims=True), 0.1 * NEG_INF)
        p = jnp.exp2(s - m)
        acc = _dot(vct_ref[g], p.astype(BF16))
        inv_l = 1.0 / jnp.maximum(acc[NSA_DH:NSA_DH + 1], 1e-30)
        o_cmp = acc[0:NSA_DH] * inv_l
        pc = p * inv_l
        ps = pc[:, 0:TQ] + pc[:, TQ:2 * TQ] + pc[:, 2 * TQ:3 * TQ] + pc[:, 3 * TQ:4 * TQ]
        ps_hi = ps.astype(BF16)
        ps_lo = (ps - ps_hi.astype(F32)).astype(BF16)
        imp = _dot(ovt, ps_hi) + _dot(ovt, ps_lo)
        return o_cmp, jnp.where(forced, BIG, jnp.where(future, -BIG, imp))

    def window(g, s):
        _, p = _softmax_cols(s)
        return _normalise(_dot(vwt_ref[g, :, win], p))

    def select(imp, q4):
        slabs = []
        for v in range(n_slc // 8):
            slab = imp[8 * v:8 * v + 8, :]
            jbs = jb[8 * v:8 * v + 8, :]
            cnt = jnp.zeros((8, TQ), F32)
            for jp in range(n_slc):
                row = imp[jp:jp + 1, :]
                if jp < 8 * v:
                    beats = row >= slab
                elif jp >= 8 * v + 8:
                    beats = row > slab
                else:
                    beats = (row > slab) | ((row == slab) & (jbs > jp))
                cnt = cnt + jnp.where(beats, 1.0, 0.0)
            slabs.append(jnp.where(cnt < top_k, 0.0, 1.0))
        if n_slc < NSA_MASK_ROWS:
            slabs.append(jnp.ones((NSA_MASK_ROWS - n_slc, TQ), F32))
        unsel = jnp.concatenate(slabs, axis=0)
        return jnp.concatenate([heads(unsel.astype(BF16)), q4], axis=0)

    def near_tiles(g, s):
        m, p = _softmax_cols(s)
        return m, _dot(vst_ref[g, :, near], p)

    win = pl.ds(pl.multiple_of(i * TQ, TQ), WIN + TQ)
    pad_row = jnp.where(lax.broadcasted_iota(jnp.int32, (NSA_WIN_MASK_ROWS, NSA_HPG * TQ), 0) == 0,
                        1.0, 0.0).astype(BF16)
    gs = range(GG)
    q4 = [queries(g) for g in gs]
    nh = kc_ref.shape[1]
    cmp_rows = pl.ds(pl.multiple_of(nh - (TQ // CMP_STRIDE) * i, TQ // CMP_STRIDE), nh)
    s_cmp = [_dot(kc_ref[g], q4[g]) + bc_ref[g, cmp_rows, :] for g in gs]
    s_win = [_dot(kw_ref[g, win, :], jnp.concatenate([pad_row, q4[g]], axis=0)) + bw_ref[g] for g in gs]
    cmp_out = [compressed(g, s_cmp[g]) for g in gs]
    o_win = [window(g, s_win[g]) for g in gs]
    qa = [select(cmp_out[g][1], q4[g]) for g in gs]
    s_near = [_dot(ks_ref[g, near, :], qa[g]) + bn_ref[g] for g in gs]
    start = tuple(near_tiles(g, s_near[g]) for g in gs)

    n_far = i - 1
    n_chunks = jnp.maximum((n_far + NSA_FAR - 1) // NSA_FAR, 0)
    far0 = NSA_SLC_PAD + TQ * (n_far - NSA_FAR * n_chunks)

    def far_step(c, carry):
        keys = pl.ds(pl.multiple_of(far0 + c * NSA_FAR * TQ, TQ), NSA_FAR * TQ)
        out = []
        logits = [_dot(ks_ref[g, keys, :], qa[g]) for g in gs]
        for g in gs:
            m, acc = carry[g]
            s = logits[g]
            m_new = jnp.maximum(m, jnp.max(s, axis=0, keepdims=True))
            p = jnp.exp2(s - m_new).astype(BF16)
            acc = jnp.exp2(m - m_new) * acc + _dot(vst_ref[g, :, keys], p)
            out.append((m_new, acc))
        return tuple(out)

    far = lax.fori_loop(0, n_chunks, far_step, start)

    for g in gs:
        o_slc = _normalise(far[g][1])
        o_cmp = cmp_out[g][0]
        sg = jax.nn.sigmoid(gl_ref[g])
        outs = []
        for h in range(NSA_HPG):
            lanes = slice(h * TQ, (h + 1) * TQ)
            outs.append(sg[h:h + 1, :] * o_cmp[:, lanes]
                        + sg[NSA_HPG + h:NSA_HPG + h + 1, :] * o_slc[:, lanes]
                        + sg[2 * NSA_HPG + h:2 * NSA_HPG + h + 1, :] * o_win[g][:, lanes])
        width = NSA_HPG * NSA_DH
        o_ref[:, g * width:(g + 1) * width] = jnp.concatenate(outs, axis=0).T.astype(o_ref.dtype)


def _rel_bucket(dist):
    n = jnp.maximum(dist, 0)
    max_exact = REL_BUCKETS // 2
    nf = jnp.maximum(n, 1).astype(F32)
    large = max_exact + (jnp.log(nf / max_exact) / math.log(REL_MAX_DIST / max_exact)
                         * (REL_BUCKETS - max_exact)).astype(jnp.int32)
    large = jnp.minimum(large, REL_BUCKETS - 1)
    return jnp.where(n < max_exact, n, large)


def _bias_table(table, dist, valid):
    onehot = (_rel_bucket(jnp.asarray(dist, jnp.int32))[..., None]
              == jnp.arange(REL_BUCKETS, dtype=jnp.int32)).astype(F32)
    b = jnp.einsum('rmk,ghk->ghrm', onehot, table, precision=lax.Precision.HIGHEST)
    return jnp.where(jnp.asarray(valid), b, NEG_INF)


def _relayout_body(x_ref, ms_ref, qt_ref, ks_ref, vst_ref, kw_ref, vwt_ref, *, lead):
    TQ = NSA_TQ
    real = pl.program_id(1) >= lead
    x = x_ref[...]

    def columns(idx):
        lo = NSA_WIDTH + idx * NSA_KV
        return jnp.where(real, x[:, lo:lo + NSA_KV], 0.0)

    @pl.when(real)
    def _():
        qt = x[:, 0:NSA_WIDTH].astype(F32).T.astype(BF16)
        qt_ref[...] = qt.reshape(qt_ref.shape)

    ones = jnp.where((lax.broadcasted_iota(jnp.int32, (NSA_V_ROWS - NSA_DH, TQ), 0) == 0) & real,
                     1.0, 0.0).astype(BF16)
    pad_col = jnp.where((lax.broadcasted_iota(jnp.int32, (TQ, NSA_WIN_MASK_ROWS), 1) == 0) & ~real,
                        NEG_INF, 0.0).astype(BF16)
    ms = ms_ref[...]
    k_slc, k_win = columns(2), columns(4)
    v_slc = columns(3).astype(F32).T.astype(BF16)
    v_win = columns(5).astype(F32).T.astype(BF16)
    for g in range(NSA_GROUPS):
        d = slice(g * NSA_DH, (g + 1) * NSA_DH)
        ks_ref[g] = jnp.concatenate([ms, k_slc[:, d]], axis=1)
        kw_ref[g] = jnp.concatenate([pad_col, k_win[:, d]], axis=1)
        vst_ref[g] = jnp.concatenate([v_slc[d, :], ones], axis=0)
        vwt_ref[g] = jnp.concatenate([v_win[d, :], ones], axis=0)


def _nsa_relayout(proj, slc_masks, B, S):
    TQ = NSA_TQ
    G, DH, SP = NSA_GROUPS, NSA_DH, NSA_SLC_PAD
    nt = S // TQ
    lead = WIN // TQ
    slc_lead = SP // TQ
    tok = lambda u: jnp.maximum(u - lead, 0)
    slc = lambda u: jnp.maximum(u - (lead - slc_lead), 0)
    return pl.pallas_call(
        functools.partial(_relayout_body, lead=lead),
        grid=(B, lead + nt),
        in_specs=[pl.BlockSpec((TQ, proj.shape[1]), lambda b, u: (b * nt + tok(u), 0)),
                  pl.BlockSpec((TQ, NSA_MASK_ROWS), lambda b, u: (slc(u), 0))],
        out_specs=[
            pl.BlockSpec((None, G, NSA_HPG * DH, TQ), lambda b, u: (b, 0, 0, tok(u))),
            pl.BlockSpec((None, G, TQ, NSA_MASK_ROWS + DH), lambda b, u: (b, 0, slc(u), 0)),
            pl.BlockSpec((None, G, NSA_V_ROWS, TQ), lambda b, u: (b, 0, 0, slc(u))),
            pl.BlockSpec((None, G, TQ, NSA_WIN_MASK_ROWS + DH), lambda b, u: (b, 0, u, 0)),
            pl.BlockSpec((None, G, NSA_V_ROWS, TQ), lambda b, u: (b, 0, 0, u)),
        ],
        out_shape=[
            jax.ShapeDtypeStruct((B, G, NSA_HPG * DH, S), BF16),
            jax.ShapeDtypeStruct((B, G, S + SP, NSA_MASK_ROWS + DH), BF16),
            jax.ShapeDtypeStruct((B, G, NSA_V_ROWS, S + SP), BF16),
            jax.ShapeDtypeStruct((B, G, S + WIN, NSA_WIN_MASK_ROWS + DH), BF16),
            jax.ShapeDtypeStruct((B, G, NSA_V_ROWS, S + WIN), BF16),
        ],
        compiler_params=_cparams("arbitrary", "arbitrary"),
        name="nsa_relayout",
    )(proj, slc_masks)


def _nsa(proj, glog, kc, vc, rel_bias, B, S):
    TQ = NSA_TQ
    G, HPG, DH = NSA_GROUPS, NSA_HPG, NSA_DH
    NH = S // CMP_STRIDE
    n_cmp = (S - CMP_LEN) // CMP_STRIDE + 1
    n_slc = S // SLC_LEN
    top_k = min(SLC_TOPK, n_slc)
    nt = S // TQ
    c0 = NSA_WIDTH

    MR = NSA_MASK_ROWS
    assert n_slc % 8 == 0 and (n_slc < MR or (n_slc == MR and top_k < n_slc))

    SP = NSA_SLC_PAD
    pos = np.arange(SP + S) - SP
    cols = np.zeros((SP + S, MR), np.float32)
    cols[:, :n_slc] = ((pos // SLC_LEN)[:, None] == np.arange(n_slc)[None, :]) & (pos >= 0)[:, None]
    if n_slc < MR:
        cols[:, n_slc] = pos < 0
    else:
        cols[pos < 0, :] = 1.0
    qt, ks, vst, kw, vwt = _nsa_relayout(proj, jnp.asarray(cols * NEG_INF, BF16), B, S)
    kcb = kc
    ones = jnp.zeros((NSA_V_ROWS - DH, NH), BF16).at[0].set(1.0)
    vct = jnp.concatenate([vc.transpose(0, 1, 3, 2), jnp.broadcast_to(ones, (B, G) + ones.shape)],
                          axis=2)

    table = rel_bias.T.reshape(G, HPG, REL_BUCKETS) * LOG2E
    far = table[:, :, REL_BUCKETS - 1]
    r = np.arange(TQ)[:, None]

    def keys_major(b):
        return b.transpose(0, 3, 1, 2).reshape(G, b.shape[3], HPG * TQ)

    d_c = r - (CMP_LEN - 1) - CMP_STRIDE * (np.arange(2 * NH)[None, :] - NH)
    bias_cmp = keys_major(_bias_table(table, d_c, d_c >= 0))
    d_w = r + WIN - np.arange(WIN + TQ)[None, :]
    bias_win = keys_major(_bias_table(table, d_w, (d_w >= 0) & (d_w < WIN)))
    d_n = r + TQ - np.arange(2 * TQ)[None, :]
    bias_near = keys_major(_bias_table(table - far[:, :, None], d_n, d_n >= 0))

    ci = np.arange(NH)[None, :] * CMP_STRIDE
    sj = np.arange(n_slc)[:, None] * SLC_LEN
    overlap_t = ((ci <= sj + SLC_LEN - 1) & (ci + CMP_LEN - 1 >= sj) & (np.arange(NH)[None, :] < n_cmp))

    gate_rows = 16
    glog_g = glog[:, :GATE_COLS].reshape(B * S, 3, G, HPG).transpose(2, 1, 3, 0).reshape(G, 3 * HPG, B * S)
    glog_g = jnp.pad(glog_g, ((0, 0), (0, gate_rows - 3 * HPG), (0, 0)))

    GG = NSA_GROUPS_PER_STEP
    whole = lambda shape: pl.BlockSpec((None, GG) + shape, lambda b, g, i: (b, g, 0, 0))
    tile = lambda k: pl.BlockSpec((GG, k, HPG * TQ), lambda b, g, i: (g, 0, 0))
    return pl.pallas_call(
        functools.partial(_nsa_body, top_k=top_k, n_slc=n_slc),
        grid=(B, G // GG, nt),
        in_specs=[
            pl.BlockSpec((None, GG, HPG * DH, TQ), lambda b, g, i: (b, g, 0, i)),
            pl.BlockSpec((GG, gate_rows, TQ), lambda b, g, i: (g, 0, b * nt + i)),
            whole((NH, DH)), whole((NSA_V_ROWS, NH)),
            whole((S + SP, MR + DH)), whole((NSA_V_ROWS, S + SP)),
            whole((S + WIN, NSA_WIN_MASK_ROWS + DH)), whole((NSA_V_ROWS, S + WIN)),
            tile(2 * NH), tile(WIN + TQ), tile(2 * TQ),
            pl.BlockSpec((n_slc, NH), lambda b, g, i: (0, 0)),
        ],
        out_specs=pl.BlockSpec((TQ, GG * HPG * DH), lambda b, g, i: (b * nt + i, g)),
        out_shape=jax.ShapeDtypeStruct((B * S, NSA_WIDTH), BF16),
        compiler_params=_cparams("arbitrary", "arbitrary", "arbitrary"),
        name="nsa_attention",
    )(qt, glog_g, kcb, vct, ks, vst, kw, vwt, bias_cmp, bias_win, bias_near,
      jnp.asarray(overlap_t, BF16))


ROUTE_E1, ROUTE_E2, ROUTE_W1, ROUTE_W2 = 0, 1, 2, 3


def _outproj_body(x_ref, yh_ref, yn_ref, wo_ref, g2_ref, wrh_ref, wrl_ref, br_ref,
                  x2_ref, h2_ref, rt_ref):
    x2 = (x_ref[...] + _dot(yh_ref[...], wo_ref[0:HG_WIDTH, :])
          + _dot(yn_ref[...], wo_ref[HG_WIDTH:HG_WIDTH + NSA_WIDTH, :]))
    x2_ref[...] = x2
    h = x2 * lax.rsqrt(jnp.mean(x2 * x2, axis=-1, keepdims=True) + RMS_EPS) * g2_ref[...]
    hb = h.astype(BF16)
    h2_ref[...] = _pack_bf16_pairs(h)
    hl = (h - hb.astype(F32)).astype(BF16)
    wrh = wrh_ref[...]
    lg = _dot(hb, wrh) + _dot(hl, wrh) + _dot(hb, wrl_ref[...]) + br_ref[...]
    lane = lax.broadcasted_iota(jnp.int32, lg.shape, 1)
    first = lambda hit: jnp.min(jnp.where(hit, lane, LANES), axis=-1, keepdims=True)
    lgg = jnp.where(lane < N_EXPERT_GROUPS, lg, NEG_INF)
    mg = jnp.max(lgg, axis=-1, keepdims=True)
    p_top = 1.0 / jnp.sum(jnp.exp(lgg - mg), axis=-1, keepdims=True)
    lo = N_EXPERT_GROUPS + EXPERTS_PER_GROUP * first(lgg == mg)
    le = jnp.where((lane >= lo) & (lane < lo + EXPERTS_PER_GROUP), lg, NEG_INF)
    v1 = jnp.max(le, axis=-1, keepdims=True)
    i1 = first(le == v1)
    le = jnp.where(lane == i1, NEG_INF, le)
    v2 = jnp.max(le, axis=-1, keepdims=True)
    i2 = first(le == v2)
    e21 = jnp.exp(v2 - v1)
    w1 = p_top / (1.0 + e21)
    rt = jnp.where(lane == ROUTE_E1, (i1 - N_EXPERT_GROUPS).astype(F32), 0.0)
    rt = jnp.where(lane == ROUTE_E2, (i2 - N_EXPERT_GROUPS).astype(F32), rt)
    rt = jnp.where(lane == ROUTE_W1, w1, rt)
    rt_ref[...] = jnp.where(lane == ROUTE_W2, w1 * e21, rt)


def _outproj(x2d, y_hg, y_nsa, w_out, g2, wr_hi, wr_lo, b_r, tm):
    T, D = x2d.shape
    row = lambda n: pl.BlockSpec((tm, n), lambda i: (i, 0))
    full = lambda a: pl.BlockSpec(a.shape, lambda i: (0, 0))
    return pl.pallas_call(
        _outproj_body,
        grid=(T // tm,),
        in_specs=[row(D), row(HG_WIDTH), row(NSA_WIDTH), full(w_out), full(g2),
                  full(wr_hi), full(wr_lo), full(b_r)],
        out_specs=[row(D), row(D // 2), row(LANES)],
        out_shape=[jax.ShapeDtypeStruct((T, D), F32), jax.ShapeDtypeStruct((T, D // 2), jnp.int32),
                   jax.ShapeDtypeStruct((T, LANES), F32)],
        compiler_params=_cparams("arbitrary"),
        name="outproj_router",
    )(x2d, y_hg, y_nsa, w_out, g2, wr_hi, wr_lo, b_r)


ROW_TILE = 512


def _dispatch_body(dest_ref, h_ref, xs0_hbm, xs_hbm, sem):
    del xs0_hbm
    tb = h_ref.shape[0]

    def start(t, carry):
        for k in range(2):
            pltpu.make_async_copy(h_ref.at[pl.ds(t, 1), :],
                                  xs_hbm.at[pl.ds(dest_ref[0, 2 * t + k], 1), :], sem).start(priority=k)
        return carry

    lax.fori_loop(0, tb, start, 0, unroll=16)
    for _ in range(2):
        pltpu.make_async_copy(h_ref, xs_hbm.at[pl.ds(0, tb), :], sem).wait()


def _dispatch(h2p, dest, n_rows):
    T, W = h2p.shape
    tb = ROW_TILE
    return pl.pallas_call(
        _dispatch_body,
        grid=(T // tb,),
        in_specs=[pl.BlockSpec((None, 1, 2 * tb), lambda i: (i, 0, 0), memory_space=pltpu.SMEM),
                  pl.BlockSpec((tb, W), lambda i: (i, 0)),
                  pl.BlockSpec(memory_space=pl.ANY)],
        out_specs=pl.BlockSpec(memory_space=pl.ANY),
        out_shape=jax.ShapeDtypeStruct((n_rows, W), jnp.int32),
        scratch_shapes=[pltpu.SemaphoreType.DMA(())],
        input_output_aliases={2: 0},
        compiler_params=_cparams("arbitrary"),
        name="moe_dispatch",
    )(dest.reshape(T // tb, 1, 2 * tb), h2p, jnp.zeros((n_rows, W), jnp.int32))


MOE_ALIGN = 128
MOE_TM = 1024
MOE_SUB = 256
MOE_TF = 256
MOE_CAST_PIECES = 4
MOE_WEIGHT_BUFFERS = 3


def _moe_body(ie_ref, ir_ref, in_ref, nb_ref, xs_hbm, wg_hbm, wu_hbm, wd_hbm, y_hbm,
              xbuf, xlo, xhi, acc, ypk, wgb, wub, wdb, wgf, wuf, wdf, sem_in, sem_out, sem_w):
    w = pl.program_id(0)
    f = pl.program_id(1)
    n_w = pl.num_programs(0)
    n_f = pl.num_programs(1)
    last_f = n_f - 1
    ring = wgf.shape[0]
    step = w * n_f + f

    def weight_copies(k):
        item = jnp.minimum(k // n_f, n_w - 1)
        wanted = (k < n_w * n_f) & (in_ref[item] > 0)
        e = ie_ref[item]
        cols = pl.ds(pl.multiple_of((k % n_f) * MOE_TF, MOE_TF), MOE_TF)
        s = k % ring
        return wanted, (pltpu.make_async_copy(wg_hbm.at[e, :, cols], wgf.at[s], sem_w.at[0, s]),
                        pltpu.make_async_copy(wu_hbm.at[e, :, cols], wuf.at[s], sem_w.at[1, s]),
                        pltpu.make_async_copy(wd_hbm.at[e, cols, :], wdf.at[s], sem_w.at[2, s]))

    def request_weights(k):
        wanted, copies = weight_copies(k)

        @pl.when(wanted)
        def _():
            for c in copies:
                c.start()

    @pl.when(step == 0)
    def _():
        for k in range(1, ring):
            request_weights(k)

    request_weights(step + ring)
    cast_next, arriving = weight_copies(step + 1)

    @pl.when(cast_next)
    def _():
        for c in arriving:
            c.wait()

    nslot = (step + 1) % ring
    nsub = in_ref[w]
    row0 = pl.multiple_of(ir_ref[w] * MOE_ALIGN, MOE_ALIGN)
    slot = w % 2
    per_mm = MOE_SUB // MOE_ALIGN
    n_full = nsub // per_mm
    n_iter = n_full + nsub % per_mm
    half = xlo.shape[1]
    cur = step % 2

    def cast_piece(p):
        rk = wgf.shape[1] // MOE_CAST_PIECES
        rows = pl.ds(pl.multiple_of(p * rk, rk), rk)
        wgb[1 - cur, rows, :] = wgf[nslot, rows, :].astype(BF16)
        wub[1 - cur, rows, :] = wuf[nslot, rows, :].astype(BF16)
        rf = wdf.shape[1] // MOE_CAST_PIECES
        rows = pl.ds(pl.multiple_of(p * rf, rf), rf)
        wdb[1 - cur, rows, :] = wdf[nslot, rows, :].astype(BF16)

    def x_copy(item, s):
        r = pl.multiple_of(ir_ref[item] * MOE_ALIGN, MOE_ALIGN)
        return pltpu.make_async_copy(xs_hbm.at[pl.ds(r, MOE_TM), :], xbuf.at[s], sem_in.at[s])

    def y_copy(r, dst_row):
        return pltpu.make_async_copy(ypk.at[pl.ds(r, MOE_ALIGN), :],
                                     y_hbm.at[pl.ds(dst_row, MOE_ALIGN), :], sem_out)

    def for_row_blocks(fn):
        def body(j, carry):
            fn(pl.multiple_of(j * MOE_SUB, MOE_SUB), MOE_SUB, j)
            return carry

        lax.fori_loop(0, n_full, body, 0)

        @pl.when(nsub % per_mm == 1)
        def _():
            fn(pl.multiple_of(n_full * MOE_SUB, MOE_ALIGN), MOE_ALIGN, n_full)

    @pl.when((f == 0) & (nsub > 0))
    def _():
        @pl.when(in_ref[jnp.maximum(w - 1, 0)] == 0)
        def _():
            x_copy(w, slot).start()

        x_copy(w, slot).wait()

        def unpack(r, size, _):
            lo, hi = _unpack_bf16_pairs(xbuf[slot, pl.ds(r, size), :])
            xlo[pl.ds(r, size), :] = lo.astype(BF16)
            xhi[pl.ds(r, size), :] = hi.astype(BF16)

        for_row_blocks(unpack)

    def experts(first, last):
        def block(r, size, it):
            rows = pl.ds(r, size)
            lo = xlo[rows, :]
            hi = xhi[rows, :]
            hg = _dot(lo, wgb[cur, 0:half, :]) + _dot(hi, wgb[cur, half:2 * half, :])
            hu = _dot(lo, wub[cur, 0:half, :]) + _dot(hi, wub[cur, half:2 * half, :])
            y = _dot((jax.nn.silu(hg) * hu).astype(BF16), wdb[cur])
            if first:
                acc[rows, :] = y
            elif not last:
                acc[rows, :] += y
            else:
                ypk[rows, :] = _pack_bf16_pairs(acc[rows, :] + y)
                for k in range(size // MOE_ALIGN):
                    y_copy(r + k * MOE_ALIGN, row0 + r + k * MOE_ALIGN).start()

            @pl.when(cast_next & (it < MOE_CAST_PIECES))
            def _():
                cast_piece(it)

        for_row_blocks(block)

        @pl.when(cast_next)
        def _():
            lax.fori_loop(jnp.minimum(n_iter, MOE_CAST_PIECES), MOE_CAST_PIECES,
                          lambda p, c: (cast_piece(p), c)[1], 0)

    @pl.when((nsub == 0) & cast_next)
    def _():
        lax.fori_loop(0, MOE_CAST_PIECES, lambda p, c: (cast_piece(p), c)[1], 0)

    @pl.when((f == 0) & (nsub > 0))
    def _():
        experts(True, False)

    @pl.when((f > 0) & (f < last_f) & (nsub > 0))
    def _():
        experts(False, False)

    @pl.when((f == last_f) & (nsub > 0))
    def _():
        nxt = jnp.minimum(w + 1, n_w - 1)

        @pl.when((w + 1 < n_w) & (in_ref[nxt] > 0))
        def _():
            x_copy(nxt, 1 - slot).start()

        experts(False, True)

        def wait(j, carry):
            y_copy(0, 0).wait()
            return carry

        lax.fori_loop(0, nsub, wait, 0)

    @pl.when((f == last_f) & (w == n_w - 1))
    def _():
        ypk[0:MOE_ALIGN, :] = jnp.zeros((MOE_ALIGN, ypk.shape[1]), jnp.int32)
        used = nb_ref[0]
        total = y_hbm.shape[0] // MOE_ALIGN

        def fill(j, carry):
            y_copy(0, pl.multiple_of(j * MOE_ALIGN, MOE_ALIGN)).start()
            return carry

        def wait(j, carry):
            y_copy(0, 0).wait()
            return carry

        lax.fori_loop(used, total, fill, 0)
        lax.fori_loop(used, total, wait, 0)


def _moe(xs, item_e, item_r, item_n, n_blocks, w_gate, w_up, w_down):
    rows = xs.shape[0]
    D = 2 * xs.shape[1]
    n_items = item_e.shape[0]
    nf = EXPERT_DFF // MOE_TF
    assert nf >= 2 and MOE_SUB == 2 * MOE_ALIGN

    R = MOE_WEIGHT_BUFFERS
    return pl.pallas_call(
        _moe_body,
        grid_spec=pltpu.PrefetchScalarGridSpec(
            num_scalar_prefetch=4,
            grid=(n_items, nf),
            in_specs=[
                pl.BlockSpec(memory_space=pl.ANY),
                pl.BlockSpec(memory_space=pl.ANY),
                pl.BlockSpec(memory_space=pl.ANY),
                pl.BlockSpec(memory_space=pl.ANY),
            ],
            out_specs=pl.BlockSpec(memory_space=pl.ANY),
            scratch_shapes=[pltpu.VMEM((2, MOE_TM, D // 2), jnp.int32),
                            pltpu.VMEM((MOE_TM, D // 2), BF16), pltpu.VMEM((MOE_TM, D // 2), BF16),
                            pltpu.VMEM((MOE_TM, D), F32), pltpu.VMEM((MOE_TM, D // 2), jnp.int32),
                            pltpu.VMEM((2, D, MOE_TF), BF16), pltpu.VMEM((2, D, MOE_TF), BF16),
                            pltpu.VMEM((2, MOE_TF, D), BF16),
                            pltpu.VMEM((R, D, MOE_TF), F32), pltpu.VMEM((R, D, MOE_TF), F32),
                            pltpu.VMEM((R, MOE_TF, D), F32),
                            pltpu.SemaphoreType.DMA((2,)), pltpu.SemaphoreType.DMA(()),
                            pltpu.SemaphoreType.DMA((3, R))],
        ),
        out_shape=jax.ShapeDtypeStruct((rows, D // 2), jnp.int32),
        compiler_params=_cparams("arbitrary", "arbitrary"),
        name="moe_experts",
    )(item_e, item_r, item_n, n_blocks, xs, w_gate, w_up, w_down)


def _final_body(dcur_ref, dnext_ref, x2_ref, rt_ref, g_ref, ys_hbm, o_ref, ybuf, sems):
    i = pl.program_id(0)
    tb = x2_ref.shape[0]
    slot = i % 2

    def gather(d_ref, s):
        def body(t, carry):
            for k in range(2):
                pltpu.make_async_copy(ys_hbm.at[pl.ds(d_ref[0, 2 * t + k], 1), :],
                                      ybuf.at[s, k, pl.ds(t, 1), :], sems.at[s]).start(priority=k)
            return carry

        lax.fori_loop(0, tb, body, 0, unroll=16)

    @pl.when(i == 0)
    def _():
        gather(dcur_ref, 0)

    @pl.when(i + 1 < pl.num_programs(0))
    def _():
        gather(dnext_ref, 1 - slot)

    for k in range(2):
        pltpu.make_async_copy(ys_hbm.at[pl.ds(0, tb), :], ybuf.at[slot, k], sems.at[slot]).wait()
    rt = rt_ref[...]
    lo1, hi1 = _unpack_bf16_pairs(ybuf[slot, 0])
    lo2, hi2 = _unpack_bf16_pairs(ybuf[slot, 1])
    w1 = rt[:, ROUTE_W1:ROUTE_W1 + 1]
    w2 = rt[:, ROUTE_W2:ROUTE_W2 + 1]
    x = x2_ref[...] + jnp.concatenate([w1 * lo1 + w2 * lo2, w1 * hi1 + w2 * hi2], axis=1)
    o_ref[...] = x * lax.rsqrt(jnp.mean(x * x, axis=-1, keepdims=True) + RMS_EPS) * g_ref[...]


def _final(x2, ys, dest, rt, g):
    T, D = x2.shape
    tb = ROW_TILE
    nb = T // tb
    dest3 = dest.reshape(nb, 1, 2 * tb)
    return pl.pallas_call(
        _final_body,
        grid=(nb,),
        in_specs=[pl.BlockSpec((None, 1, 2 * tb), lambda i: (i, 0, 0), memory_space=pltpu.SMEM),
                  pl.BlockSpec((None, 1, 2 * tb), lambda i: (jnp.minimum(i + 1, nb - 1), 0, 0),
                               memory_space=pltpu.SMEM),
                  pl.BlockSpec((tb, D), lambda i: (i, 0)),
                  pl.BlockSpec((tb, LANES), lambda i: (i, 0)),
                  pl.BlockSpec((1, D), lambda i: (0, 0)),
                  pl.BlockSpec(memory_space=pl.ANY)],
        out_specs=pl.BlockSpec((tb, D), lambda i: (i, 0)),
        out_shape=jax.ShapeDtypeStruct((T, D), F32),
        scratch_shapes=[pltpu.VMEM((2, 2, tb, D // 2), jnp.int32), pltpu.SemaphoreType.DMA((2,))],
        compiler_params=_cparams("arbitrary"),
        name="combine_final_norm",
    )(dest3, dest3, x2, rt, g, ys)


def _moe_layout(e_flat):
    A = e_flat.shape[0]
    onehot = (e_flat[:, None] == jnp.arange(N_EXPERTS, dtype=jnp.int32)[None, :]).astype(jnp.int32)
    csum = jnp.cumsum(onehot, axis=0)
    counts = csum[-1]
    nblk = (counts + MOE_ALIGN - 1) // MOE_ALIGN
    blk0 = jnp.cumsum(nblk) - nblk
    dest = jnp.sum(onehot * (blk0[None, :] * MOE_ALIGN + csum - 1), axis=1).astype(jnp.int32)
    per_item = MOE_TM // MOE_ALIGN
    nitem = (nblk + per_item - 1) // per_item
    iend = jnp.cumsum(nitem)
    n_items = (A // MOE_ALIGN + N_EXPERTS + N_EXPERTS * (per_item - 1)) // per_item
    w = jnp.arange(n_items, dtype=jnp.int32)
    total = iend[-1]
    wv = jnp.minimum(w, total - 1)
    item_e = jnp.minimum(jnp.searchsorted(iend, wv, side='right'), N_EXPERTS - 1).astype(jnp.int32)
    sb = wv - (iend - nitem)[item_e]
    item_r = (blk0[item_e] + sb * per_item).astype(jnp.int32)
    item_n = jnp.where(w < total, jnp.minimum(per_item, nblk[item_e] - sb * per_item), 0).astype(jnp.int32)
    n_blocks = jnp.sum(nblk).astype(jnp.int32).reshape(1)
    lead = lambda a, v: jnp.concatenate([jnp.asarray(v, jnp.int32).reshape(1), a])
    return dest, lead(item_e, item_e[0]), lead(item_r, 0), lead(item_n, 0), n_blocks


def kernel(x, norm1_g, w_in, hg_lb_logits, hg_norm_g, cmp_pe_k, cmp_w1_k, cmp_w2_k, cmp_pe_v,
           cmp_w1_v, cmp_w2_v, rel_bias, w_out, norm2_g, w_router_group, b_router_group,
           w_router_expert, b_router_expert, w_expert_gate, w_expert_up, w_expert_down,
           final_norm_g):
    B, S, D = x.shape
    T = B * S
    assert w_in.shape[0] == 1, "single-layer block"
    x2d = x.reshape(T, D)

    lower = jax.nn.softmax(hg_lb_logits.astype(F32), axis=0)[0].reshape(HG_HEADS, 1, HG_DIM)

    w_main = w_in[0, :, :MAIN_COLS].astype(BF16)
    w_gate_cols = jnp.pad(w_in[0, :, MAIN_COLS:].astype(BF16), ((0, 0), (0, LANES - GATE_COLS)))
    proj_hg, proj_nsa, glog = _inproj(x2d, norm1_g[0].reshape(1, D), w_main, w_gate_cols,
                                      tm=INPROJ_TM, tn=INPROJ_TN)

    y_hg = _hgrn(proj_hg, lower, hg_norm_g[0].reshape(1, HG_DIM), B, S, tb=HGRN_TB)
    kcvc = _nsa_compress_all(proj_nsa, cmp_pe_k[0], cmp_w1_k[0], cmp_w2_k[0],
                             cmp_pe_v[0], cmp_w1_v[0], cmp_w2_v[0], B, S)
    y_nsa = _nsa(proj_nsa, glog, kcvc[0], kcvc[1], rel_bias, B, S)

    w_r = jnp.concatenate([w_router_group[0], w_router_expert[0]], axis=1)
    w_r = jnp.pad(w_r, ((0, 0), (0, LANES - w_r.shape[1])))
    wr_hi = w_r.astype(BF16)
    wr_lo = (w_r - wr_hi.astype(F32)).astype(BF16)
    b_r = jnp.concatenate([b_router_group[0], b_router_expert[0]])
    b_r = jnp.pad(b_r, (0, LANES - b_r.shape[0])).reshape(1, LANES)
    x2, h2, rt = _outproj(x2d, y_hg, y_nsa, w_out[0].astype(BF16), norm2_g[0].reshape(1, D),
                          wr_hi, wr_lo, b_r, tm=OUTPROJ_TM)

    e_flat = rt[:, ROUTE_E1:ROUTE_E2 + 1].astype(jnp.int32).reshape(2 * T)
    dest, item_e, item_r, item_n, n_blocks = _moe_layout(e_flat)
    n_rows = (2 * T // MOE_ALIGN + N_EXPERTS) * MOE_ALIGN + MOE_TM
    xs = _dispatch(h2, dest, n_rows)
    ys = _moe(xs, item_e, item_r, item_n, n_blocks,
              w_expert_gate[0], w_expert_up[0], w_expert_down[0])

    out = _final(x2, ys, dest, rt, final_norm_g.reshape(1, D))
    return out.reshape(B, S, D)
```
